```python
import math
import jax, jax.numpy as jnp
from jax import lax
import numpy as np

D_MODEL = 1024
BATCH = 4
SEQ = 4096
DEPTH = 4

CTX_LEN = 256
GRID_W = 64
N_MIXERS = 3
EPS = 1e-6
ROPE_THETA = 10000.0

GLA_HEADS = 4
GLA_DK = D_MODEL // 2 // GLA_HEADS
GLA_DV = D_MODEL // GLA_HEADS
GLA_GATE_RANK = 16
GLA_TAU = 16.0
GLA_CHUNK = 64

DIFF_HD = 64
DIFF_HEADS = D_MODEL // (2 * DIFF_HD)
DIFF_QBLOCK = 128

SSD_DINNER = 2 * D_MODEL
SSD_HEADDIM = 64
SSD_HEADS = SSD_DINNER // SSD_HEADDIM
SSD_STATE = 128
SSD_GROUPS = 4
SSD_CONV = 5
SSD_CHUNK = 128

PEER_KEYS = 128
PEER_EXPERTS = PEER_KEYS * PEER_KEYS
PEER_HEADS = 8
PEER_DKEY = 256
PEER_TOPK = 16
PEER_CHUNK = 128

kernel_name = "hybrid_gla_diffattn_ssd_peer_dit"


def rmsnorm(x, g):
    xf = x.astype(jnp.float32)
    xf = xf * lax.rsqrt(jnp.mean(jnp.square(xf), axis=-1, keepdims=True) + EPS)
    return (xf * g.astype(jnp.float32)).astype(x.dtype)


def modulate(h, shift, scale):
    return h * (1 + scale) + shift


def split_cols(t, widths):
    out, start = [], 0
    for w in widths:
        out.append(t[..., start:start + w])
        start += w
    return out


def flip(t):
    return jnp.flip(t, axis=1)


def lambda_init(layer_idx):
    return 0.8 - 0.6 * math.exp(-0.3 * layer_idx)


def axial_rope(rows):
    row = jnp.repeat(jnp.arange(rows), GRID_W).astype(jnp.float32)
    col = jnp.tile(jnp.arange(GRID_W), rows).astype(jnp.float32)
    n_freq = DIFF_HD // 4
    freqs = ROPE_THETA ** (-jnp.arange(n_freq, dtype=jnp.float32) / n_freq)
    ang = jnp.concatenate([row[:, None] * freqs, col[:, None] * freqs], axis=-1)
    return jnp.cos(ang), jnp.sin(ang)


def apply_rope(t, cos, sin):
    half = t.shape[-1] // 2
    cs, sn = cos[None, :, None, None, :], sin[None, :, None, None, :]
    t1, t2 = t[..., :half], t[..., half:]
    return jnp.concatenate([t1 * cs - t2 * sn, t1 * sn + t2 * cs], axis=-1).astype(t.dtype)


def gla_chunk_scan(q, k, v, log_a, s0):
    Bn, L, H, DK = q.shape
    DV = v.shape[-1]
    nc = L // GLA_CHUNK

    def to_chunks(t):
        return jnp.moveaxis(t.astype(jnp.float32).reshape(Bn, nc, GLA_CHUNK, H, t.shape[-1]), 1, 0)

    mask = jnp.tril(jnp.ones((GLA_CHUNK, GLA_CHUNK), bool))[None, :, :, None, None]

    def step(s, inp):
        qc, kc, vc, ac = inp
        b = jnp.cumsum(ac, axis=1)
        seg = b[:, :, None] - b[:, None, :]
        decay = jnp.exp(jnp.where(mask, seg, -jnp.inf))
        scores = jnp.sum(qc[:, :, None] * kc[:, None, :] * decay, axis=-1)
        o = jnp.einsum('bijh,bjhv->bihv', scores, vc)
        o = o + jnp.einsum('bihd,bhdv->bihv', qc * jnp.exp(b), s)
        b_last = b[:, -1]
        k_dec = kc * jnp.exp(b_last[:, None] - b)
        s = jnp.exp(b_last)[..., None] * s + jnp.einsum('bjhd,bjhv->bhdv', k_dec, vc)
        return s, o

    s_fin, o = lax.scan(step, s0, (to_chunks(q), to_chunks(k), to_chunks(v), to_chunks(log_a)))
    return jnp.moveaxis(o, 0, 1).reshape(Bn, L, H, DV), s_fin


def gla_project(h, w_in, w_alpha, b_alpha):
    Bn, L, _ = h.shape
    hk, hv = GLA_HEADS * GLA_DK, GLA_HEADS * GLA_DV
    q, k, v, g, lo_f, lo_b = split_cols(h @ w_in, (hk, hk, hv, hv, GLA_GATE_RANK, GLA_GATE_RANK))
    q = q.reshape(Bn, L, GLA_HEADS, GLA_DK) * (GLA_DK ** -0.5)
    k = k.reshape(Bn, L, GLA_HEADS, GLA_DK)
    v = v.reshape(Bn, L, GLA_HEADS, GLA_DV)

    def log_gate(lo, d):
        z = (lo @ w_alpha[d] + b_alpha[d]).astype(jnp.float32)
        return (jax.nn.log_sigmoid(z) / GLA_TAU).reshape(Bn, L, GLA_HEADS, GLA_DK)

    return q, k, v, g, log_gate(lo_f, 0), log_gate(lo_b, 1)


def gla_mixer(h_ctx, h_lat, w_in, w_alpha, b_alpha, norm_g, w_out, ctx_out):
    def scans(proj, s_f0, s_b0):
        q, k, v, g, la_f, la_b = proj
        o_f, s_f = gla_chunk_scan(q, k, v, la_f, s_f0)
        o_b, s_b = gla_chunk_scan(flip(q), flip(k), flip(v), flip(la_b), s_b0)
        return (o_f + flip(o_b)).astype(v.dtype), g, s_f, s_b

    def finish(o, g):
        o = rmsnorm(o, norm_g).reshape(g.shape) * jax.nn.silu(g)
        return o @ w_out

    s0 = jnp.zeros((h_ctx.shape[0], GLA_HEADS, GLA_DK, GLA_DV), jnp.float32)
    o_c, g_c, s_f, s_b = scans(gla_project(h_ctx, w_in, w_alpha, b_alpha), s0, s0)
    o_l, g_l, _, _ = scans(gla_project(h_lat, w_in, w_alpha, b_alpha), s_f, s_b)
    return (finish(o_c, g_c) if ctx_out else None), finish(o_l, g_l)


def diff_project(h, w_in):
    Bn, L, _ = h.shape
    wd = DIFF_HEADS * 2 * DIFF_HD
    q, k, v = split_cols(h @ w_in, (wd, wd, wd))
    return (q.reshape(Bn, L, DIFF_HEADS, 2, DIFF_HD), k.reshape(Bn, L, DIFF_HEADS, 2, DIFF_HD),
            v.reshape(Bn, L, DIFF_HEADS, 2 * DIFF_HD))


def diff_attend(q, k, v, lam):
    Bn, Lq = q.shape[:2]
    nb = Lq // DIFF_QBLOCK
    qb = jnp.moveaxis(q.reshape(Bn, nb, DIFF_QBLOCK, DIFF_HEADS, 2, DIFF_HD), 1, 0)

    def block(qi):
        s = jnp.einsum('bqhmd,bkhmd->bhmqk', qi, k).astype(jnp.float32) * (DIFF_HD ** -0.5)
        p = jax.nn.softmax(s, axis=-1)
        a = p[:, :, 0] - lam * p[:, :, 1]
        return jnp.einsum('bhqk,bkhv->bqhv', a.astype(v.dtype), v)

    o = lax.map(block, qb)
    return jnp.moveaxis(o, 0, 1).reshape(Bn, Lq, DIFF_HEADS, 2 * DIFF_HD)


def diff_mixer(h_ctx, h_lat, w_in, lam_vecs, norm_g, w_out, cos, sin, lam_init, ctx_out):
    lv = lam_vecs.astype(jnp.float32)
    lam = jnp.exp(jnp.sum(lv[0] * lv[1])) - jnp.exp(jnp.sum(lv[2] * lv[3])) + lam_init
    q_c, k_c, v_c = diff_project(h_ctx, w_in)
    q_l, k_l, v_l = diff_project(h_lat, w_in)
    q_l = apply_rope(q_l, cos, sin)
    k_l = apply_rope(k_l, cos, sin)

    def finish(o):
        o = rmsnorm(o, norm_g) * (1 - lam_init)
        return o.reshape(o.shape[0], o.shape[1], -1) @ w_out

    o_l = diff_attend(q_l, jnp.concatenate([k_c, k_l], axis=1), jnp.concatenate([v_c, v_l], axis=1), lam)
    o_c = finish(diff_attend(q_c, k_c, v_c, lam)) if ctx_out else None
    return o_c, finish(o_l)


def dwconv_centred(t, w, b):
    y = lax.conv_general_dilated(t, w[:, None, :].astype(t.dtype), window_strides=(1,),
                                 padding=((SSD_CONV // 2, SSD_CONV // 2),),
                                 dimension_numbers=('NWC', 'WIO', 'NWC'), feature_group_count=t.shape[-1])
    return y + b


def ssd_project(h, w_in, conv_w, conv_b, dt_bias):
    Bn, L, _ = h.shape
    gn = SSD_GROUPS * SSD_STATE
    z, xbc, dt = split_cols(h @ w_in, (SSD_DINNER, SSD_DINNER + 2 * gn, 2 * SSD_HEADS))
    xbc = jax.nn.silu(dwconv_centred(xbc, conv_w, conv_b))
    xs, bm, cm = split_cols(xbc, (SSD_DINNER, gn, gn))
    xs = xs.reshape(Bn, L, SSD_HEADS, SSD_HEADDIM)
    bm = bm.reshape(Bn, L, SSD_GROUPS, SSD_STATE)
    cm = cm.reshape(Bn, L, SSD_GROUPS, SSD_STATE)
    dt = jax.nn.softplus(dt.astype(jnp.float32).reshape(Bn, L, 2, SSD_HEADS) + dt_bias.astype(jnp.float32))
    return z, xs, bm, cm, dt


def ssd_chunk_scan(xs, dt, a, bm, cm, h0):
    Bn, L = xs.shape[:2]
    nc, Q, R = L // SSD_CHUNK, SSD_CHUNK, SSD_HEADS // SSD_GROUPS
    xg = xs.astype(jnp.float32).reshape(Bn, nc, Q, SSD_GROUPS, R, SSD_HEADDIM)
    dg = dt.reshape(Bn, nc, Q, SSD_GROUPS, R)
    bg = bm.astype(jnp.float32).reshape(Bn, nc, Q, SSD_GROUPS, SSD_STATE)
    cg = cm.astype(jnp.float32).reshape(Bn, nc, Q, SSD_GROUPS, SSD_STATE)
    ag = a.reshape(SSD_GROUPS, R)
    mask = jnp.tril(jnp.ones((Q, Q), bool))[None, :, :, None, None]

    def step(h, inp):
        xc, dc, bc, cc = inp
        acum = jnp.cumsum(dc * ag, axis=1)
        seg = acum[:, :, None] - acum[:, None, :]
        lmat = jnp.exp(jnp.where(mask, seg, -jnp.inf))
        cb = jnp.einsum('bign,bjgn->bijg', cc, bc)
        w = cb[..., None] * lmat * dc[:, None]
        y = jnp.einsum('bijgr,bjgrp->bigrp', w, xc)
        y = y + jnp.einsum('bign,bgrpn->bigrp', cc, h) * jnp.exp(acum)[..., None]
        a_last = acum[:, -1]
        wt = jnp.exp(a_last[:, None] - acum) * dc
        h = jnp.exp(a_last)[..., None, None] * h + jnp.einsum('bjgr,bjgrp,bjgn->bgrpn', wt, xc, bc)
        return h, y

    chunks = tuple(jnp.moveaxis(t, 1, 0) for t in (xg, dg, bg, cg))
    h_fin, y = lax.scan(step, h0, chunks)
    return jnp.moveaxis(y, 0, 1).reshape(Bn, L, SSD_HEADS, SSD_HEADDIM), h_fin


def ssd_mixer(h_ctx, h_lat, w_in, conv_w, conv_b, dt_bias, a_log, d_skip, norm_g, w_out, ctx_out):
    a = -jnp.exp(a_log.astype(jnp.float32))
    gsz = SSD_DINNER // SSD_GROUPS

    def scans(h, hf0, hb0):
        z, xs, bm, cm, dt = ssd_project(h, w_in, conv_w, conv_b, dt_bias)
        y_f, hf = ssd_chunk_scan(xs, dt[:, :, 0], a[0], bm, cm, hf0)
        y_b, hb = ssd_chunk_scan(flip(xs), flip(dt[:, :, 1]), a[1], flip(bm), flip(cm), hb0)
        y = y_f + flip(y_b) + d_skip.astype(jnp.float32)[:, None] * xs.astype(jnp.float32)
        return y, z, hf, hb

    def finish(y, z):
        Bn, L = z.shape[:2]
        y = y.reshape(Bn, L, SSD_DINNER).astype(z.dtype) * jax.nn.silu(z)
        y = rmsnorm(y.reshape(Bn, L, SSD_GROUPS, gsz), norm_g.reshape(SSD_GROUPS, gsz)).reshape(Bn, L, SSD_DINNER)
        return y @ w_out

    h0 = jnp.zeros((h_ctx.shape[0], SSD_GROUPS, SSD_HEADS // SSD_GROUPS, SSD_HEADDIM, SSD_STATE), jnp.float32)
    y_c, z_c, hf, hb = scans(h_ctx, h0, h0)
    y_l, z_l, _, _ = scans(h_lat, hf, hb)
    return (finish(y_c, z_c) if ctx_out else None), finish(y_l, z_l)


def peer_ffn(h, wq, subkeys, u, v):
    Bn, L, D = h.shape
    hc = h.reshape(Bn * L // PEER_CHUNK, PEER_CHUNK, D)

    def block(xc):
        C = xc.shape[0]
        q = (xc @ wq).reshape(C, PEER_HEADS, 2, PEER_DKEY // 2)
        s = jnp.einsum('chzd,zkd->chzk', q, subkeys).astype(jnp.float32)
        top_s, top_i = lax.top_k(s, PEER_TOPK)
        cand_s = (top_s[:, :, 0, :, None] + top_s[:, :, 1, None, :]).reshape(C, PEER_HEADS, -1)
        cand_i = (top_i[:, :, 0, :, None] * PEER_KEYS + top_i[:, :, 1, None, :]).reshape(C, PEER_HEADS, -1)
        best_s, best_j = lax.top_k(cand_s, PEER_TOPK)
        idx = jnp.take_along_axis(cand_i, best_j, axis=-1)
        g = jax.nn.softmax(best_s, axis=-1)
        ue = jnp.take(u, idx, axis=0)
        ve = jnp.take(v, idx, axis=0)
        act = jax.nn.gelu(jnp.einsum('cd,chkd->chk', xc, ue).astype(jnp.float32), approximate=False)
        return jnp.einsum('chk,chkd->cd', (g * act).astype(ve.dtype), ve)

    return lax.map(block, hc).reshape(Bn, L, D)


def setup_inputs(seed: int = 0) -> dict:
    key = jax.random.key(seed)
    ks = iter(jax.random.split(key, 40))

    def nrm(shape, scale):
        return jax.random.normal(next(ks), shape, jnp.float32) * scale

    def gain(shape):
        return 1.0 + nrm(shape, 0.02)

    n_a = len(range(0, DEPTH, N_MIXERS))
    n_b = len(range(1, DEPTH, N_MIXERS))
    n_c = len(range(2, DEPTH, N_MIXERS))
    D = D_MODEL
    gla_in = 2 * GLA_HEADS * GLA_DK + 2 * GLA_HEADS * GLA_DV + 2 * GLA_GATE_RANK
    diff_in = 3 * DIFF_HEADS * 2 * DIFF_HD
    conv_ch = SSD_DINNER + 2 * SSD_GROUPS * SSD_STATE
    ssd_in = SSD_DINNER + conv_ch + 2 * SSD_HEADS
    dt0 = jnp.exp(jax.random.uniform(next(ks), (n_c, 2, SSD_HEADS), jnp.float32,
                                     minval=math.log(1e-3), maxval=math.log(1e-1)))
    a0 = jax.random.uniform(next(ks), (n_c, 2, SSD_HEADS), jnp.float32, minval=1.0, maxval=16.0)
    return {
        "x": nrm((BATCH, SEQ, D), 1.0),
        "c": nrm((BATCH, D), 1.0),
        "ctx": nrm((BATCH, CTX_LEN, D), 1.0),
        "c_ctx": nrm((D,), 1.0),
        "norm1_g": gain((DEPTH, D)),
        "norm2_g": gain((DEPTH, D)),
        "w_mod": nrm((DEPTH, D, 6 * D), 0.5 * D ** -0.5),
        "b_mod": nrm((DEPTH, 6 * D), 0.02),
        "peer_wq": nrm((DEPTH, D, PEER_HEADS * PEER_DKEY), D ** -0.5),
        "peer_subkeys": nrm((DEPTH, 2, PEER_KEYS, PEER_DKEY // 2), (PEER_DKEY // 2) ** -0.5),
        "peer_u": nrm((DEPTH, PEER_EXPERTS, D), D ** -0.5),
        "peer_v": nrm((DEPTH, PEER_EXPERTS, D), (PEER_HEADS * PEER_TOPK) ** -0.5),
        "gla_w_in": nrm((n_a, D, gla_in), D ** -0.5),
        "gla_w_alpha": nrm((n_a, 2, GLA_GATE_RANK, GLA_HEADS * GLA_DK), GLA_GATE_RANK ** -0.5),
        "gla_b_alpha": nrm((n_a, 2, GLA_HEADS * GLA_DK), 0.5),
        "gla_norm_g": gain((n_a, GLA_DV)),
        "gla_w_out": nrm((n_a, GLA_HEADS * GLA_DV, D), (GLA_HEADS * GLA_DV) ** -0.5),
        "diff_w_in": nrm((n_b, D, diff_in), D ** -0.5),
        "diff_lambda": nrm((n_b, 4, DIFF_HD), 0.1),
        "diff_norm_g": gain((n_b, 2 * DIFF_HD)),
        "diff_w_out": nrm((n_b, DIFF_HEADS * 2 * DIFF_HD, D), (DIFF_HEADS * 2 * DIFF_HD) ** -0.5),
        "ssd_w_in": nrm((n_c, D, ssd_in), D ** -0.5),
        "ssd_conv_w": nrm((n_c, SSD_CONV, conv_ch), SSD_CONV ** -0.5),
        "ssd_conv_b": nrm((n_c, conv_ch), 0.02),
        "ssd_dt_bias": dt0 + jnp.log(-jnp.expm1(-dt0)),
        "ssd_a_log": jnp.log(a0),
        "ssd_d": gain((n_c, SSD_HEADS)),
        "ssd_norm_g": gain((n_c, SSD_DINNER)),
        "ssd_w_out": nrm((n_c, SSD_DINNER, D), SSD_DINNER ** -0.5),
        "final_g": gain((D,)),
    }


def reference(x, c, ctx, c_ctx, norm1_g, norm2_g, w_mod, b_mod,
              peer_wq, peer_subkeys, peer_u, peer_v,
              gla_w_in, gla_w_alpha, gla_b_alpha, gla_norm_g, gla_w_out,
              diff_w_in, diff_lambda, diff_norm_g, diff_w_out,
              ssd_w_in, ssd_conv_w, ssd_conv_b, ssd_dt_bias, ssd_a_log, ssd_d, ssd_norm_g, ssd_w_out,
              final_g):
    n_ctx = ctx.shape[1]
    rows = x.shape[1] // GRID_W
    cos, sin = axial_rope(rows)
    cond_lat = jax.nn.silu(c)[:, None, :]
    cond_ctx = jax.nn.silu(c_ctx)[None, None, :]
    h_lat, h_ctx = x, ctx
    for i in range(DEPTH):
        kind, j = i % N_MIXERS, i // N_MIXERS
        last = i == DEPTH - 1
        mod_l = split_cols(cond_lat @ w_mod[i] + b_mod[i], (D_MODEL,) * 6)
        mod_c = split_cols(cond_ctx @ w_mod[i] + b_mod[i], (D_MODEL,) * 6)
        a_lat = modulate(rmsnorm(h_lat, norm1_g[i]), mod_l[0], mod_l[1])
        a_ctx = modulate(rmsnorm(h_ctx, norm1_g[i]), mod_c[0], mod_c[1])
        if kind == 0:
            o_ctx, o_lat = gla_mixer(a_ctx, a_lat, gla_w_in[j], gla_w_alpha[j], gla_b_alpha[j],
                                     gla_norm_g[j], gla_w_out[j], not last)
        elif kind == 1:
            o_ctx, o_lat = diff_mixer(a_ctx, a_lat, diff_w_in[j], diff_lambda[j], diff_norm_g[j],
                                      diff_w_out[j], cos, sin, lambda_init(i), not last)
        else:
            o_ctx, o_lat = ssd_mixer(a_ctx, a_lat, ssd_w_in[j], ssd_conv_w[j], ssd_conv_b[j], ssd_dt_bias[j],
                                     ssd_a_log[j], ssd_d[j], ssd_norm_g[j], ssd_w_out[j], not last)
        h_lat = h_lat + mod_l[2] * o_lat
        f_lat = modulate(rmsnorm(h_lat, norm2_g[i]), mod_l[3], mod_l[4])
        if last:
            h_lat = h_lat + mod_l[5] * peer_ffn(f_lat, peer_wq[i], peer_subkeys[i], peer_u[i], peer_v[i])
        else:
            h_ctx = h_ctx + mod_c[2] * o_ctx
            f_ctx = modulate(rmsnorm(h_ctx, norm2_g[i]), mod_c[3], mod_c[4])
            f = peer_ffn(jnp.concatenate([f_ctx, f_lat], axis=1), peer_wq[i], peer_subkeys[i], peer_u[i], peer_v[i])
            h_ctx = h_ctx + mod_c[5] * f[:, :n_ctx]
            h_lat = h_lat + mod_l[5] * f[:, n_ctx:]
    return rmsnorm(h_lat, final_g)
```

```python
import functools
import math

import jax
import jax.numpy as jnp
from jax import lax
from jax.experimental import pallas as pl
from jax.experimental.pallas import tpu as pltpu

F32 = jnp.float32
BF16 = jnp.bfloat16
I32 = jnp.int32

EPS = 1e-6
GRID_W = 64
ROPE_THETA = 10000.0
N_MIXERS = 3

GLA_HEADS, GLA_DK, GLA_DV, GLA_RANK, GLA_TAU = 4, 128, 256, 16, 16.0
GLA_CHUNK = 128
DIFF_HEADS, DIFF_HD = 8, 64
SSD_HEADS, SSD_P, SSD_N, SSD_G, SSD_CONV, SSD_CHUNK = 32, 64, 128, 4, 5, 128
SSD_R = SSD_HEADS // SSD_G
PEER_KEYS, PEER_HEADS, PEER_TOPK = 128, 8, 16

TM = 256
LANE = 128
G_PITCH = 136
VMEM_LIMIT = 56 * 1024 * 1024


def _cp(sem, vmem=VMEM_LIMIT):
    return pltpu.CompilerParams(dimension_semantics=sem, vmem_limit_bytes=vmem)


def _dot(a, b):
    return jnp.dot(a, b, preferred_element_type=F32)


def _dot_nt(a, b):
    return lax.dot_general(a, b, (((1,), (1,)), ((), ())), preferred_element_type=F32)


def _split2(x):
    hi = x.astype(BF16)
    lo = (x - hi.astype(F32)).astype(BF16)
    return hi, lo


def _dot3(a, b, dot=_dot):
    ah, al = _split2(a)
    bh, bl = _split2(b)
    return dot(ah, bh) + dot(ah, bl) + dot(al, bh)


def _dot_exact_rhs(w01, x):
    h1 = x.astype(BF16)
    r1 = x - h1.astype(F32)
    h2 = r1.astype(BF16)
    h3 = (r1 - h2.astype(F32)).astype(BF16)
    return _dot(w01, h1) + _dot(w01, h2) + _dot(w01, h3)


def _dot_01(x, w01):
    xh, xl = _split2(x)
    return _dot(xh, w01) + _dot(xl, w01)


def _silu(x):
    return x * (1.0 / (1.0 + jnp.exp(-x)))


def _softplus(x):
    return jnp.maximum(x, 0.0) + jnp.log1p(jnp.exp(-jnp.abs(x)))


def _rms(x, g):
    return x * lax.rsqrt(jnp.mean(x * x, axis=-1, keepdims=True) + EPS) * g


def _mod_row(b, j, n_ctx_tiles):
    return jnp.where(j < n_ctx_tiles, 4, b)


def _mod_kernel(cond_ref, w_ref, b_ref, o_ref):
    c = _silu(cond_ref[...])
    o_ref[0] = _dot3(c, w_ref[0]) + b_ref[0]


def mod_table(cond8, w_mod, b_mod):
    depth, d, n = w_mod.shape
    tn = 1024
    return pl.pallas_call(
        _mod_kernel,
        grid=(depth, n // tn),
        in_specs=[pl.BlockSpec((8, d), lambda i, j: (0, 0)),
                  pl.BlockSpec((1, d, tn), lambda i, j: (i, 0, j)),
                  pl.BlockSpec((1, 1, tn), lambda i, j: (i, 0, j))],
        out_specs=pl.BlockSpec((1, 8, tn), lambda i, j: (i, 0, j)),
        out_shape=jax.ShapeDtypeStruct((depth, 8, n), F32),
        compiler_params=_cp(("parallel", "parallel")),
        name="mod_table",
    )(cond8, w_mod, b_mod.reshape(depth, 1, n))


def _inproj_kernel(h_ref, g_ref, sh_ref, sc_ref, w_ref, *rest, n_ctx_tiles, rope_cols, q_cols, q_scale):
    if rope_cols:
        cos_ref, sin_ref, o_ref, a_ref = rest
    else:
        o_ref, a_ref = rest
    b, j, n = pl.program_id(0), pl.program_id(1), pl.program_id(2)

    @pl.when(n == 0)
    def _():
        row = _mod_row(b, j, n_ctx_tiles)
        sh = sh_ref[0, pl.ds(row, 1), :]
        sc = sc_ref[0, pl.ds(row, 1), :]
        a = _rms(h_ref[0], g_ref[...]) * (1.0 + sc) + sh
        a_ref[...] = a.astype(BF16)

    y = _dot(a_ref[...], w_ref[...])
    if rope_cols:
        tn = y.shape[1]
        lane = lax.broadcasted_iota(I32, y.shape, 1)
        first = (lane & (DIFF_HD - 1)) < (DIFF_HD // 2)
        part = jnp.where(first, pltpu.roll(y, tn - DIFF_HD // 2, axis=1), pltpu.roll(y, DIFF_HD // 2, axis=1))
        cs = jnp.concatenate([cos_ref[0]] * (tn // LANE), axis=1)
        sn = jnp.concatenate([sin_ref[0]] * (tn // LANE), axis=1)
        is_rope = (n * tn) < rope_cols
        cs = jnp.where(is_rope, cs, 1.0)
        sn = jnp.where(is_rope, sn, 0.0)
        y = y * cs + part * sn
        y = y * jnp.where((n * tn) < q_cols, q_scale, 1.0)
    o_ref[0] = y.astype(o_ref.dtype)


def inproj(h, g, mods, layer, k_shift, w, n_ctx, tn, out_dtype=F32, rope=None):
    bsz, lt, d = h.shape
    n = w.shape[1]
    nt = lt // TM
    kern = functools.partial(_inproj_kernel, n_ctx_tiles=n_ctx // TM, rope_cols=rope[2] if rope else 0,
                             q_cols=rope[3] if rope else 0, q_scale=rope[4] if rope else 1.0)
    in_specs = [pl.BlockSpec((1, TM, d), lambda b, j, k: (b, j, 0)),
                pl.BlockSpec((1, d), lambda b, j, k: (0, 0)),
                pl.BlockSpec((1, 8, d), lambda b, j, k: (layer, 0, k_shift)),
                pl.BlockSpec((1, 8, d), lambda b, j, k: (layer, 0, k_shift + 1)),
                pl.BlockSpec((d, tn), lambda b, j, k: (0, k))]
    args = [h, g.reshape(1, d), mods, mods, w]
    if rope:
        in_specs += [pl.BlockSpec((1, TM, LANE), lambda b, j, k: (0, j, 0))] * 2
        args += [rope[0][None], rope[1][None]]
    return pl.pallas_call(
        kern,
        grid=(bsz, nt, n // tn),
        in_specs=in_specs,
        out_specs=pl.BlockSpec((1, TM, tn), lambda b, j, k: (b, j, k)),
        out_shape=jax.ShapeDtypeStruct((bsz, lt, n), out_dtype),
        scratch_shapes=[pltpu.VMEM((TM, d), BF16)],
        compiler_params=_cp(("parallel", "parallel", "arbitrary")),
        name="inproj",
    )(*args)


def _post_gla(refs, prm):
    o_f, o_b, gate, ng = refs
    o = o_f[0] + o_b[0]
    g = gate[0]
    outs = []
    for hd in range(GLA_HEADS):
        sl = slice(hd * GLA_DV, (hd + 1) * GLA_DV)
        outs.append(_rms(o[:, sl], ng[...]) * _silu(g[:, sl]))
    return jnp.concatenate(outs, axis=1)


def _post_diff(refs, prm):
    o, ng = refs
    x = o[0]
    outs = []
    for hd in range(DIFF_HEADS):
        sl = slice(hd * 2 * DIFF_HD, (hd + 1) * 2 * DIFF_HD)
        outs.append(_rms(x[:, sl], ng[...]) * (1.0 - prm["lam_init"]))
    return jnp.concatenate(outs, axis=1)


def _post_ssd(refs, prm):
    y_f, y_b, xs, z, dexp, ng = refs
    y = (y_f[0] + y_b[0] + dexp[...] * xs[0]) * _silu(z[0])
    gs = y.shape[1] // SSD_G
    outs = []
    for gi in range(SSD_G):
        sl = slice(gi * gs, (gi + 1) * gs)
        outs.append(_rms(y[:, sl], ng[:, sl]))
    return jnp.concatenate(outs, axis=1)


_POST = {"gla": (_post_gla, 4), "diff": (_post_diff, 2), "ssd": (_post_ssd, 6)}


def _finish_kernel(*refs, kind, prm, n_ctx_tiles):
    post, n_in = _POST[kind]
    mix = refs[:n_in]
    w_ref, h_ref, gm_ref, g2_ref, sh_ref, sc_ref, hn_ref, f_ref = refs[n_in:]
    b, j = pl.program_id(0), pl.program_id(1)
    row = _mod_row(b, j, n_ctx_tiles)
    y = post(mix, prm)
    o = _dot(y.astype(BF16), w_ref[...])
    hn = h_ref[0] + gm_ref[0, pl.ds(row, 1), :] * o
    hn_ref[0] = hn
    f_ref[0] = _rms(hn, g2_ref[...]) * (1.0 + sc_ref[0, pl.ds(row, 1), :]) + sh_ref[0, pl.ds(row, 1), :]


def finish(kind, mix_args, mix_specs, w_out, h, g2, mods, layer, n_ctx, prm=None):
    bsz, lt, d = h.shape
    nt = lt // TM
    dm = w_out.shape[0]
    tile = lambda: pl.BlockSpec((1, TM, d), lambda b, j: (b, j, 0))
    modspec = lambda k: pl.BlockSpec((1, 8, d), lambda b, j: (layer, 0, k))
    kern = functools.partial(_finish_kernel, kind=kind, prm=prm or {}, n_ctx_tiles=n_ctx // TM)
    return pl.pallas_call(
        kern,
        grid=(bsz, nt),
        in_specs=list(mix_specs) + [pl.BlockSpec((dm, d), lambda b, j: (0, 0)), tile(), modspec(2),
                                    pl.BlockSpec((1, d), lambda b, j: (0, 0)), modspec(3), modspec(4)],
        out_specs=[tile(), tile()],
        out_shape=[jax.ShapeDtypeStruct((bsz, lt, d), F32)] * 2,
        compiler_params=_cp(("parallel", "parallel")),
        name="finish_" + kind,
    )(*mix_args, w_out, h, mods, g2.reshape(1, d), mods, mods)


def _gla_kernel(q_ref, k_ref, v_ref, lo_ref, wa_ref, ba_ref, o_ref, st_ref, *, rev):
    c = GLA_CHUNK

    @pl.when(pl.program_id(2) == 0)
    def _():
        st_ref[...] = jnp.zeros_like(st_ref)

    q = q_ref[0] * (GLA_DK ** -0.5)
    k = k_ref[0]
    v = v_ref[0]
    z = _dot3(lo_ref[0], wa_ref[0]) + ba_ref[0]
    la = (jnp.minimum(z, 0.0) - jnp.log1p(jnp.exp(-jnp.abs(z)))) * (1.0 / GLA_TAU)

    ri = lax.broadcasted_iota(I32, (c, c), 0)
    ci = lax.broadcasted_iota(I32, (c, c), 1)
    row = lax.broadcasted_iota(I32, (c, GLA_DK), 0)
    tri = jnp.where((ci >= ri) if rev else (ci <= ri), 1.0, 0.0).astype(BF16)
    bsum = _dot_exact_rhs(tri, la)
    tot = bsum[0:1, :] if rev else bsum[c - 1:c, :]

    a = jnp.where(ci == ri, _dot_nt(q.astype(BF16), k.astype(BF16)), 0.0)
    s = c // 2
    while s >= 1:
        pos = row & (2 * s - 1)
        q_half = (pos < s) if rev else (pos >= s)
        if s >= 4:
            blk = bsum.reshape(c // (2 * s), 2 * s, GLA_DK)
            rr = s if rev else s - 1
            ref = jnp.broadcast_to(blk[:, rr:rr + 1, :], blk.shape).reshape(c, GLA_DK)
            u = bsum - ref
            w = ref - bsum
        elif s == 2:
            pos4 = row & 3
            if rev:
                u = jnp.where(pos4 == 1, la, la + pltpu.roll(la, c - 1, axis=0))
                w = jnp.where(pos4 == 2, 0.0, pltpu.roll(la, 1, axis=0))
            else:
                u = jnp.where(pos4 == 2, la, la + pltpu.roll(la, 1, axis=0))
                w = jnp.where(pos4 == 1, 0.0, pltpu.roll(la, c - 1, axis=0))
        else:
            u = la
            w = jnp.zeros_like(la)
        qd = jnp.where(q_half, q * jnp.exp(jnp.minimum(u, 0.0)), 0.0).astype(BF16)
        kd = jnp.where(q_half, 0.0, k * jnp.exp(jnp.minimum(w, 0.0))).astype(BF16)
        same_block = (ri & -(2 * s)) == (ci & -(2 * s))
        a = a + jnp.where(same_block, _dot_nt(qd, kd), 0.0)
        s //= 2

    st = st_ref[...]
    o = _dot(a.astype(BF16), v.astype(BF16))
    o = o + _dot_nt((q * jnp.exp(bsum)).astype(BF16), st.astype(BF16))
    o_ref[0] = o
    kd = (k * jnp.exp(tot - bsum)).astype(BF16)
    st_ref[...] = st * jnp.exp(tot) + _dot(v.T.astype(BF16), kd)


def _scan_chunk(c, n_ctx_chunks, n_chunks, rev):
    if not rev:
        return c
    return jnp.where(c < n_ctx_chunks, n_ctx_chunks - 1 - c, n_chunks - 1 - (c - n_ctx_chunks))


def gla_scan(qkvg, lo, wa_pad, ba, n_ctx, rev):
    bsz, lt, _ = qkvg.shape
    c = GLA_CHUNK
    nch, ncc = lt // c, n_ctx // c
    d = 1 if rev else 0
    cm = functools.partial(_scan_chunk, n_ctx_chunks=ncc, n_chunks=nch, rev=rev)
    return pl.pallas_call(
        functools.partial(_gla_kernel, rev=rev),
        grid=(bsz, GLA_HEADS, nch),
        in_specs=[pl.BlockSpec((1, c, GLA_DK), lambda b, h, i: (b, cm(i), h)),
                  pl.BlockSpec((1, c, GLA_DK), lambda b, h, i: (b, cm(i), GLA_HEADS + h)),
                  pl.BlockSpec((1, c, GLA_DV), lambda b, h, i: (b, cm(i), GLA_HEADS + h)),
                  pl.BlockSpec((1, c, LANE), lambda b, h, i: (b, cm(i), 0)),
                  pl.BlockSpec((1, LANE, GLA_DK), lambda b, h, i: (d, 0, h)),
                  pl.BlockSpec((1, 1, GLA_DK), lambda b, h, i: (d, 0, h))],
        out_specs=pl.BlockSpec((1, c, GLA_DV), lambda b, h, i: (b, cm(i), h)),
        out_shape=jax.ShapeDtypeStruct((bsz, lt, GLA_HEADS * GLA_DV), F32),
        scratch_shapes=[pltpu.VMEM((GLA_DV, GLA_DK), F32)],
        compiler_params=_cp(("parallel", "parallel", "arbitrary")),
        name="gla_scan_bwd" if rev else "gla_scan_fwd",
    )(qkvg, qkvg, qkvg, lo, wa_pad, ba)


def _diff_kernel(lam_ref, q_ref, k_ref, v_ref, o_ref, *, n_ctx, lam_init):
    j = pl.program_id(2)
    lv = lam_ref[...]
    l1 = jnp.sum(lv[0:1] * lv[1:2], axis=1, keepdims=True)
    l2 = jnp.sum(lv[2:3] * lv[3:4], axis=1, keepdims=True)
    lam = jnp.exp(l1) - jnp.exp(l2) + lam_init

    q = q_ref[0]
    k = k_ref[0]
    lane = lax.broadcasted_iota(I32, q.shape, 1)
    zero = jnp.zeros_like(q)
    s0 = _dot_nt(jnp.where(lane < DIFF_HD, q, zero), k)
    s1 = _dot_nt(jnp.where(lane < DIFF_HD, zero, q), k)
    key = lax.broadcasted_iota(I32, s0.shape, 1)
    visible = jnp.logical_or(key < n_ctx, j * TM >= n_ctx)
    ninf = jnp.float32(-jnp.inf)
    s0 = jnp.where(visible, s0, ninf)
    s1 = jnp.where(visible, s1, ninf)
    p0 = jnp.exp(s0 - jnp.max(s0, axis=1, keepdims=True))
    p1 = jnp.exp(s1 - jnp.max(s1, axis=1, keepdims=True))
    r0 = 1.0 / jnp.sum(p0, axis=1, keepdims=True)
    r1 = lam / jnp.sum(p1, axis=1, keepdims=True)
    a = p0 * r0 - p1 * r1
    o_ref[0] = _dot(a.astype(BF16), v_ref[0])


def diff_attention(qkv, lam_vecs, n_ctx, lam_init):
    bsz, lt, _ = qkv.shape
    nh = DIFF_HEADS
    return pl.pallas_call(
        functools.partial(_diff_kernel, n_ctx=n_ctx, lam_init=lam_init),
        grid=(bsz, nh, lt // TM),
        in_specs=[pl.BlockSpec((4, DIFF_HD), lambda b, h, j: (0, 0)),
                  pl.BlockSpec((1, TM, LANE), lambda b, h, j: (b, j, h)),
                  pl.BlockSpec((1, lt, LANE), lambda b, h, j: (b, 0, nh + h)),
                  pl.BlockSpec((1, lt, LANE), lambda b, h, j: (b, 0, 2 * nh + h))],
        out_specs=pl.BlockSpec((1, TM, LANE), lambda b, h, j: (b, j, h)),
        out_shape=jax.ShapeDtypeStruct((bsz, lt, nh * LANE), F32),
        compiler_params=_cp(("parallel", "parallel", "arbitrary")),
        name="diff_attention",
    )(lam_vecs, qkv, qkv, qkv)


def _conv_kernel(x_ref, p_ref, n_ref, w_ref, b_ref, o_ref, *, n_ctx_tiles, n_tiles):
    j = pl.program_id(1)
    first = jnp.logical_or(j == 0, j == n_ctx_tiles)
    last = jnp.logical_or(j == n_ctx_tiles - 1, j == n_tiles - 1)
    x = x_ref[0]
    prev = jnp.where(first, 0.0, p_ref[0])
    nxt = jnp.where(last, 0.0, n_ref[0])
    xe = jnp.concatenate([prev, x, nxt], axis=0)
    ne = xe.shape[0]
    half = SSD_CONV // 2
    y = b_ref[...] + w_ref[half:half + 1, :] * x
    for t in range(SSD_CONV):
        if t == half:
            continue
        sh = pltpu.roll(xe, (half - t) % ne, axis=0)[8:8 + TM]
        y = y + w_ref[t:t + 1, :] * sh
    o_ref[0] = _silu(y)


def ssd_conv(xbc, conv_w, conv_b, n_ctx):
    bsz, lt, ch = xbc.shape
    nt = lt // TM
    r8 = TM // 8
    return pl.pallas_call(
        functools.partial(_conv_kernel, n_ctx_tiles=n_ctx // TM, n_tiles=nt),
        grid=(bsz, nt),
        in_specs=[pl.BlockSpec((1, TM, ch), lambda b, j: (b, j, 0)),
                  pl.BlockSpec((1, 8, ch), lambda b, j: (b, jnp.maximum(j * r8 - 1, 0), 0)),
                  pl.BlockSpec((1, 8, ch), lambda b, j: (b, jnp.minimum((j + 1) * r8, lt // 8 - 1), 0)),
                  pl.BlockSpec((SSD_CONV, ch), lambda b, j: (0, 0)),
                  pl.BlockSpec((1, ch), lambda b, j: (0, 0))],
        out_specs=pl.BlockSpec((1, TM, ch), lambda b, j: (b, j, 0)),
        out_shape=jax.ShapeDtypeStruct((bsz, lt, ch), F32),
        compiler_params=_cp(("parallel", "parallel")),
        name="ssd_conv",
    )(xbc, xbc, xbc, conv_w, conv_b.reshape(1, ch))


def _ssd_kernel(xs_ref, bm_ref, cm_ref, dt_ref, dtb_ref, al_ref, ex_ref, y_ref, st_ref, *, rev):
    qn = SSD_CHUNK

    @pl.when(pl.program_id(2) == 0)
    def _():
        st_ref[...] = jnp.zeros_like(st_ref)

    xs = xs_ref[0]
    bm = bm_ref[0]
    cmat = cm_ref[0]
    dt = _softplus(dt_ref[0] + dtb_ref[...])
    da = dt * (-jnp.exp(al_ref[...]))
    ri = lax.broadcasted_iota(I32, (qn, qn), 0)
    ci = lax.broadcasted_iota(I32, (qn, qn), 1)
    causal = (ci >= ri) if rev else (ci <= ri)
    tri = jnp.where(causal, 1.0, 0.0).astype(BF16)
    acum = _dot_exact_rhs(tri, da)
    acum_t = acum.T
    dt_t = dt.T
    tot = acum[0:1, :] if rev else acum[qn - 1:qn, :]

    cb = _dot_nt(cmat.astype(BF16), bm.astype(BF16))
    xs16 = xs.astype(BF16)
    lane = lax.broadcasted_iota(I32, (qn, 2 * SSD_P), 1)
    pieces = []
    for rp in range(SSD_R // 2):
        acc = None
        for sub in range(2):
            r = 2 * rp + sub
            seg = acum[:, r:r + 1] - acum_t[r:r + 1, :]
            w = cb * jnp.exp(jnp.minimum(seg, 0.0)) * dt_t[r:r + 1, :]
            w = jnp.where(causal, w, 0.0).astype(BF16)
            xpair = xs16[:, rp * 2 * SSD_P:(rp + 1) * 2 * SSD_P]
            keep = (lane < SSD_P) if sub == 0 else (lane >= SSD_P)
            part = _dot(w, jnp.where(keep, xpair, jnp.zeros_like(xpair)))
            acc = part if acc is None else acc + part
        pieces.append(acc)
    y = jnp.concatenate(pieces, axis=1)

    ex = ex_ref[...]
    st = st_ref[...]
    e_i = _dot_01(jnp.exp(acum), ex)
    y = y + _dot(cmat.astype(BF16), st.astype(BF16)) * e_i
    y_ref[0] = y
    wt = _dot_01(jnp.exp(tot - acum) * dt, ex)
    e_tot = _dot_01(jnp.broadcast_to(jnp.exp(tot), (8, LANE)), ex)[0:1, :]
    st_ref[...] = st * e_tot + _dot(bm.T.astype(BF16), (xs * wt).astype(BF16))


def ssd_scan(xc, dt, dtb, alog, expand, n_ctx, rev):
    bsz, lt, _ = xc.shape
    qn = SSD_CHUNK
    nch, ncc = lt // qn, n_ctx // qn
    d = 1 if rev else 0
    gw = SSD_R * SSD_P
    cm = functools.partial(_scan_chunk, n_ctx_chunks=ncc, n_chunks=nch, rev=rev)
    nb0 = SSD_HEADS * SSD_P // LANE
    return pl.pallas_call(
        functools.partial(_ssd_kernel, rev=rev),
        grid=(bsz, SSD_G, nch),
        in_specs=[pl.BlockSpec((1, qn, gw), lambda b, g, i: (b, cm(i), g)),
                  pl.BlockSpec((1, qn, SSD_N), lambda b, g, i: (b, cm(i), nb0 + g)),
                  pl.BlockSpec((1, qn, SSD_N), lambda b, g, i: (b, cm(i), nb0 + SSD_G + g)),
                  pl.BlockSpec((1, qn, LANE), lambda b, g, i: (b, cm(i), d * SSD_G + g)),
                  pl.BlockSpec((1, LANE), lambda b, g, i: (0, d * SSD_G + g)),
                  pl.BlockSpec((1, LANE), lambda b, g, i: (0, d * SSD_G + g)),
                  pl.BlockSpec((LANE, gw), lambda b, g, i: (0, 0))],
        out_specs=pl.BlockSpec((1, qn, gw), lambda b, g, i: (b, cm(i), g)),
        out_shape=jax.ShapeDtypeStruct((bsz, lt, SSD_HEADS * SSD_P), F32),
        scratch_shapes=[pltpu.VMEM((SSD_N, gw), F32)],
        compiler_params=_cp(("parallel", "parallel", "arbitrary")),
        name="ssd_scan_bwd" if rev else "ssd_scan_fwd",
    )(xc, xc, xc, dt, dtb, alog, expand)


def _top16(s_ref, ids, vals_ref, idx_ref, big):
    def body(r, carry):
        s = s_ref[...]
        m = jnp.max(s, axis=0, keepdims=True)
        sel = jnp.min(jnp.where(s == m, ids, big), axis=0, keepdims=True)
        s_ref[...] = jnp.where(ids == sel, -jnp.inf, s)
        vals_ref[pl.ds(r, 1), :] = m
        idx_ref[pl.ds(r, 1), :] = sel
        return carry
    lax.fori_loop(0, PEER_TOPK, body, 0)


def _peer_route_kernel(f_ref, wh_ref, wl_ref, kh_ref, kl_ref, a_ref, b_ref, g_ref,
                       q_ref, s_ref, c_ref, v1_ref, i1_ref, v2_ref, i2_ref, vb_ref, ib_ref):
    h = pl.program_id(1)
    nk = PEER_KEYS

    @pl.when(h == 0)
    def _():
        fh, fl = _split2(f_ref[...])
        q_ref[...] = _dot(fh, wh_ref[...]) + _dot(fh, wl_ref[...]) + _dot(fl, wh_ref[...])

    kid = lax.broadcasted_iota(I32, (nk, TM), 0)
    for z, (vr, ir) in enumerate(((v1_ref, i1_ref), (v2_ref, i2_ref))):
        col = pl.multiple_of((h * 2 + z) * nk, nk)
        qh, ql = _split2(q_ref[:, pl.ds(col, nk)])
        kh, kl = kh_ref[z], kl_ref[z]
        s_ref[...] = _dot_nt(kh, qh) + _dot_nt(kh, ql) + _dot_nt(kl, qh)
        _top16(s_ref, kid, vr, ir, nk)

    v1, v2 = v1_ref[...], v2_ref[...]
    parts = [v1[0:1] + v2]
    parts += [v1[p:p + 1] + v2[0:8] for p in range(1, 8)]
    parts += [v1[8:16] + v2[0:1]]
    c_ref[...] = jnp.concatenate(parts, axis=0)
    nc = c_ref.shape[0]
    r = lax.broadcasted_iota(I32, (nc, TM), 0)
    flat = jnp.where(r < 16, r, jnp.where(r < 72, (1 + ((r - 16) >> 3)) * 16 + ((r - 16) & 7), (r - 64) * 16))
    _top16(c_ref, flat, vb_ref, ib_ref, 1 << 20)

    sel = ib_ref[...]
    p, qq = sel >> 4, sel & (PEER_TOPK - 1)
    i1, i2 = i1_ref[...], i2_ref[...]
    a = jnp.zeros_like(sel)
    b = jnp.zeros_like(sel)
    for t in range(PEER_TOPK):
        a = jnp.where(p == t, i1[t:t + 1], a)
        b = jnp.where(qq == t, i2[t:t + 1], b)
    vb = vb_ref[...]
    e = jnp.exp(vb - vb[0:1])
    a_ref[0] = a
    b_ref[0] = b
    g_ref[0] = e / jnp.sum(e, axis=0, keepdims=True)


def peer_route(f2, wq_hi, wq_lo, sk_hi, sk_lo):
    t, d = f2.shape
    nq = wq_hi.shape[1]
    k = PEER_TOPK
    outspec = pl.BlockSpec((1, k, TM), lambda i, h: (h, 0, i))
    return pl.pallas_call(
        _peer_route_kernel,
        grid=(t // TM, PEER_HEADS),
        in_specs=[pl.BlockSpec((TM, d), lambda i, h: (i, 0)),
                  pl.BlockSpec((d, nq), lambda i, h: (0, 0)),
                  pl.BlockSpec((d, nq), lambda i, h: (0, 0)),
                  pl.BlockSpec(sk_hi.shape, lambda i, h: (0, 0, 0)),
                  pl.BlockSpec(sk_lo.shape, lambda i, h: (0, 0, 0))],
        out_specs=[outspec, outspec, outspec],
        out_shape=[jax.ShapeDtypeStruct((PEER_HEADS, k, t), I32)] * 2 + [jax.ShapeDtypeStruct((PEER_HEADS, k, t), F32)],
        scratch_shapes=[pltpu.VMEM((TM, nq), F32), pltpu.VMEM((PEER_KEYS, TM), F32), pltpu.VMEM((80, TM), F32),
                        pltpu.VMEM((k, TM), F32), pltpu.VMEM((k, TM), I32),
                        pltpu.VMEM((k, TM), F32), pltpu.VMEM((k, TM), I32),
                        pltpu.VMEM((k, TM), F32), pltpu.VMEM((k, TM), I32)],
        compiler_params=_cp(("parallel", "arbitrary")),
        name="peer_route",
    )(f2, wq_hi, wq_lo, sk_hi, sk_lo)


def _peer_dense_kernel(f_ref, h_ref, gm_ref, a_ref, b_ref, g_ref, u_ref, v_ref, o_ref,
                       f16_ref, ar_ref, br_ref, gr_ref, gs_ref, acc_ref, *, tiles_per_batch, n_ctx_tiles, te):
    i, e = pl.program_id(0), pl.program_id(1)
    nk = PEER_KEYS

    @pl.when(e == 0)
    def _():
        f16_ref[...] = f_ref[...].astype(BF16)
        acc_ref[...] = jnp.zeros_like(acc_ref)
        ar_ref[...] = a_ref[...].T
        br_ref[...] = b_ref[...].T
        gr_ref[...] = g_ref[...].T
        sub = lax.broadcasted_iota(I32, (nk, nk), 0)

        def per_token(t, carry):
            arow = ar_ref[pl.ds(t, 1), :]
            brow = br_ref[pl.ds(t, 1), :]
            grow = gr_ref[pl.ds(t, 1), :]
            ga = jnp.where(sub == arow, grow, 0.0)
            ob = jnp.where(sub == brow, 1.0, 0.0).astype(BF16)
            gh, gl = _split2(ga)
            off = pl.multiple_of(t * G_PITCH, 8)
            gs_ref[pl.ds(off, nk), :] = _dot_nt(gh, ob) + _dot_nt(gl, ob)
            return carry
        lax.fori_loop(0, TM, per_token, 0)

    s = _dot_nt(f16_ref[...], u_ref[...])
    gts = [gs_ref[pl.ds(e * (te // nk) + c, TM, stride=G_PITCH), :] for c in range(te // nk)]
    gt = jnp.concatenate(gts, axis=1) if len(gts) > 1 else gts[0]
    act = 0.5 * s * (1.0 + lax.erf(s * (2.0 ** -0.5)))
    acc_ref[...] += _dot((act * gt).astype(BF16), v_ref[...])

    @pl.when(e == pl.num_programs(1) - 1)
    def _():
        row = _mod_row(i // tiles_per_batch, i % tiles_per_batch, n_ctx_tiles)
        o_ref[...] = h_ref[...] + gm_ref[0, pl.ds(row, 1), :] * acc_ref[...]


def peer_dense(f2, h2, mods, layer, a_t, b_t, g_t, u16, v16, tiles_per_batch, n_ctx):
    t, d = f2.shape
    ne = u16.shape[0]
    te = 256
    hk = PEER_HEADS * PEER_TOPK
    kern = functools.partial(_peer_dense_kernel, tiles_per_batch=tiles_per_batch, n_ctx_tiles=n_ctx // TM, te=te)
    tok = lambda: pl.BlockSpec((TM, d), lambda i, e: (i, 0))
    rt = lambda: pl.BlockSpec((hk, TM), lambda i, e: (0, i))
    return pl.pallas_call(
        kern,
        grid=(t // TM, ne // te),
        in_specs=[tok(), tok(), pl.BlockSpec((1, 8, d), lambda i, e: (layer, 0, 5)), rt(), rt(), rt(),
                  pl.BlockSpec((te, d), lambda i, e: (e, 0)),
                  pl.BlockSpec((te, d), lambda i, e: (e, 0))],
        out_specs=tok(),
        out_shape=jax.ShapeDtypeStruct((t, d), F32),
        scratch_shapes=[pltpu.VMEM((TM, d), BF16), pltpu.VMEM((TM, hk), I32), pltpu.VMEM((TM, hk), I32),
                        pltpu.VMEM((TM, hk), F32), pltpu.VMEM((TM * G_PITCH, PEER_KEYS), F32),
                        pltpu.VMEM((TM, d), F32)],
        compiler_params=_cp(("parallel", "arbitrary")),
        name="peer_dense",
    )(f2, h2, mods, a_t, b_t, g_t, u16, v16)


def peer_ffn(f, h, mods, layer, wq, subkeys, u, v, n_ctx):
    bsz, lt, d = h.shape
    t = bsz * lt
    f2, h2 = f.reshape(t, d), h.reshape(t, d)
    wq_hi, wq_lo = _split2(wq)
    sk_hi, sk_lo = _split2(subkeys)
    a, b, g = peer_route(f2, wq_hi, wq_lo, sk_hi, sk_lo)
    hk = PEER_HEADS * PEER_TOPK
    out = peer_dense(f2, h2, mods, layer, a.reshape(hk, t), b.reshape(hk, t), g.reshape(hk, t),
                     u.astype(BF16), v.astype(BF16), lt // TM, n_ctx)
    return out.reshape(bsz, lt, d)


def _final_kernel(h_ref, g_ref, o_ref):
    o_ref[0] = _rms(h_ref[0], g_ref[...])


def final_norm(h, g, n_ctx):
    bsz, lt, d = h.shape
    l = lt - n_ctx
    off = n_ctx // TM
    return pl.pallas_call(
        _final_kernel,
        grid=(bsz, l // TM),
        in_specs=[pl.BlockSpec((1, TM, d), lambda b, j: (b, j + off, 0)),
                  pl.BlockSpec((1, d), lambda b, j: (0, 0))],
        out_specs=pl.BlockSpec((1, TM, d), lambda b, j: (b, j, 0)),
        out_shape=jax.ShapeDtypeStruct((bsz, l, d), F32),
        compiler_params=_cp(("parallel", "parallel")),
        name="final_norm",
    )(h, g.reshape(1, d))


def _rope_tables(l, n_ctx):
    rows = l // GRID_W
    row = jnp.repeat(jnp.arange(rows), GRID_W).astype(F32)
    col = jnp.tile(jnp.arange(GRID_W), rows).astype(F32)
    n_freq = DIFF_HD // 4
    freqs = ROPE_THETA ** (-jnp.arange(n_freq, dtype=F32) / n_freq)
    ang = jnp.concatenate([row[:, None] * freqs, col[:, None] * freqs], axis=-1)
    cos, sin = jnp.cos(ang), jnp.sin(ang)
    cos = jnp.concatenate([jnp.ones((n_ctx, DIFF_HD // 2), F32), cos], axis=0)
    sin = jnp.concatenate([jnp.zeros((n_ctx, DIFF_HD // 2), F32), sin], axis=0)
    cos_t = jnp.concatenate([cos, cos, cos, cos], axis=1)
    sin_t = jnp.concatenate([-sin, sin, -sin, sin], axis=1)
    return cos_t, sin_t


def _gla_layer(h, mods, i, j, p, n_ctx):
    hk = GLA_HEADS * GLA_DK
    hv = GLA_HEADS * GLA_DV
    w_in = p["gla_w_in"][j]
    n_main = 2 * hk + 2 * hv
    qkvg = inproj(h, p["norm1_g"][i], mods, i, 0, w_in[:, :n_main].astype(BF16), n_ctx, 1024)
    w_lo = jnp.pad(w_in[:, n_main:], ((0, 0), (0, LANE - 2 * GLA_RANK))).astype(BF16)
    lo = inproj(h, p["norm1_g"][i], mods, i, 0, w_lo, n_ctx, LANE)
    wa = p["gla_w_alpha"][j]
    wa_pad = jnp.stack([jnp.pad(wa[0], ((0, LANE - GLA_RANK), (0, 0))),
                        jnp.pad(wa[1], ((GLA_RANK, LANE - 2 * GLA_RANK), (0, 0)))])
    ba = p["gla_b_alpha"][j].reshape(2, 1, hk)
    o_f = gla_scan(qkvg, lo, wa_pad, ba, n_ctx, False)
    o_b = gla_scan(qkvg, lo, wa_pad, ba, n_ctx, True)
    tile = lambda: pl.BlockSpec((1, TM, hv), lambda b, t: (b, t, 0))
    specs = [tile(), tile(), pl.BlockSpec((1, TM, hv), lambda b, t: (b, t, 2)),
             pl.BlockSpec((1, GLA_DV), lambda b, t: (0, 0))]
    args = [o_f, o_b, qkvg, p["gla_norm_g"][j].reshape(1, GLA_DV)]
    return finish("gla", args, specs, p["gla_w_out"][j].astype(BF16), h, p["norm2_g"][i], mods, i, n_ctx)


def _diff_layer(h, mods, i, j, p, n_ctx, rope_tabs):
    wd = DIFF_HEADS * 2 * DIFF_HD
    cos_t, sin_t = rope_tabs
    lt = h.shape[1]
    qkv = inproj(h, p["norm1_g"][i], mods, i, 0, p["diff_w_in"][j].astype(BF16), n_ctx, 1024,
                 out_dtype=BF16, rope=(cos_t, sin_t, 2 * wd, wd, DIFF_HD ** -0.5))
    lam_init = 0.8 - 0.6 * math.exp(-0.3 * i)
    o = diff_attention(qkv, p["diff_lambda"][j], n_ctx, lam_init)
    specs = [pl.BlockSpec((1, TM, wd), lambda b, t: (b, t, 0)),
             pl.BlockSpec((1, 2 * DIFF_HD), lambda b, t: (0, 0))]
    args = [o, p["diff_norm_g"][j].reshape(1, 2 * DIFF_HD)]
    return finish("diff", args, specs, p["diff_w_out"][j].astype(BF16), h, p["norm2_g"][i], mods, i, n_ctx,
                  prm={"lam_init": lam_init})


def _ssd_layer(h, mods, i, j, p, n_ctx):
    din = SSD_HEADS * SSD_P
    gn = SSD_G * SSD_N
    w_in = p["ssd_w_in"][j]
    g1 = p["norm1_g"][i]
    z = inproj(h, g1, mods, i, 0, w_in[:, :din].astype(BF16), n_ctx, 1024)
    xbc = inproj(h, g1, mods, i, 0, w_in[:, din:2 * din + 2 * gn].astype(BF16), n_ctx, 1024)
    w_dt = w_in[:, 2 * din + 2 * gn:].reshape(-1, 2 * SSD_G, SSD_R)
    w_dt = jnp.pad(w_dt, ((0, 0), (0, 0), (0, LANE - SSD_R))).reshape(-1, 2 * SSD_G * LANE)
    dt = inproj(h, g1, mods, i, 0, w_dt.astype(BF16), n_ctx, 1024)
    pad_r = lambda a: jnp.pad(a.reshape(2 * SSD_G, SSD_R), ((0, 0), (0, LANE - SSD_R))).reshape(1, -1)
    dtb, alog = pad_r(p["ssd_dt_bias"][j]), pad_r(p["ssd_a_log"][j])
    xc = ssd_conv(xbc, p["ssd_conv_w"][j], p["ssd_conv_b"][j], n_ctx)
    expand = (jnp.arange(LANE)[:, None] == (jnp.arange(SSD_R * SSD_P)[None, :] // SSD_P)).astype(BF16)
    y_f = ssd_scan(xc, dt, dtb, alog, expand, n_ctx, False)
    y_b = ssd_scan(xc, dt, dtb, alog, expand, n_ctx, True)
    tile = lambda: pl.BlockSpec((1, TM, din), lambda b, t: (b, t, 0))
    row = lambda: pl.BlockSpec((1, din), lambda b, t: (0, 0))
    specs = [tile(), tile(), tile(), tile(), row(), row()]
    dexp = jnp.repeat(p["ssd_d"][j], SSD_P).reshape(1, din)
    args = [y_f, y_b, xc, z, dexp, p["ssd_norm_g"][j].reshape(1, din)]
    return finish("ssd", args, specs, p["ssd_w_out"][j].astype(BF16), h, p["norm2_g"][i], mods, i, n_ctx)


def kernel(x, c, ctx, c_ctx, norm1_g, norm2_g, w_mod, b_mod, peer_wq, peer_subkeys, peer_u, peer_v, gla_w_in, gla_w_alpha, gla_b_alpha, gla_norm_g, gla_w_out, diff_w_in, diff_lambda, diff_norm_g, diff_w_out, ssd_w_in, ssd_conv_w, ssd_conv_b, ssd_dt_bias, ssd_a_log, ssd_d, ssd_norm_g, ssd_w_out, final_g):
    p = dict(norm1_g=norm1_g, norm2_g=norm2_g, gla_w_in=gla_w_in, gla_w_alpha=gla_w_alpha, gla_b_alpha=gla_b_alpha,
             gla_norm_g=gla_norm_g, gla_w_out=gla_w_out, diff_w_in=diff_w_in, diff_lambda=diff_lambda,
             diff_norm_g=diff_norm_g, diff_w_out=diff_w_out, ssd_w_in=ssd_w_in, ssd_conv_w=ssd_conv_w,
             ssd_conv_b=ssd_conv_b, ssd_dt_bias=ssd_dt_bias, ssd_a_log=ssd_a_log, ssd_d=ssd_d,
             ssd_norm_g=ssd_norm_g, ssd_w_out=ssd_w_out)
    bsz, l, d = x.shape
    n_ctx = ctx.shape[1]
    depth = w_mod.shape[0]
    assert bsz <= 4 and n_ctx % TM == 0 and l % TM == 0 and l % GRID_W == 0
    cond8 = jnp.concatenate([c, jnp.zeros((4 - bsz, d), F32), c_ctx[None], jnp.zeros((3, d), F32)], axis=0)
    mods = mod_table(cond8, w_mod, b_mod)
    rope_tabs = _rope_tables(l, n_ctx)
    h = jnp.concatenate([ctx, x], axis=1)
    for i in range(depth):
        kind, j = i % N_MIXERS, i // N_MIXERS
        if kind == 0:
            h, f = _gla_layer(h, mods, i, j, p, n_ctx)
        elif kind == 1:
            h, f = _diff_layer(h, mods, i, j, p, n_ctx, rope_tabs)
        else:
            h, f = _ssd_layer(h, mods, i, j, p, n_ctx)
        h = peer_ffn(f, h, mods, i, peer_wq[i], peer_subkeys[i], peer_u[i], peer_v[i], n_ctx)
    return final_norm(h, final_g, n_ctx)
```

```python
import functools
import math

import jax
import jax.numpy as jnp
from jax import lax
from jax.experimental import pallas as pl
from jax.experimental.pallas import tpu as pltpu

F32 = jnp.float32
BF16 = jnp.bfloat16
I32 = jnp.int32

EPS = 1e-6
GRID_W = 64
ROPE_THETA = 10000.0
N_MIXERS = 3

GLA_HEADS, GLA_DK, GLA_DV, GLA_RANK, GLA_TAU = 4, 128, 256, 16, 16.0
GLA_CHUNK = 128
DIFF_HEADS, DIFF_HD = 8, 64
SSD_HEADS, SSD_P, SSD_N, SSD_G, SSD_CONV, SSD_CHUNK = 32, 64, 128, 4, 5, 128
SSD_R = SSD_HEADS // SSD_G
PEER_KEYS, PEER_HEADS, PEER_TOPK = 128, 8, 16

TM = 256
LANE = 128
PEER_TM = 512
PEER_TE = 512
G_PITCH = 72
VMEM_LIMIT = 56 * 1024 * 1024


def _cp(sem, vmem=VMEM_LIMIT):
    return pltpu.CompilerParams(dimension_semantics=sem, vmem_limit_bytes=vmem)


def _dot(a, b):
    return jnp.dot(a, b, preferred_element_type=F32)


def _dot_nt(a, b):
    return lax.dot_general(a, b, (((1,), (1,)), ((), ())), preferred_element_type=F32)


def _split2(x):
    hi = x.astype(BF16)
    lo = (x - hi.astype(F32)).astype(BF16)
    return hi, lo


def _dot3(a, b, dot=_dot):
    ah, al = _split2(a)
    bh, bl = _split2(b)
    return dot(ah, bh) + dot(ah, bl) + dot(al, bh)


def _dot_exact_rhs(w01, x):
    h1 = x.astype(BF16)
    r1 = x - h1.astype(F32)
    h2 = r1.astype(BF16)
    h3 = (r1 - h2.astype(F32)).astype(BF16)
    return _dot(w01, h1) + _dot(w01, h2) + _dot(w01, h3)


def _dot_01(x, w01):
    xh, xl = _split2(x)
    return _dot(xh, w01) + _dot(xl, w01)


def _silu(x):
    return x * (1.0 / (1.0 + jnp.exp(-x)))


def _softplus(x):
    return jnp.maximum(x, 0.0) + jnp.log1p(jnp.exp(-jnp.abs(x)))


def _rms(x, g):
    return x * lax.rsqrt(jnp.mean(x * x, axis=-1, keepdims=True) + EPS) * g


def _mod_row(b, j, n_ctx_tiles):
    return jnp.where(j < n_ctx_tiles, 4, b)


def _mod_kernel(cond_ref, w_ref, b_ref, o_ref):
    c = _silu(cond_ref[...])
    o_ref[0] = _dot3(c, w_ref[0]) + b_ref[0]


def mod_table(cond8, w_mod, b_mod):
    depth, d, n = w_mod.shape
    tn = 1024
    return pl.pallas_call(
        _mod_kernel,
        grid=(depth, n // tn),
        in_specs=[pl.BlockSpec((8, d), lambda i, j: (0, 0)),
                  pl.BlockSpec((1, d, tn), lambda i, j: (i, 0, j)),
                  pl.BlockSpec((1, 1, tn), lambda i, j: (i, 0, j))],
        out_specs=pl.BlockSpec((1, 8, tn), lambda i, j: (i, 0, j)),
        out_shape=jax.ShapeDtypeStruct((depth, 8, n), F32),
        compiler_params=_cp(("parallel", "parallel")),
        name="mod_table",
    )(cond8, w_mod, b_mod.reshape(depth, 1, n))


INPROJ_COLS = 1024


def _inproj_kernel(h_ref, g_ref, sh_ref, sc_ref, *rest, n_w, n_ctx_tiles, rope_cols, q_cols, q_scale):
    w_refs, rest = rest[:n_w], rest[n_w:]
    if rope_cols:
        cos_ref, sin_ref = rest[:2]
        rest = rest[2:]
    o_refs = rest
    b, j = pl.program_id(0), pl.program_id(1)
    row = _mod_row(b, j, n_ctx_tiles)
    a = _rms(h_ref[0], g_ref[...]) * (1.0 + sc_ref[0, pl.ds(row, 1), :]) + sh_ref[0, pl.ds(row, 1), :]
    a = a.astype(BF16)
    for k, (w_ref, o_ref) in enumerate(zip(w_refs, o_refs)):
        n = w_ref.shape[1]
        for c0 in range(0, n, INPROJ_COLS):
            c1 = min(n, c0 + INPROJ_COLS)
            y = _dot(a, w_ref[:, c0:c1])
            if k == 0 and c0 < rope_cols:
                tn = c1 - c0
                lane = lax.broadcasted_iota(I32, y.shape, 1)
                first = (lane & (DIFF_HD - 1)) < (DIFF_HD // 2)
                part = jnp.where(first, pltpu.roll(y, tn - DIFF_HD // 2, axis=1),
                                 pltpu.roll(y, DIFF_HD // 2, axis=1))
                cs = jnp.concatenate([cos_ref[0]] * (tn // LANE), axis=1)
                sn = jnp.concatenate([sin_ref[0]] * (tn // LANE), axis=1)
                y = y * cs + part * sn
                if c0 < q_cols:
                    y = y * q_scale
            o_ref[0, :, c0:c1] = y.astype(o_ref.dtype)


def inproj(h, g, mods, layer, k_shift, ws, n_ctx, out_dtype=F32, rope=None):
    bsz, lt, d = h.shape
    nt = lt // TM
    kern = functools.partial(_inproj_kernel, n_w=len(ws), n_ctx_tiles=n_ctx // TM, rope_cols=rope[2] if rope else 0,
                             q_cols=rope[3] if rope else 0, q_scale=rope[4] if rope else 1.0)
    in_specs = [pl.BlockSpec((1, TM, d), lambda b, j: (b, j, 0)),
                pl.BlockSpec((1, d), lambda b, j: (0, 0)),
                pl.BlockSpec((1, 8, d), lambda b, j: (layer, 0, k_shift)),
                pl.BlockSpec((1, 8, d), lambda b, j: (layer, 0, k_shift + 1))]
    in_specs += [pl.BlockSpec(w.shape, lambda b, j: (0, 0)) for w in ws]
    args = [h, g.reshape(1, d), mods, mods, *ws]
    if rope:
        in_specs += [pl.BlockSpec((1, TM, LANE), lambda b, j: (0, j, 0))] * 2
        args += [rope[0][None], rope[1][None]]
    return pl.pallas_call(
        kern,
        grid=(bsz, nt),
        in_specs=in_specs,
        out_specs=[pl.BlockSpec((1, TM, w.shape[1]), lambda b, j: (b, j, 0)) for w in ws],
        out_shape=[jax.ShapeDtypeStruct((bsz, lt, w.shape[1]), out_dtype) for w in ws],
        compiler_params=_cp(("parallel", "parallel")),
        name="inproj",
    )(*args)


def _post_gla(refs, prm):
    o_f, o_b, gate, ng = refs
    o = o_f[0] + o_b[0]
    g = gate[0]
    outs = []
    for hd in range(GLA_HEADS):
        sl = slice(hd * GLA_DV, (hd + 1) * GLA_DV)
        outs.append(_rms(o[:, sl], ng[...]) * _silu(g[:, sl]))
    return jnp.concatenate(outs, axis=1)


def _post_diff(refs, prm):
    o, ng = refs
    x = o[0]
    outs = []
    for hd in range(DIFF_HEADS):
        sl = slice(hd * 2 * DIFF_HD, (hd + 1) * 2 * DIFF_HD)
        outs.append(_rms(x[:, sl], ng[...]) * (1.0 - prm["lam_init"]))
    return jnp.concatenate(outs, axis=1)


def _post_ssd(refs, prm):
    y_f, y_b, xs, z, dexp, ng = refs
    y = (y_f[0] + y_b[0] + dexp[...] * xs[0]) * _silu(z[0])
    gs = y.shape[1] // SSD_G
    outs = []
    for gi in range(SSD_G):
        sl = slice(gi * gs, (gi + 1) * gs)
        outs.append(_rms(y[:, sl], ng[:, sl]))
    return jnp.concatenate(outs, axis=1)


_POST = {"gla": (_post_gla, 4), "diff": (_post_diff, 2), "ssd": (_post_ssd, 6)}


def _finish_kernel(*refs, kind, prm, n_ctx_tiles):
    post, n_in = _POST[kind]
    mix = refs[:n_in]
    w_ref, h_ref, gm_ref, g2_ref, sh_ref, sc_ref, hn_ref, f_ref = refs[n_in:]
    b, j = pl.program_id(0), pl.program_id(1)
    row = _mod_row(b, j, n_ctx_tiles)
    y = post(mix, prm)
    o = _dot(y.astype(BF16), w_ref[...])
    hn = h_ref[0] + gm_ref[0, pl.ds(row, 1), :] * o
    hn_ref[0] = hn
    f_ref[0] = _rms(hn, g2_ref[...]) * (1.0 + sc_ref[0, pl.ds(row, 1), :]) + sh_ref[0, pl.ds(row, 1), :]


def finish(kind, mix_args, mix_specs, w_out, h, g2, mods, layer, n_ctx, prm=None):
    bsz, lt, d = h.shape
    nt = lt // TM
    dm = w_out.shape[0]
    tile = lambda: pl.BlockSpec((1, TM, d), lambda b, j: (b, j, 0))
    modspec = lambda k: pl.BlockSpec((1, 8, d), lambda b, j: (layer, 0, k))
    kern = functools.partial(_finish_kernel, kind=kind, prm=prm or {}, n_ctx_tiles=n_ctx // TM)
    return pl.pallas_call(
        kern,
        grid=(bsz, nt),
        in_specs=list(mix_specs) + [pl.BlockSpec((dm, d), lambda b, j: (0, 0)), tile(), modspec(2),
                                    pl.BlockSpec((1, d), lambda b, j: (0, 0)), modspec(3), modspec(4)],
        out_specs=[tile(), tile()],
        out_shape=[jax.ShapeDtypeStruct((bsz, lt, d), F32)] * 2,
        compiler_params=_cp(("parallel", "parallel")),
        name="finish_" + kind,
    )(*mix_args, w_out, h, mods, g2.reshape(1, d), mods, mods)


def _gla_kernel(q_ref, k_ref, v_ref, lo_ref, wa_ref, ba_ref, o_ref, st_ref, *, rev):
    c = GLA_CHUNK
    hk = GLA_HEADS * GLA_DK
    heads = [slice(hd * GLA_DK, (hd + 1) * GLA_DK) for hd in range(GLA_HEADS)]

    @pl.when(pl.program_id(1) == 0)
    def _():
        st_ref[...] = jnp.zeros_like(st_ref)

    q = q_ref[0] * (GLA_DK ** -0.5)
    k = k_ref[0]
    v = v_ref[0]
    z = _dot3(lo_ref[0], wa_ref[0]) + ba_ref[0]
    la = (jnp.minimum(z, 0.0) - jnp.log1p(jnp.exp(-jnp.abs(z)))) * (1.0 / GLA_TAU)

    ri = lax.broadcasted_iota(I32, (c, c), 0)
    ci = lax.broadcasted_iota(I32, (c, c), 1)
    row = lax.broadcasted_iota(I32, (c, hk), 0)
    tri = jnp.where((ci >= ri) if rev else (ci <= ri), 1.0, 0.0).astype(BF16)
    bsum = _dot_exact_rhs(tri, la)
    tot = bsum[0:1, :] if rev else bsum[c - 1:c, :]

    q16, k16 = q.astype(BF16), k.astype(BF16)
    eye = ci == ri
    a = [jnp.where(eye, _dot_nt(q16[:, sl], k16[:, sl]), 0.0) for sl in heads]
    s = c // 2
    while s >= 1:
        pos = row & (2 * s - 1)
        q_half = (pos < s) if rev else (pos >= s)
        if s >= 4:
            blk = bsum.reshape(c // (2 * s), 2 * s, hk)
            rr = s if rev else s - 1
            ref = jnp.broadcast_to(blk[:, rr:rr + 1, :], blk.shape).reshape(c, hk)
            e = jnp.where(q_half, bsum - ref, ref - bsum)
        elif s == 2:
            pos4 = row & 3
            nxt, prv = pltpu.roll(la, c - 1, axis=0), pltpu.roll(la, 1, axis=0)
            if rev:
                e = jnp.where(pos4 == 0, la + nxt, jnp.where(pos4 == 1, la, jnp.where(pos4 == 2, 0.0, prv)))
            else:
                e = jnp.where(pos4 == 3, la + prv, jnp.where(pos4 == 2, la, jnp.where(pos4 == 1, 0.0, nxt)))
        else:
            e = jnp.where(q_half, la, 0.0)
        f = jnp.exp(jnp.minimum(e, 0.0))
        qd = jnp.where(q_half, q * f, 0.0).astype(BF16)
        kd = jnp.where(q_half, 0.0, k * f).astype(BF16)
        same_block = (ri & -(2 * s)) == (ci & -(2 * s))
        a = [a[hd] + jnp.where(same_block, _dot_nt(qd[:, sl], kd[:, sl]), 0.0) for hd, sl in enumerate(heads)]
        s //= 2

    qe = (q * jnp.exp(bsum)).astype(BF16)
    kt = (k * jnp.exp(tot - bsum)).astype(BF16)
    et = jnp.exp(tot)
    for hd, sl in enumerate(heads):
        vh = v[:, hd * GLA_DV:(hd + 1) * GLA_DV]
        st = st_ref[hd]
        o_ref[0, :, hd * GLA_DV:(hd + 1) * GLA_DV] = (_dot(a[hd].astype(BF16), vh.astype(BF16))
                                                       + _dot_nt(qe[:, sl], st.astype(BF16)))
        st_ref[hd] = st * et[:, sl] + _dot(vh.T.astype(BF16), kt[:, sl])


def _scan_chunk(c, n_ctx_chunks, n_chunks, rev):
    if not rev:
        return c
    return jnp.where(c < n_ctx_chunks, n_ctx_chunks - 1 - c, n_chunks - 1 - (c - n_ctx_chunks))


def gla_scan(qkvg, lo, wa_pad, ba, n_ctx, rev):
    bsz, lt, _ = qkvg.shape
    c = GLA_CHUNK
    nch, ncc = lt // c, n_ctx // c
    hk, hv = GLA_HEADS * GLA_DK, GLA_HEADS * GLA_DV
    d = 1 if rev else 0
    cm = functools.partial(_scan_chunk, n_ctx_chunks=ncc, n_chunks=nch, rev=rev)
    return pl.pallas_call(
        functools.partial(_gla_kernel, rev=rev),
        grid=(bsz, nch),
        in_specs=[pl.BlockSpec((1, c, hk), lambda b, i: (b, cm(i), 0)),
                  pl.BlockSpec((1, c, hk), lambda b, i: (b, cm(i), 1)),
                  pl.BlockSpec((1, c, hv), lambda b, i: (b, cm(i), 1)),
                  pl.BlockSpec((1, c, LANE), lambda b, i: (b, cm(i), 0)),
                  pl.BlockSpec((1, LANE, hk), lambda b, i: (d, 0, 0)),
                  pl.BlockSpec((1, 1, hk), lambda b, i: (d, 0, 0))],
        out_specs=pl.BlockSpec((1, c, hv), lambda b, i: (b, cm(i), 0)),
        out_shape=jax.ShapeDtypeStruct((bsz, lt, hv), F32),
        scratch_shapes=[pltpu.VMEM((GLA_HEADS, GLA_DV, GLA_DK), F32)],
        compiler_params=_cp(("parallel", "arbitrary")),
        name="gla_scan_bwd" if rev else "gla_scan_fwd",
    )(qkvg, qkvg, qkvg, lo, wa_pad, ba)


def _diff_kernel(lam_ref, q_ref, k_ref, v_ref, o_ref, *, n_ctx, lam_init):
    j = pl.program_id(2)
    lv = lam_ref[...]
    l1 = jnp.sum(lv[0:1] * lv[1:2], axis=1, keepdims=True)
    l2 = jnp.sum(lv[2:3] * lv[3:4], axis=1, keepdims=True)
    lam = jnp.exp(l1) - jnp.exp(l2) + lam_init

    q = q_ref[0]
    lane = lax.broadcasted_iota(I32, q.shape, 1)
    zero = jnp.zeros_like(q)
    q0 = jnp.where(lane < DIFF_HD, q, zero)
    q1 = jnp.where(lane < DIFF_HD, zero, q)

    def attend(n_keys):
        k = k_ref[0, :n_keys, :]
        s0 = _dot_nt(q0, k)
        s1 = _dot_nt(q1, k)
        p0 = jnp.exp(s0 - jnp.max(s0, axis=1, keepdims=True))
        p1 = jnp.exp(s1 - jnp.max(s1, axis=1, keepdims=True))
        r0 = 1.0 / jnp.sum(p0, axis=1, keepdims=True)
        r1 = lam / jnp.sum(p1, axis=1, keepdims=True)
        a = p0 * r0 - p1 * r1
        o_ref[0] = _dot(a.astype(BF16), v_ref[0, :n_keys, :])

    @pl.when(j * TM < n_ctx)
    def _():
        attend(n_ctx)

    @pl.when(j * TM >= n_ctx)
    def _():
        attend(k_ref.shape[1])


def diff_attention(qkv, lam_vecs, n_ctx, lam_init):
    bsz, lt, _ = qkv.shape
    nh = DIFF_HEADS
    return pl.pallas_call(
        functools.partial(_diff_kernel, n_ctx=n_ctx, lam_init=lam_init),
        grid=(bsz, nh, lt // TM),
        in_specs=[pl.BlockSpec((4, DIFF_HD), lambda b, h, j: (0, 0)),
                  pl.BlockSpec((1, TM, LANE), lambda b, h, j: (b, j, h)),
                  pl.BlockSpec((1, lt, LANE), lambda b, h, j: (b, 0, nh + h)),
                  pl.BlockSpec((1, lt, LANE), lambda b, h, j: (b, 0, 2 * nh + h))],
        out_specs=pl.BlockSpec((1, TM, LANE), lambda b, h, j: (b, j, h)),
        out_shape=jax.ShapeDtypeStruct((bsz, lt, nh * LANE), F32),
        compiler_params=_cp(("parallel", "parallel", "arbitrary")),
        name="diff_attention",
    )(lam_vecs, qkv, qkv, qkv)


def _conv_kernel(x_ref, p_ref, n_ref, w_ref, b_ref, o_ref, *, n_ctx_tiles, n_tiles):
    j = pl.program_id(1)
    first = jnp.logical_or(j == 0, j == n_ctx_tiles)
    last = jnp.logical_or(j == n_ctx_tiles - 1, j == n_tiles - 1)
    x = x_ref[0]
    prev = jnp.where(first, 0.0, p_ref[0])
    nxt = jnp.where(last, 0.0, n_ref[0])
    xe = jnp.concatenate([prev, x, nxt], axis=0)
    ne = xe.shape[0]
    half = SSD_CONV // 2
    y = b_ref[...] + w_ref[half:half + 1, :] * x
    for t in range(SSD_CONV):
        if t == half:
            continue
        sh = pltpu.roll(xe, (half - t) % ne, axis=0)[8:8 + TM]
        y = y + w_ref[t:t + 1, :] * sh
    o_ref[0] = _silu(y)


def ssd_conv(xbc, conv_w, conv_b, n_ctx):
    bsz, lt, ch = xbc.shape
    nt = lt // TM
    r8 = TM // 8
    return pl.pallas_call(
        functools.partial(_conv_kernel, n_ctx_tiles=n_ctx // TM, n_tiles=nt),
        grid=(bsz, nt),
        in_specs=[pl.BlockSpec((1, TM, ch), lambda b, j: (b, j, 0)),
                  pl.BlockSpec((1, 8, ch), lambda b, j: (b, jnp.maximum(j * r8 - 1, 0), 0)),
                  pl.BlockSpec((1, 8, ch), lambda b, j: (b, jnp.minimum((j + 1) * r8, lt // 8 - 1), 0)),
                  pl.BlockSpec((SSD_CONV, ch), lambda b, j: (0, 0)),
                  pl.BlockSpec((1, ch), lambda b, j: (0, 0))],
        out_specs=pl.BlockSpec((1, TM, ch), lambda b, j: (b, j, 0)),
        out_shape=jax.ShapeDtypeStruct((bsz, lt, ch), F32),
        compiler_params=_cp(("parallel", "parallel")),
        name="ssd_conv",
    )(xbc, xbc, xbc, conv_w, conv_b.reshape(1, ch))


def _ssd_kernel(xs_ref, bm_ref, cm_ref, dt_ref, dtb_ref, al_ref, ex_ref, y_ref, st_ref, *, rev):
    qn = SSD_CHUNK
    gw = SSD_R * SSD_P

    @pl.when(pl.program_id(1) == 0)
    def _():
        st_ref[...] = jnp.zeros_like(st_ref)

    dt = _softplus(dt_ref[0] + dtb_ref[...])
    da = dt * (-jnp.exp(al_ref[...]))
    ri = lax.broadcasted_iota(I32, (qn, qn), 0)
    ci = lax.broadcasted_iota(I32, (qn, qn), 1)
    causal = (ci >= ri) if rev else (ci <= ri)
    tri = jnp.where(causal, 1.0, 0.0).astype(BF16)
    acum = _dot_exact_rhs(tri, da)
    acum_t = acum.T
    dt_t = dt.T
    tot = acum[0:1, :] if rev else acum[qn - 1:qn, :]
    lane = lax.broadcasted_iota(I32, (qn, 2 * SSD_P), 1)
    e_acum = jnp.exp(acum)
    w_state = jnp.exp(tot - acum) * dt
    e_tot8 = jnp.broadcast_to(jnp.exp(tot), (8, LANE))

    for g in range(SSD_G):
        xs = xs_ref[0, :, g * gw:(g + 1) * gw]
        bm = bm_ref[0, :, g * SSD_N:(g + 1) * SSD_N]
        cmat = cm_ref[0, :, g * SSD_N:(g + 1) * SSD_N]
        ex = ex_ref[g]

        cb = _dot_nt(cmat.astype(BF16), bm.astype(BF16))
        xs16 = xs.astype(BF16)
        pieces = []
        for rp in range(SSD_R // 2):
            acc = None
            for sub in range(2):
                r = g * SSD_R + 2 * rp + sub
                seg = acum[:, r:r + 1] - acum_t[r:r + 1, :]
                w = cb * jnp.exp(jnp.minimum(seg, 0.0)) * dt_t[r:r + 1, :]
                w = jnp.where(causal, w, 0.0).astype(BF16)
                xpair = xs16[:, rp * 2 * SSD_P:(rp + 1) * 2 * SSD_P]
                keep = (lane < SSD_P) if sub == 0 else (lane >= SSD_P)
                part = _dot(w, jnp.where(keep, xpair, jnp.zeros_like(xpair)))
                acc = part if acc is None else acc + part
            pieces.append(acc)
        y = jnp.concatenate(pieces, axis=1)

        st = st_ref[g]
        e_i = _dot_01(e_acum, ex)
        y_ref[0, :, g * gw:(g + 1) * gw] = y + _dot(cmat.astype(BF16), st.astype(BF16)) * e_i
        wt = _dot_01(w_state, ex)
        e_tot = _dot_01(e_tot8, ex)[0:1, :]
        st_ref[g] = st * e_tot + _dot(bm.T.astype(BF16), (xs * wt).astype(BF16))


def ssd_scan(xc, dt, dtb, alog, expand, n_ctx, rev):
    bsz, lt, _ = xc.shape
    qn = SSD_CHUNK
    nch, ncc = lt // qn, n_ctx // qn
    d = 1 if rev else 0
    gw = SSD_R * SSD_P
    cm = functools.partial(_scan_chunk, n_ctx_chunks=ncc, n_chunks=nch, rev=rev)
    din, gn = SSD_HEADS * SSD_P, SSD_G * SSD_N
    return pl.pallas_call(
        functools.partial(_ssd_kernel, rev=rev),
        grid=(bsz, nch),
        in_specs=[pl.BlockSpec((1, qn, din), lambda b, i: (b, cm(i), 0)),
                  pl.BlockSpec((1, qn, gn), lambda b, i: (b, cm(i), din // gn)),
                  pl.BlockSpec((1, qn, gn), lambda b, i: (b, cm(i), din // gn + 1)),
                  pl.BlockSpec((1, qn, LANE), lambda b, i: (b, cm(i), d)),
                  pl.BlockSpec((1, LANE), lambda b, i: (0, d)),
                  pl.BlockSpec((1, LANE), lambda b, i: (0, d)),
                  pl.BlockSpec((SSD_G, LANE, gw), lambda b, i: (0, 0, 0))],
        out_specs=pl.BlockSpec((1, qn, din), lambda b, i: (b, cm(i), 0)),
        out_shape=jax.ShapeDtypeStruct((bsz, lt, din), F32),
        scratch_shapes=[pltpu.VMEM((SSD_G, SSD_N, gw), F32)],
        compiler_params=_cp(("parallel", "arbitrary")),
        name="ssd_scan_bwd" if rev else "ssd_scan_fwd",
    )(xc, xc, xc, dt, dtb, alog, expand)


def _top16(tasks, big):
    def body(r, carry):
        for s_ref, ids, vals_ref, idx_ref in tasks:
            s = s_ref[...]
            m = jnp.max(s, axis=0, keepdims=True)
            cand = jnp.where(s == m, ids, big)
            sel = jnp.min(cand, axis=0, keepdims=True)
            s_ref[...] = jnp.where(cand == sel, -jnp.inf, s)
            vals_ref[pl.ds(r, 1), :] = m
            idx_ref[pl.ds(r, 1), :] = sel
        return carry
    lax.fori_loop(0, PEER_TOPK, body, 0, unroll=2)


def _peer_route_kernel(f_ref, wh_ref, wl_ref, kh_ref, kl_ref, a_ref, b_ref, g_ref,
                       q_ref, s1_ref, s2_ref, c_ref, v1_ref, i1_ref, v2_ref, i2_ref, vb_ref, ib_ref):
    h = pl.program_id(1)
    nk = PEER_KEYS

    @pl.when(h == 0)
    def _():
        fh, fl = _split2(f_ref[...])
        q_ref[...] = _dot(fh, wh_ref[...]) + _dot(fh, wl_ref[...]) + _dot(fl, wh_ref[...])

    kid = lax.broadcasted_iota(I32, (nk, TM), 0)
    for z, s_ref in enumerate((s1_ref, s2_ref)):
        col = pl.multiple_of((h * 2 + z) * nk, nk)
        qh, ql = _split2(q_ref[:, pl.ds(col, nk)])
        kh, kl = kh_ref[z], kl_ref[z]
        s_ref[...] = _dot_nt(kh, qh) + _dot_nt(kh, ql) + _dot_nt(kl, qh)
    _top16([(s1_ref, kid, v1_ref, i1_ref), (s2_ref, kid, v2_ref, i2_ref)], nk)

    v1, v2 = v1_ref[...], v2_ref[...]
    parts = [v1[0:1] + v2]
    parts += [v1[p:p + 1] + v2[0:8] for p in range(1, 8)]
    parts += [v1[8:16] + v2[0:1]]
    c_ref[...] = jnp.concatenate(parts, axis=0)
    nc = c_ref.shape[0]
    r = lax.broadcasted_iota(I32, (nc, TM), 0)
    flat = jnp.where(r < 16, r, jnp.where(r < 72, (1 + ((r - 16) >> 3)) * 16 + ((r - 16) & 7), (r - 64) * 16))
    _top16([(c_ref, flat, vb_ref, ib_ref)], 1 << 20)

    sel = ib_ref[...]
    p, qq = sel >> 4, sel & (PEER_TOPK - 1)
    i1, i2 = i1_ref[...], i2_ref[...]
    a = jnp.zeros_like(sel)
    b = jnp.zeros_like(sel)
    for t in range(PEER_TOPK):
        a = jnp.where(p == t, i1[t:t + 1], a)
        b = jnp.where(qq == t, i2[t:t + 1], b)
    vb = vb_ref[...]
    e = jnp.exp(vb - vb[0:1])
    a_ref[0] = a
    b_ref[0] = b
    g_ref[0] = e / jnp.sum(e, axis=0, keepdims=True)


def peer_route(f2, wq_hi, wq_lo, sk_hi, sk_lo):
    t, d = f2.shape
    nq = wq_hi.shape[1]
    k = PEER_TOPK
    outspec = pl.BlockSpec((1, k, TM), lambda i, h: (h, 0, i))
    return pl.pallas_call(
        _peer_route_kernel,
        grid=(t // TM, PEER_HEADS),
        in_specs=[pl.BlockSpec((TM, d), lambda i, h: (i, 0)),
                  pl.BlockSpec((d, nq), lambda i, h: (0, 0)),
                  pl.BlockSpec((d, nq), lambda i, h: (0, 0)),
                  pl.BlockSpec(sk_hi.shape, lambda i, h: (0, 0, 0)),
                  pl.BlockSpec(sk_lo.shape, lambda i, h: (0, 0, 0))],
        out_specs=[outspec, outspec, outspec],
        out_shape=[jax.ShapeDtypeStruct((PEER_HEADS, k, t), I32)] * 2 + [jax.ShapeDtypeStruct((PEER_HEADS, k, t), F32)],
        scratch_shapes=[pltpu.VMEM((TM, nq), F32), pltpu.VMEM((PEER_KEYS, TM), F32),
                        pltpu.VMEM((PEER_KEYS, TM), F32), pltpu.VMEM((80, TM), F32),
                        pltpu.VMEM((k, TM), F32), pltpu.VMEM((k, TM), I32),
                        pltpu.VMEM((k, TM), F32), pltpu.VMEM((k, TM), I32),
                        pltpu.VMEM((k, TM), F32), pltpu.VMEM((k, TM), I32)],
        compiler_params=_cp(("parallel", "arbitrary")),
        name="peer_route",
    )(f2, wq_hi, wq_lo, sk_hi, sk_lo)


def _peer_dense_kernel(f_ref, h_ref, gm_ref, a_ref, b_ref, g_ref, u_ref, v_ref, o_ref,
                       f16_ref, ar_ref, br_ref, gr_ref, gs_ref, acc_ref, *, tiles_per_batch, n_ctx_tiles):
    i, e = pl.program_id(0), pl.program_id(1)
    nk = PEER_KEYS
    half = nk // 2
    hi_mask = jnp.uint32(0xFFFF0000)

    @pl.when(e == 0)
    def _():
        f16_ref[...] = f_ref[...].astype(BF16)
        acc_ref[...] = jnp.zeros_like(acc_ref)
        ar_ref[...] = a_ref[...].T
        br_ref[...] = b_ref[...].T
        gr_ref[...] = g_ref[...].T
        r = lax.broadcasted_iota(I32, (nk, nk), 0)
        key1 = jnp.where(r < half, 2 * r, 2 * (r - half) + 1)
        key2 = r

        def per_token(t, carry):
            arow = ar_ref[pl.ds(t, 1), :]
            brow = br_ref[pl.ds(t, 1), :]
            grow = gr_ref[pl.ds(t, 1), :]
            ga = jnp.where(key1 == arow, grow, 0.0).astype(BF16)
            ob = jnp.where(key2 == brow, 1.0, 0.0).astype(BF16)
            gm = _dot_nt(ga, ob).astype(BF16).astype(F32)
            bits = lax.bitcast_convert_type(gm, jnp.uint32)
            off = pl.multiple_of(t * G_PITCH, 8)
            gs_ref[pl.ds(off, half), :] = bits[:half] | (bits[half:] >> 16)
            return carry
        lax.fori_loop(0, PEER_TM, per_token, 0, unroll=8)

    s = _dot_nt(f16_ref[...], u_ref[...])
    gts = []
    for c in range(PEER_TE // (2 * nk)):
        word = gs_ref[pl.ds(e * (PEER_TE // (2 * nk)) + c, PEER_TM, stride=G_PITCH), :]
        gts.append(lax.bitcast_convert_type(word & hi_mask, F32))
        gts.append(lax.bitcast_convert_type(word << 16, F32))
    gt = jnp.concatenate(gts, axis=1)
    act = 0.5 * s * (1.0 + lax.erf(s * (2.0 ** -0.5)))
    acc_ref[...] += _dot((act * gt).astype(BF16), v_ref[...])

    @pl.when(e == pl.num_programs(1) - 1)
    def _():
        for sub in range(PEER_TM // TM):
            j = i * (PEER_TM // TM) + sub
            row = _mod_row(j // tiles_per_batch, j % tiles_per_batch, n_ctx_tiles)
            sl = slice(sub * TM, (sub + 1) * TM)
            o_ref[sl, :] = h_ref[sl, :] + gm_ref[0, pl.ds(row, 1), :] * acc_ref[sl, :]


def peer_dense(f2, h2, mods, layer, a_t, b_t, g_t, u16, v16, tiles_per_batch, n_ctx):
    t, d = f2.shape
    ne = u16.shape[0]
    hk = PEER_HEADS * PEER_TOPK
    kern = functools.partial(_peer_dense_kernel, tiles_per_batch=tiles_per_batch, n_ctx_tiles=n_ctx // TM)
    tok = lambda: pl.BlockSpec((PEER_TM, d), lambda i, e: (i, 0))
    rt = lambda: pl.BlockSpec((hk, PEER_TM), lambda i, e: (0, i))
    return pl.pallas_call(
        kern,
        grid=(t // PEER_TM, ne // PEER_TE),
        in_specs=[tok(), tok(), pl.BlockSpec((1, 8, d), lambda i, e: (layer, 0, 5)), rt(), rt(), rt(),
                  pl.BlockSpec((PEER_TE, d), lambda i, e: (e, 0)),
                  pl.BlockSpec((PEER_TE, d), lambda i, e: (e, 0))],
        out_specs=tok(),
        out_shape=jax.ShapeDtypeStruct((t, d), F32),
        scratch_shapes=[pltpu.VMEM((PEER_TM, d), BF16), pltpu.VMEM((PEER_TM, hk), I32),
                        pltpu.VMEM((PEER_TM, hk), I32), pltpu.VMEM((PEER_TM, hk), F32),
                        pltpu.VMEM((PEER_TM * G_PITCH, PEER_KEYS), jnp.uint32), pltpu.VMEM((PEER_TM, d), F32)],
        compiler_params=_cp(("parallel", "arbitrary")),
        name="peer_dense",
    )(f2, h2, mods, a_t, b_t, g_t, u16, v16)


def peer_ffn(f, h, mods, layer, wq, subkeys, u, v, n_ctx):
    bsz, lt, d = h.shape
    t = bsz * lt
    f2, h2 = f.reshape(t, d), h.reshape(t, d)
    wq_hi, wq_lo = _split2(wq)
    sk_hi, sk_lo = _split2(subkeys)
    a, b, g = peer_route(f2, wq_hi, wq_lo, sk_hi, sk_lo)
    hk = PEER_HEADS * PEER_TOPK
    out = peer_dense(f2, h2, mods, layer, a.reshape(hk, t), b.reshape(hk, t), g.reshape(hk, t),
                     u.astype(BF16), v.astype(BF16), lt // TM, n_ctx)
    return out.reshape(bsz, lt, d)


def _final_kernel(h_ref, g_ref, o_ref):
    o_ref[0] = _rms(h_ref[0], g_ref[...])


def final_norm(h, g, n_ctx):
    bsz, lt, d = h.shape
    l = lt - n_ctx
    off = n_ctx // TM
    return pl.pallas_call(
        _final_kernel,
        grid=(bsz, l // TM),
        in_specs=[pl.BlockSpec((1, TM, d), lambda b, j: (b, j + off, 0)),
                  pl.BlockSpec((1, d), lambda b, j: (0, 0))],
        out_specs=pl.BlockSpec((1, TM, d), lambda b, j: (b, j, 0)),
        out_shape=jax.ShapeDtypeStruct((bsz, l, d), F32),
        compiler_params=_cp(("parallel", "parallel")),
        name="final_norm",
    )(h, g.reshape(1, d))


def _rope_tables(l, n_ctx):
    rows = l // GRID_W
    row = jnp.repeat(jnp.arange(rows), GRID_W).astype(F32)
    col = jnp.tile(jnp.arange(GRID_W), rows).astype(F32)
    n_freq = DIFF_HD // 4
    freqs = ROPE_THETA ** (-jnp.arange(n_freq, dtype=F32) / n_freq)
    ang = jnp.concatenate([row[:, None] * freqs, col[:, None] * freqs], axis=-1)
    cos, sin = jnp.cos(ang), jnp.sin(ang)
    cos = jnp.concatenate([jnp.ones((n_ctx, DIFF_HD // 2), F32), cos], axis=0)
    sin = jnp.concatenate([jnp.zeros((n_ctx, DIFF_HD // 2), F32), sin], axis=0)
    cos_t = jnp.concatenate([cos, cos, cos, cos], axis=1)
    sin_t = jnp.concatenate([-sin, sin, -sin, sin], axis=1)
    return cos_t, sin_t


def _gla_layer(h, mods, i, j, p, n_ctx):
    hk = GLA_HEADS * GLA_DK
    hv = GLA_HEADS * GLA_DV
    w_in = p["gla_w_in"][j]
    n_main = 2 * hk + 2 * hv
    w_lo = jnp.pad(w_in[:, n_main:], ((0, 0), (0, LANE - 2 * GLA_RANK))).astype(BF16)
    qkvg, lo = inproj(h, p["norm1_g"][i], mods, i, 0, [w_in[:, :n_main].astype(BF16), w_lo], n_ctx)
    wa = p["gla_w_alpha"][j]
    wa_pad = jnp.stack([jnp.pad(wa[0], ((0, LANE - GLA_RANK), (0, 0))),
                        jnp.pad(wa[1], ((GLA_RANK, LANE - 2 * GLA_RANK), (0, 0)))])
    ba = p["gla_b_alpha"][j].reshape(2, 1, hk)
    o_f = gla_scan(qkvg, lo, wa_pad, ba, n_ctx, False)
    o_b = gla_scan(qkvg, lo, wa_pad, ba, n_ctx, True)
    tile = lambda: pl.BlockSpec((1, TM, hv), lambda b, t: (b, t, 0))
    specs = [tile(), tile(), pl.BlockSpec((1, TM, hv), lambda b, t: (b, t, 2)),
             pl.BlockSpec((1, GLA_DV), lambda b, t: (0, 0))]
    args = [o_f, o_b, qkvg, p["gla_norm_g"][j].reshape(1, GLA_DV)]
    return finish("gla", args, specs, p["gla_w_out"][j].astype(BF16), h, p["norm2_g"][i], mods, i, n_ctx)


def _diff_layer(h, mods, i, j, p, n_ctx, rope_tabs):
    wd = DIFF_HEADS * 2 * DIFF_HD
    cos_t, sin_t = rope_tabs
    lt = h.shape[1]
    qkv, = inproj(h, p["norm1_g"][i], mods, i, 0, [p["diff_w_in"][j].astype(BF16)], n_ctx,
                  out_dtype=BF16, rope=(cos_t, sin_t, 2 * wd, wd, DIFF_HD ** -0.5))
    lam_init = 0.8 - 0.6 * math.exp(-0.3 * i)
    o = diff_attention(qkv, p["diff_lambda"][j], n_ctx, lam_init)
    specs = [pl.BlockSpec((1, TM, wd), lambda b, t: (b, t, 0)),
             pl.BlockSpec((1, 2 * DIFF_HD), lambda b, t: (0, 0))]
    args = [o, p["diff_norm_g"][j].reshape(1, 2 * DIFF_HD)]
    return finish("diff", args, specs, p["diff_w_out"][j].astype(BF16), h, p["norm2_g"][i], mods, i, n_ctx,
                  prm={"lam_init": lam_init})


def _ssd_layer(h, mods, i, j, p, n_ctx):
    din = SSD_HEADS * SSD_P
    gn = SSD_G * SSD_N
    w_in = p["ssd_w_in"][j]
    g1 = p["norm1_g"][i]
    w_dt = w_in[:, 2 * din + 2 * gn:].reshape(-1, 2, SSD_HEADS)
    w_dt = jnp.pad(w_dt, ((0, 0), (0, 0), (0, LANE - SSD_HEADS))).reshape(-1, 2 * LANE)
    z, xbc, dt = inproj(h, g1, mods, i, 0, [w_in[:, :din].astype(BF16),
                                              w_in[:, din:2 * din + 2 * gn].astype(BF16), w_dt.astype(BF16)], n_ctx)
    pad_h = lambda a: jnp.pad(a, ((0, 0), (0, LANE - SSD_HEADS))).reshape(1, -1)
    dtb, alog = pad_h(p["ssd_dt_bias"][j]), pad_h(p["ssd_a_log"][j])
    xc = ssd_conv(xbc, p["ssd_conv_w"][j], p["ssd_conv_b"][j], n_ctx)
    head_of_col = jnp.arange(SSD_G)[:, None, None] * SSD_R + jnp.arange(SSD_R * SSD_P)[None, None, :] // SSD_P
    expand = (jnp.arange(LANE)[None, :, None] == head_of_col).astype(BF16)
    y_f = ssd_scan(xc, dt, dtb, alog, expand, n_ctx, False)
    y_b = ssd_scan(xc, dt, dtb, alog, expand, n_ctx, True)
    tile = lambda: pl.BlockSpec((1, TM, din), lambda b, t: (b, t, 0))
    row = lambda: pl.BlockSpec((1, din), lambda b, t: (0, 0))
    specs = [tile(), tile(), tile(), tile(), row(), row()]
    dexp = jnp.repeat(p["ssd_d"][j], SSD_P).reshape(1, din)
    args = [y_f, y_b, xc, z, dexp, p["ssd_norm_g"][j].reshape(1, din)]
    return finish("ssd", args, specs, p["ssd_w_out"][j].astype(BF16), h, p["norm2_g"][i], mods, i, n_ctx)


def kernel(x, c, ctx, c_ctx, norm1_g, norm2_g, w_mod, b_mod, peer_wq, peer_subkeys, peer_u, peer_v, gla_w_in, gla_w_alpha, gla_b_alpha, gla_norm_g, gla_w_out, diff_w_in, diff_lambda, diff_norm_g, diff_w_out, ssd_w_in, ssd_conv_w, ssd_conv_b, ssd_dt_bias, ssd_a_log, ssd_d, ssd_norm_g, ssd_w_out, final_g):
    p = dict(norm1_g=norm1_g, norm2_g=norm2_g, gla_w_in=gla_w_in, gla_w_alpha=gla_w_alpha, gla_b_alpha=gla_b_alpha,
             gla_norm_g=gla_norm_g, gla_w_out=gla_w_out, diff_w_in=diff_w_in, diff_lambda=diff_lambda,
             diff_norm_g=diff_norm_g, diff_w_out=diff_w_out, ssd_w_in=ssd_w_in, ssd_conv_w=ssd_conv_w,
             ssd_conv_b=ssd_conv_b, ssd_dt_bias=ssd_dt_bias, ssd_a_log=ssd_a_log, ssd_d=ssd_d,
             ssd_norm_g=ssd_norm_g, ssd_w_out=ssd_w_out)
    bsz, l, d = x.shape
    n_ctx = ctx.shape[1]
    depth = w_mod.shape[0]
    assert bsz <= 4 and n_ctx % TM == 0 and l % TM == 0 and l % GRID_W == 0
    cond8 = jnp.concatenate([c, jnp.zeros((4 - bsz, d), F32), c_ctx[None], jnp.zeros((3, d), F32)], axis=0)
    mods = mod_table(cond8, w_mod, b_mod)
    rope_tabs = _rope_tables(l, n_ctx)
    h = jnp.concatenate([ctx, x], axis=1)
    for i in range(depth):
        kind, j = i % N_MIXERS, i // N_MIXERS
        if kind == 0:
            h, f = _gla_layer(h, mods, i, j, p, n_ctx)
        elif kind == 1:
            h, f = _diff_layer(h, mods, i, j, p, n_ctx, rope_tabs)
        else:
            h, f = _ssd_layer(h, mods, i, j, p, n_ctx)
        h = peer_ffn(f, h, mods, i, peer_wq[i], peer_subkeys[i], peer_u[i], peer_v[i], n_ctx)
    return final_norm(h, final_g, n_ctx)
```

```python
import functools
import math

import jax
import jax.numpy as jnp
from jax import lax
from jax.experimental import pallas as pl
from jax.experimental.pallas import tpu as pltpu

F32 = jnp.float32
BF16 = jnp.bfloat16
I32 = jnp.int32

EPS = 1e-6
GRID_W = 64
ROPE_THETA = 10000.0
N_MIXERS = 3

GLA_HEADS, GLA_DK, GLA_DV, GLA_RANK, GLA_TAU = 4, 128, 256, 16, 16.0
GLA_CHUNK = 128
DIFF_HEADS, DIFF_HD = 8, 64
DIFF_HPS = 2
SSD_HEADS, SSD_P, SSD_N, SSD_G, SSD_CONV, SSD_CHUNK = 32, 64, 128, 4, 5, 128
SSD_R = SSD_HEADS // SSD_G
PEER_KEYS, PEER_HEADS, PEER_TOPK = 128, 8, 16

TM = 256
LANE = 128
ROUTE_TM = 512
PEER_TM = 512
PEER_TE = 1024
G_PITCH = 72
VMEM_LIMIT = 56 * 1024 * 1024


def _cp(sem, vmem=VMEM_LIMIT):
    return pltpu.CompilerParams(dimension_semantics=sem, vmem_limit_bytes=vmem)


def _dot(a, b):
    return jnp.dot(a, b, preferred_element_type=F32)


def _dot_nt(a, b):
    return lax.dot_general(a, b, (((1,), (1,)), ((), ())), preferred_element_type=F32)


def _split2(x):
    hi = x.astype(BF16)
    lo = (x - hi.astype(F32)).astype(BF16)
    return hi, lo


def _dot3(a, b, dot=_dot):
    ah, al = _split2(a)
    bh, bl = _split2(b)
    return dot(ah, bh) + dot(ah, bl) + dot(al, bh)


def _dot_exact_rhs(w01, x):
    h1 = x.astype(BF16)
    r1 = x - h1.astype(F32)
    h2 = r1.astype(BF16)
    h3 = (r1 - h2.astype(F32)).astype(BF16)
    return _dot(w01, h1) + _dot(w01, h2) + _dot(w01, h3)


def _dot_01(x, w01):
    xh, xl = _split2(x)
    return _dot(xh, w01) + _dot(xl, w01)


def _silu(x):
    return x * (1.0 / (1.0 + jnp.exp(-x)))


def _softplus(x):
    return jnp.maximum(x, 0.0) + jnp.log1p(jnp.exp(-jnp.abs(x)))


def _rms(x, g):
    return x * lax.rsqrt(jnp.mean(x * x, axis=-1, keepdims=True) + EPS) * g


def _mod_row(b, j, n_ctx_tiles):
    return jnp.where(j < n_ctx_tiles, 4, b)


def _mod_kernel(cond_ref, w_ref, b_ref, o_ref):
    c = _silu(cond_ref[...])
    o_ref[0] = _dot3(c, w_ref[0]) + b_ref[0]


def mod_table(cond8, w_mod, b_mod):
    depth, d, n = w_mod.shape
    tn = 1024
    return pl.pallas_call(
        _mod_kernel,
        grid=(depth, n // tn),
        in_specs=[pl.BlockSpec((8, d), lambda i, j: (0, 0)),
                  pl.BlockSpec((1, d, tn), lambda i, j: (i, 0, j)),
                  pl.BlockSpec((1, 1, tn), lambda i, j: (i, 0, j))],
        out_specs=pl.BlockSpec((1, 8, tn), lambda i, j: (i, 0, j)),
        out_shape=jax.ShapeDtypeStruct((depth, 8, n), F32),
        compiler_params=_cp(("parallel", "parallel")),
        name="mod_table",
    )(cond8, w_mod, b_mod.reshape(depth, 1, n))


INPROJ_COLS = 1024


def _inproj_kernel(h_ref, g_ref, sh_ref, sc_ref, *rest, n_w, n_ctx_tiles, rope_cols, q_cols, q_scale):
    w_refs, rest = rest[:n_w], rest[n_w:]
    if rope_cols:
        cos_ref, sin_ref = rest[:2]
        rest = rest[2:]
    o_refs = rest
    b, j = pl.program_id(0), pl.program_id(1)
    row = _mod_row(b, j, n_ctx_tiles)
    a = _rms(h_ref[0], g_ref[...]) * (1.0 + sc_ref[0, pl.ds(row, 1), :]) + sh_ref[0, pl.ds(row, 1), :]
    a = a.astype(BF16)
    for k, (w_ref, o_ref) in enumerate(zip(w_refs, o_refs)):
        n = w_ref.shape[1]
        for c0 in range(0, n, INPROJ_COLS):
            c1 = min(n, c0 + INPROJ_COLS)
            y = _dot(a, w_ref[:, c0:c1])
            if k == 0 and c0 < rope_cols:
                tn = c1 - c0
                lane = lax.broadcasted_iota(I32, y.shape, 1)
                first = (lane & (DIFF_HD - 1)) < (DIFF_HD // 2)
                part = jnp.where(first, pltpu.roll(y, tn - DIFF_HD // 2, axis=1),
                                 pltpu.roll(y, DIFF_HD // 2, axis=1))
                cs = jnp.concatenate([cos_ref[0]] * (tn // LANE), axis=1)
                sn = jnp.concatenate([sin_ref[0]] * (tn // LANE), axis=1)
                y = y * cs + part * sn
                if c0 < q_cols:
                    y = y * q_scale
            o_ref[0, :, c0:c1] = y.astype(o_ref.dtype)


def inproj(h, g, mods, layer, k_shift, ws, n_ctx, out_dtype=F32, rope=None):
    bsz, lt, d = h.shape
    nt = lt // TM
    kern = functools.partial(_inproj_kernel, n_w=len(ws), n_ctx_tiles=n_ctx // TM, rope_cols=rope[2] if rope else 0,
                             q_cols=rope[3] if rope else 0, q_scale=rope[4] if rope else 1.0)
    in_specs = [pl.BlockSpec((1, TM, d), lambda b, j: (b, j, 0)),
                pl.BlockSpec((1, d), lambda b, j: (0, 0)),
                pl.BlockSpec((1, 8, d), lambda b, j: (layer, 0, k_shift)),
                pl.BlockSpec((1, 8, d), lambda b, j: (layer, 0, k_shift + 1))]
    in_specs += [pl.BlockSpec(w.shape, lambda b, j: (0, 0)) for w in ws]
    args = [h, g.reshape(1, d), mods, mods, *ws]
    if rope:
        in_specs += [pl.BlockSpec((1, TM, LANE), lambda b, j: (0, j, 0))] * 2
        args += [rope[0][None], rope[1][None]]
    return pl.pallas_call(
        kern,
        grid=(bsz, nt),
        in_specs=in_specs,
        out_specs=[pl.BlockSpec((1, TM, w.shape[1]), lambda b, j: (b, j, 0)) for w in ws],
        out_shape=[jax.ShapeDtypeStruct((bsz, lt, w.shape[1]), out_dtype) for w in ws],
        compiler_params=_cp(("parallel", "parallel")),
        name="inproj",
    )(*args)


def _post_gla(refs, prm):
    o_f, o_b, gate, ng = refs
    o = o_f[0] + o_b[0]
    g = gate[0]
    outs = []
    for hd in range(GLA_HEADS):
        sl = slice(hd * GLA_DV, (hd + 1) * GLA_DV)
        outs.append(_rms(o[:, sl], ng[...]) * _silu(g[:, sl]))
    return jnp.concatenate(outs, axis=1)


def _post_diff(refs, prm):
    o, ng = refs
    x = o[0]
    outs = []
    for hd in range(DIFF_HEADS):
        sl = slice(hd * 2 * DIFF_HD, (hd + 1) * 2 * DIFF_HD)
        outs.append(_rms(x[:, sl], ng[...]) * (1.0 - prm["lam_init"]))
    return jnp.concatenate(outs, axis=1)


def _post_ssd(refs, prm):
    y_f, y_b, xs, z, dexp, ng = refs
    y = (y_f[0] + y_b[0] + dexp[...] * xs[0]) * _silu(z[0])
    gs = y.shape[1] // SSD_G
    outs = []
    for gi in range(SSD_G):
        sl = slice(gi * gs, (gi + 1) * gs)
        outs.append(_rms(y[:, sl], ng[:, sl]))
    return jnp.concatenate(outs, axis=1)


_POST = {"gla": (_post_gla, 4), "diff": (_post_diff, 2), "ssd": (_post_ssd, 6)}


def _finish_kernel(*refs, kind, prm, n_ctx_tiles):
    post, n_in = _POST[kind]
    mix = refs[:n_in]
    w_ref, h_ref, gm_ref, g2_ref, sh_ref, sc_ref, hn_ref, f_ref = refs[n_in:]
    b, j = pl.program_id(0), pl.program_id(1)
    row = _mod_row(b, j, n_ctx_tiles)
    y = post(mix, prm)
    o = _dot(y.astype(BF16), w_ref[...])
    hn = h_ref[0] + gm_ref[0, pl.ds(row, 1), :] * o
    hn_ref[0] = hn
    f_ref[0] = _rms(hn, g2_ref[...]) * (1.0 + sc_ref[0, pl.ds(row, 1), :]) + sh_ref[0, pl.ds(row, 1), :]


def finish(kind, mix_args, mix_specs, w_out, h, g2, mods, layer, n_ctx, prm=None):
    bsz, lt, d = h.shape
    nt = lt // TM
    dm = w_out.shape[0]
    tile = lambda: pl.BlockSpec((1, TM, d), lambda b, j: (b, j, 0))
    modspec = lambda k: pl.BlockSpec((1, 8, d), lambda b, j: (layer, 0, k))
    kern = functools.partial(_finish_kernel, kind=kind, prm=prm or {}, n_ctx_tiles=n_ctx // TM)
    return pl.pallas_call(
        kern,
        grid=(bsz, nt),
        in_specs=list(mix_specs) + [pl.BlockSpec((dm, d), lambda b, j: (0, 0)), tile(), modspec(2),
                                    pl.BlockSpec((1, d), lambda b, j: (0, 0)), modspec(3), modspec(4)],
        out_specs=[tile(), tile()],
        out_shape=[jax.ShapeDtypeStruct((bsz, lt, d), F32)] * 2,
        compiler_params=_cp(("parallel", "parallel")),
        name="finish_" + kind,
    )(*mix_args, w_out, h, mods, g2.reshape(1, d), mods, mods)


def _gla_kernel(q_ref, k_ref, v_ref, lo_ref, wa_ref, ba_ref, o_ref, st_ref, *, rev):
    c = GLA_CHUNK
    hk = GLA_HEADS * GLA_DK
    heads = [slice(hd * GLA_DK, (hd + 1) * GLA_DK) for hd in range(GLA_HEADS)]

    @pl.when(pl.program_id(1) == 0)
    def _():
        st_ref[...] = jnp.zeros_like(st_ref)

    q = q_ref[0] * (GLA_DK ** -0.5)
    k = k_ref[0]
    v = v_ref[0]
    z = _dot3(lo_ref[0], wa_ref[0]) + ba_ref[0]
    la = (jnp.minimum(z, 0.0) - jnp.log1p(jnp.exp(-jnp.abs(z)))) * (1.0 / GLA_TAU)

    ri = lax.broadcasted_iota(I32, (c, c), 0)
    ci = lax.broadcasted_iota(I32, (c, c), 1)
    row = lax.broadcasted_iota(I32, (c, hk), 0)
    tri = jnp.where((ci >= ri) if rev else (ci <= ri), 1.0, 0.0).astype(BF16)
    bsum = _dot_exact_rhs(tri, la)
    tot = bsum[0:1, :] if rev else bsum[c - 1:c, :]

    q16, k16 = q.astype(BF16), k.astype(BF16)
    eye = ci == ri
    a = [jnp.where(eye, _dot_nt(q16[:, sl], k16[:, sl]), 0.0) for sl in heads]
    s = c // 2
    while s >= 1:
        pos = row & (2 * s - 1)
        q_half = (pos < s) if rev else (pos >= s)
        if s >= 4:
            blk = bsum.reshape(c // (2 * s), 2 * s, hk)
            rr = s if rev else s - 1
            ref = jnp.broadcast_to(blk[:, rr:rr + 1, :], blk.shape).reshape(c, hk)
            e = jnp.where(q_half, bsum - ref, ref - bsum)
        elif s == 2:
            pos4 = row & 3
            nxt, prv = pltpu.roll(la, c - 1, axis=0), pltpu.roll(la, 1, axis=0)
            if rev:
                e = jnp.where(pos4 == 0, la + nxt, jnp.where(pos4 == 1, la, jnp.where(pos4 == 2, 0.0, prv)))
            else:
                e = jnp.where(pos4 == 3, la + prv, jnp.where(pos4 == 2, la, jnp.where(pos4 == 1, 0.0, nxt)))
        else:
            e = jnp.where(q_half, la, 0.0)
        f = jnp.exp(jnp.minimum(e, 0.0))
        qd = jnp.where(q_half, q * f, 0.0).astype(BF16)
        kd = jnp.where(q_half, 0.0, k * f).astype(BF16)
        same_block = (ri & -(2 * s)) == (ci & -(2 * s))
        a = [a[hd] + jnp.where(same_block, _dot_nt(qd[:, sl], kd[:, sl]), 0.0) for hd, sl in enumerate(heads)]
        s //= 2

    qe = (q * jnp.exp(bsum)).astype(BF16)
    kt = (k * jnp.exp(tot - bsum)).astype(BF16)
    et = jnp.exp(tot)
    for hd, sl in enumerate(heads):
        vh = v[:, hd * GLA_DV:(hd + 1) * GLA_DV]
        st = st_ref[hd]
        o_ref[0, :, hd * GLA_DV:(hd + 1) * GLA_DV] = (_dot(a[hd].astype(BF16), vh.astype(BF16))
                                                       + _dot_nt(qe[:, sl], st.astype(BF16)))
        st_ref[hd] = st * et[:, sl] + _dot(vh.T.astype(BF16), kt[:, sl])


def _scan_chunk(c, n_ctx_chunks, n_chunks, rev):
    if not rev:
        return c
    return jnp.where(c < n_ctx_chunks, n_ctx_chunks - 1 - c, n_chunks - 1 - (c - n_ctx_chunks))


def gla_scan(qkvg, lo, wa_pad, ba, n_ctx, rev):
    bsz, lt, _ = qkvg.shape
    c = GLA_CHUNK
    nch, ncc = lt // c, n_ctx // c
    hk, hv = GLA_HEADS * GLA_DK, GLA_HEADS * GLA_DV
    d = 1 if rev else 0
    cm = functools.partial(_scan_chunk, n_ctx_chunks=ncc, n_chunks=nch, rev=rev)
    return pl.pallas_call(
        functools.partial(_gla_kernel, rev=rev),
        grid=(bsz, nch),
        in_specs=[pl.BlockSpec((1, c, hk), lambda b, i: (b, cm(i), 0)),
                  pl.BlockSpec((1, c, hk), lambda b, i: (b, cm(i), 1)),
                  pl.BlockSpec((1, c, hv), lambda b, i: (b, cm(i), 1)),
                  pl.BlockSpec((1, c, LANE), lambda b, i: (b, cm(i), 0)),
                  pl.BlockSpec((1, LANE, hk), lambda b, i: (d, 0, 0)),
                  pl.BlockSpec((1, 1, hk), lambda b, i: (d, 0, 0))],
        out_specs=pl.BlockSpec((1, c, hv), lambda b, i: (b, cm(i), 0)),
        out_shape=jax.ShapeDtypeStruct((bsz, lt, hv), F32),
        scratch_shapes=[pltpu.VMEM((GLA_HEADS, GLA_DV, GLA_DK), F32)],
        compiler_params=_cp(("parallel", "arbitrary")),
        name="gla_scan_bwd" if rev else "gla_scan_fwd",
    )(qkvg, qkvg, qkvg, lo, wa_pad, ba)


def _diff_kernel(lam_ref, q_ref, k_ref, v_ref, o_ref, *, n_ctx, lam_init):
    j = pl.program_id(2)
    lv = lam_ref[...]
    l1 = jnp.sum(lv[0:1] * lv[1:2], axis=1, keepdims=True)
    l2 = jnp.sum(lv[2:3] * lv[3:4], axis=1, keepdims=True)
    lam = jnp.exp(l1) - jnp.exp(l2) + lam_init

    lane = lax.broadcasted_iota(I32, (TM, LANE), 1)

    def attend(n_keys):
        for hh in range(DIFF_HPS):
            cols = slice(hh * LANE, (hh + 1) * LANE)
            q = q_ref[0, :, cols]
            zero = jnp.zeros_like(q)
            q0 = jnp.where(lane < DIFF_HD, q, zero)
            q1 = jnp.where(lane < DIFF_HD, zero, q)
            k = k_ref[0, :n_keys, cols]
            s0 = _dot_nt(q0, k)
            s1 = _dot_nt(q1, k)
            p0 = jnp.exp(s0 - jnp.max(s0, axis=1, keepdims=True))
            p1 = jnp.exp(s1 - jnp.max(s1, axis=1, keepdims=True))
            r0 = 1.0 / jnp.sum(p0, axis=1, keepdims=True)
            r1 = lam / jnp.sum(p1, axis=1, keepdims=True)
            a = p0 * r0 - p1 * r1
            o_ref[0, :, cols] = _dot(a.astype(BF16), v_ref[0, :n_keys, cols])

    @pl.when(j * TM < n_ctx)
    def _():
        attend(n_ctx)

    @pl.when(j * TM >= n_ctx)
    def _():
        attend(k_ref.shape[1])


def diff_attention(qkv, lam_vecs, n_ctx, lam_init):
    bsz, lt, _ = qkv.shape
    nh = DIFF_HEADS // DIFF_HPS
    w = DIFF_HPS * LANE
    return pl.pallas_call(
        functools.partial(_diff_kernel, n_ctx=n_ctx, lam_init=lam_init),
        grid=(bsz, nh, lt // TM),
        in_specs=[pl.BlockSpec((4, DIFF_HD), lambda b, h, j: (0, 0)),
                  pl.BlockSpec((1, TM, w), lambda b, h, j: (b, j, h)),
                  pl.BlockSpec((1, lt, w), lambda b, h, j: (b, 0, nh + h)),
                  pl.BlockSpec((1, lt, w), lambda b, h, j: (b, 0, 2 * nh + h))],
        out_specs=pl.BlockSpec((1, TM, w), lambda b, h, j: (b, j, h)),
        out_shape=jax.ShapeDtypeStruct((bsz, lt, DIFF_HEADS * LANE), F32),
        compiler_params=_cp(("parallel", "parallel", "arbitrary")),
        name="diff_attention",
    )(lam_vecs, qkv, qkv, qkv)


def _conv_kernel(x_ref, p_ref, n_ref, w_ref, b_ref, o_ref, *, n_ctx_tiles, n_tiles):
    j = pl.program_id(1)
    first = jnp.logical_or(j == 0, j == n_ctx_tiles)
    last = jnp.logical_or(j == n_ctx_tiles - 1, j == n_tiles - 1)
    x = x_ref[0]
    prev = jnp.where(first, 0.0, p_ref[0])
    nxt = jnp.where(last, 0.0, n_ref[0])
    xe = jnp.concatenate([prev, x, nxt], axis=0)
    ne = xe.shape[0]
    half = SSD_CONV // 2
    y = b_ref[...] + w_ref[half:half + 1, :] * x
    for t in range(SSD_CONV):
        if t == half:
            continue
        sh = pltpu.roll(xe, (half - t) % ne, axis=0)[8:8 + TM]
        y = y + w_ref[t:t + 1, :] * sh
    o_ref[0] = _silu(y)


def ssd_conv(xbc, conv_w, conv_b, n_ctx):
    bsz, lt, ch = xbc.shape
    nt = lt // TM
    r8 = TM // 8
    return pl.pallas_call(
        functools.partial(_conv_kernel, n_ctx_tiles=n_ctx // TM, n_tiles=nt),
        grid=(bsz, nt),
        in_specs=[pl.BlockSpec((1, TM, ch), lambda b, j: (b, j, 0)),
                  pl.BlockSpec((1, 8, ch), lambda b, j: (b, jnp.maximum(j * r8 - 1, 0), 0)),
                  pl.BlockSpec((1, 8, ch), lambda b, j: (b, jnp.minimum((j + 1) * r8, lt // 8 - 1), 0)),
                  pl.BlockSpec((SSD_CONV, ch), lambda b, j: (0, 0)),
                  pl.BlockSpec((1, ch), lambda b, j: (0, 0))],
        out_specs=pl.BlockSpec((1, TM, ch), lambda b, j: (b, j, 0)),
        out_shape=jax.ShapeDtypeStruct((bsz, lt, ch), F32),
        compiler_params=_cp(("parallel", "parallel")),
        name="ssd_conv",
    )(xbc, xbc, xbc, conv_w, conv_b.reshape(1, ch))


def _ssd_kernel(xs_ref, bm_ref, cm_ref, dt_ref, dtb_ref, al_ref, ex_ref, y_ref, st_ref, *, rev):
    qn = SSD_CHUNK
    gw = SSD_R * SSD_P

    @pl.when(pl.program_id(1) == 0)
    def _():
        st_ref[...] = jnp.zeros_like(st_ref)

    dt = _softplus(dt_ref[0] + dtb_ref[...])
    da = dt * (-jnp.exp(al_ref[...]))
    ri = lax.broadcasted_iota(I32, (qn, qn), 0)
    ci = lax.broadcasted_iota(I32, (qn, qn), 1)
    causal = (ci >= ri) if rev else (ci <= ri)
    tri = jnp.where(causal, 1.0, 0.0).astype(BF16)
    acum = _dot_exact_rhs(tri, da)
    acum_t = acum.T
    dt_t = dt.T
    tot = acum[0:1, :] if rev else acum[qn - 1:qn, :]
    lane = lax.broadcasted_iota(I32, (qn, 2 * SSD_P), 1)
    e_acum = jnp.exp(acum)
    w_state = jnp.exp(tot - acum) * dt
    e_tot8 = jnp.broadcast_to(jnp.exp(tot), (8, LANE))

    for g in range(SSD_G):
        xs = xs_ref[0, :, g * gw:(g + 1) * gw]
        bm = bm_ref[0, :, g * SSD_N:(g + 1) * SSD_N]
        cmat = cm_ref[0, :, g * SSD_N:(g + 1) * SSD_N]
        ex = ex_ref[g]

        cb = _dot_nt(cmat.astype(BF16), bm.astype(BF16))
        xs16 = xs.astype(BF16)
        pieces = []
        for rp in range(SSD_R // 2):
            acc = None
            for sub in range(2):
                r = g * SSD_R + 2 * rp + sub
                seg = acum[:, r:r + 1] - acum_t[r:r + 1, :]
                w = cb * jnp.exp(jnp.minimum(seg, 0.0)) * dt_t[r:r + 1, :]
                w = jnp.where(causal, w, 0.0).astype(BF16)
                xpair = xs16[:, rp * 2 * SSD_P:(rp + 1) * 2 * SSD_P]
                keep = (lane < SSD_P) if sub == 0 else (lane >= SSD_P)
                part = _dot(w, jnp.where(keep, xpair, jnp.zeros_like(xpair)))
                acc = part if acc is None else acc + part
            pieces.append(acc)
        y = jnp.concatenate(pieces, axis=1)

        st = st_ref[g]
        e_i = _dot_01(e_acum, ex)
        y_ref[0, :, g * gw:(g + 1) * gw] = y + _dot(cmat.astype(BF16), st.astype(BF16)) * e_i
        wt = _dot_01(w_state, ex)
        e_tot = _dot_01(e_tot8, ex)[0:1, :]
        st_ref[g] = st * e_tot + _dot(bm.T.astype(BF16), (xs * wt).astype(BF16))


def ssd_scan(xc, dt, dtb, alog, expand, n_ctx, rev):
    bsz, lt, _ = xc.shape
    qn = SSD_CHUNK
    nch, ncc = lt // qn, n_ctx // qn
    d = 1 if rev else 0
    gw = SSD_R * SSD_P
    cm = functools.partial(_scan_chunk, n_ctx_chunks=ncc, n_chunks=nch, rev=rev)
    din, gn = SSD_HEADS * SSD_P, SSD_G * SSD_N
    return pl.pallas_call(
        functools.partial(_ssd_kernel, rev=rev),
        grid=(bsz, nch),
        in_specs=[pl.BlockSpec((1, qn, din), lambda b, i: (b, cm(i), 0)),
                  pl.BlockSpec((1, qn, gn), lambda b, i: (b, cm(i), din // gn)),
                  pl.BlockSpec((1, qn, gn), lambda b, i: (b, cm(i), din // gn + 1)),
                  pl.BlockSpec((1, qn, LANE), lambda b, i: (b, cm(i), d)),
                  pl.BlockSpec((1, LANE), lambda b, i: (0, d)),
                  pl.BlockSpec((1, LANE), lambda b, i: (0, d)),
                  pl.BlockSpec((SSD_G, LANE, gw), lambda b, i: (0, 0, 0))],
        out_specs=pl.BlockSpec((1, qn, din), lambda b, i: (b, cm(i), 0)),
        out_shape=jax.ShapeDtypeStruct((bsz, lt, din), F32),
        scratch_shapes=[pltpu.VMEM((SSD_G, SSD_N, gw), F32)],
        compiler_params=_cp(("parallel", "arbitrary")),
        name="ssd_scan_bwd" if rev else "ssd_scan_fwd",
    )(xc, xc, xc, dt, dtb, alog, expand)


def _top16(tasks, big):
    def body(r, carry):
        for s_ref, ids, vals_ref, idx_ref in tasks:
            s = s_ref[...]
            m = jnp.max(s, axis=0, keepdims=True)
            cand = jnp.where(s == m, ids, big)
            sel = jnp.min(cand, axis=0, keepdims=True)
            s_ref[...] = jnp.where(cand == sel, -jnp.inf, s)
            vals_ref[pl.ds(r, 1), :] = m
            idx_ref[pl.ds(r, 1), :] = sel
        return carry
    lax.fori_loop(0, PEER_TOPK, body, 0, unroll=2)


def _peer_route_kernel(f_ref, wh_ref, wl_ref, kh_ref, kl_ref, a_ref, b_ref, g_ref,
                       q_ref, s1_ref, s2_ref, c_ref, v1_ref, i1_ref, v2_ref, i2_ref, vb_ref, ib_ref):
    h = pl.program_id(1)
    nk = PEER_KEYS

    @pl.when(h == 0)
    def _():
        fh, fl = _split2(f_ref[...])
        q_ref[...] = _dot(fh, wh_ref[...]) + _dot(fh, wl_ref[...]) + _dot(fl, wh_ref[...])

    tm = s1_ref.shape[1]
    kid = lax.broadcasted_iota(I32, (nk, tm), 0)
    for z, s_ref in enumerate((s1_ref, s2_ref)):
        col = pl.multiple_of((h * 2 + z) * nk, nk)
        qh, ql = _split2(q_ref[:, pl.ds(col, nk)])
        kh, kl = kh_ref[z], kl_ref[z]
        s_ref[...] = _dot_nt(kh, qh) + _dot_nt(kh, ql) + _dot_nt(kl, qh)
    _top16([(s1_ref, kid, v1_ref, i1_ref), (s2_ref, kid, v2_ref, i2_ref)], nk)

    v1, v2 = v1_ref[...], v2_ref[...]
    parts = [v1[0:1] + v2]
    parts += [v1[p:p + 1] + v2[0:8] for p in range(1, 8)]
    parts += [v1[8:16] + v2[0:1]]
    c_ref[...] = jnp.concatenate(parts, axis=0)
    nc = c_ref.shape[0]
    r = lax.broadcasted_iota(I32, (nc, tm), 0)
    flat = jnp.where(r < 16, r, jnp.where(r < 72, (1 + ((r - 16) >> 3)) * 16 + ((r - 16) & 7), (r - 64) * 16))
    _top16([(c_ref, flat, vb_ref, ib_ref)], 1 << 20)

    sel = ib_ref[...]
    p, qq = sel >> 4, sel & (PEER_TOPK - 1)
    i1, i2 = i1_ref[...], i2_ref[...]
    a = jnp.zeros_like(sel)
    b = jnp.zeros_like(sel)
    for t in range(PEER_TOPK):
        a = jnp.where(p == t, i1[t:t + 1], a)
        b = jnp.where(qq == t, i2[t:t + 1], b)
    vb = vb_ref[...]
    e = jnp.exp(vb - vb[0:1])
    a_ref[0] = a
    b_ref[0] = b
    g_ref[0] = e / jnp.sum(e, axis=0, keepdims=True)


def peer_route(f2, wq_hi, wq_lo, sk_hi, sk_lo):
    t, d = f2.shape
    nq = wq_hi.shape[1]
    k = PEER_TOPK
    tm = ROUTE_TM
    outspec = pl.BlockSpec((1, k, tm), lambda i, h: (h, 0, i))
    return pl.pallas_call(
        _peer_route_kernel,
        grid=(t // tm, PEER_HEADS),
        in_specs=[pl.BlockSpec((tm, d), lambda i, h: (i, 0)),
                  pl.BlockSpec((d, nq), lambda i, h: (0, 0)),
                  pl.BlockSpec((d, nq), lambda i, h: (0, 0)),
                  pl.BlockSpec(sk_hi.shape, lambda i, h: (0, 0, 0)),
                  pl.BlockSpec(sk_lo.shape, lambda i, h: (0, 0, 0))],
        out_specs=[outspec, outspec, outspec],
        out_shape=[jax.ShapeDtypeStruct((PEER_HEADS, k, t), I32)] * 2 + [jax.ShapeDtypeStruct((PEER_HEADS, k, t), F32)],
        scratch_shapes=[pltpu.VMEM((tm, nq), F32), pltpu.VMEM((PEER_KEYS, tm), F32),
                        pltpu.VMEM((PEER_KEYS, tm), F32), pltpu.VMEM((80, tm), F32),
                        pltpu.VMEM((k, tm), F32), pltpu.VMEM((k, tm), I32),
                        pltpu.VMEM((k, tm), F32), pltpu.VMEM((k, tm), I32),
                        pltpu.VMEM((k, tm), F32), pltpu.VMEM((k, tm), I32)],
        compiler_params=_cp(("parallel", "arbitrary")),
        name="peer_route",
    )(f2, wq_hi, wq_lo, sk_hi, sk_lo)


def _peer_dense_kernel(f_ref, h_ref, gm_ref, a_ref, b_ref, g_ref, u_ref, v_ref, o_ref,
                       f16_ref, ar_ref, br_ref, gr_ref, gs_ref, acc_ref, *, tiles_per_batch, n_ctx_tiles):
    i, e = pl.program_id(0), pl.program_id(1)
    nk = PEER_KEYS
    half = nk // 2
    hi_mask = jnp.uint32(0xFFFF0000)

    @pl.when(e == 0)
    def _():
        f16_ref[...] = f_ref[...].astype(BF16)
        acc_ref[...] = jnp.zeros_like(acc_ref)
        ar_ref[...] = a_ref[...].T
        br_ref[...] = b_ref[...].T
        gr_ref[...] = g_ref[...].T
        r = lax.broadcasted_iota(I32, (nk, nk), 0)
        key1 = jnp.where(r < half, 2 * r, 2 * (r - half) + 1)
        key2 = r

        def per_token(t, carry):
            arow = ar_ref[pl.ds(t, 1), :]
            brow = br_ref[pl.ds(t, 1), :]
            grow = gr_ref[pl.ds(t, 1), :]
            ga = jnp.where(key1 == arow, grow, 0.0).astype(BF16)
            ob = jnp.where(key2 == brow, 1.0, 0.0).astype(BF16)
            gm = _dot_nt(ga, ob).astype(BF16).astype(F32)
            bits = lax.bitcast_convert_type(gm, jnp.uint32)
            off = pl.multiple_of(t * G_PITCH, 8)
            gs_ref[pl.ds(off, half), :] = bits[:half] | (bits[half:] >> 16)
            return carry
        lax.fori_loop(0, PEER_TM, per_token, 0, unroll=32)

    f16 = f16_ref[...]
    sub = 2 * nk
    ws = []
    for c in range(PEER_TE // sub):
        s = _dot_nt(f16, u_ref[c * sub:(c + 1) * sub, :])
        word = gs_ref[pl.ds(e * (PEER_TE // sub) + c, PEER_TM, stride=G_PITCH), :]
        gt = jnp.concatenate([lax.bitcast_convert_type(word & hi_mask, F32),
                              lax.bitcast_convert_type(word << 16, F32)], axis=1)
        act = 0.5 * s * (1.0 + lax.erf(s * (2.0 ** -0.5)))
        ws.append((act * gt).astype(BF16))
    acc_ref[...] += _dot(jnp.concatenate(ws, axis=1), v_ref[...])

    @pl.when(e == pl.num_programs(1) - 1)
    def _():
        for sub in range(PEER_TM // TM):
            j = i * (PEER_TM // TM) + sub
            row = _mod_row(j // tiles_per_batch, j % tiles_per_batch, n_ctx_tiles)
            sl = slice(sub * TM, (sub + 1) * TM)
            o_ref[sl, :] = h_ref[sl, :] + gm_ref[0, pl.ds(row, 1), :] * acc_ref[sl, :]


def peer_dense(f2, h2, mods, layer, a_t, b_t, g_t, u16, v16, tiles_per_batch, n_ctx):
    t, d = f2.shape
    ne = u16.shape[0]
    hk = PEER_HEADS * PEER_TOPK
    kern = functools.partial(_peer_dense_kernel, tiles_per_batch=tiles_per_batch, n_ctx_tiles=n_ctx // TM)
    tok = lambda: pl.BlockSpec((PEER_TM, d), lambda i, e: (i, 0))
    rt = lambda: pl.BlockSpec((hk, PEER_TM), lambda i, e: (0, i))
    return pl.pallas_call(
        kern,
        grid=(t // PEER_TM, ne // PEER_TE),
        in_specs=[tok(), tok(), pl.BlockSpec((1, 8, d), lambda i, e: (layer, 0, 5)), rt(), rt(), rt(),
                  pl.BlockSpec((PEER_TE, d), lambda i, e: (e, 0)),
                  pl.BlockSpec((PEER_TE, d), lambda i, e: (e, 0))],
        out_specs=tok(),
        out_shape=jax.ShapeDtypeStruct((t, d), F32),
        scratch_shapes=[pltpu.VMEM((PEER_TM, d), BF16), pltpu.VMEM((PEER_TM, hk), I32),
                        pltpu.VMEM((PEER_TM, hk), I32), pltpu.VMEM((PEER_TM, hk), F32),
                        pltpu.VMEM((PEER_TM * G_PITCH, PEER_KEYS), jnp.uint32), pltpu.VMEM((PEER_TM, d), F32)],
        compiler_params=_cp(("parallel", "arbitrary")),
        name="peer_dense",
    )(f2, h2, mods, a_t, b_t, g_t, u16, v16)


def peer_ffn(f, h, mods, layer, wq, subkeys, u, v, n_ctx):
    bsz, lt, d = h.shape
    t = bsz * lt
    f2, h2 = f.reshape(t, d), h.reshape(t, d)
    wq_hi, wq_lo = _split2(wq)
    sk_hi, sk_lo = _split2(subkeys)
    a, b, g = peer_route(f2, wq_hi, wq_lo, sk_hi, sk_lo)
    hk = PEER_HEADS * PEER_TOPK
    out = peer_dense(f2, h2, mods, layer, a.reshape(hk, t), b.reshape(hk, t), g.reshape(hk, t),
                     u.astype(BF16), v.astype(BF16), lt // TM, n_ctx)
    return out.reshape(bsz, lt, d)


def _final_kernel(h_ref, g_ref, o_ref):
    o_ref[0] = _rms(h_ref[0], g_ref[...])


def final_norm(h, g, n_ctx):
    bsz, lt, d = h.shape
    l = lt - n_ctx
    off = n_ctx // TM
    return pl.pallas_call(
        _final_kernel,
        grid=(bsz, l // TM),
        in_specs=[pl.BlockSpec((1, TM, d), lambda b, j: (b, j + off, 0)),
                  pl.BlockSpec((1, d), lambda b, j: (0, 0))],
        out_specs=pl.BlockSpec((1, TM, d), lambda b, j: (b, j, 0)),
        out_shape=jax.ShapeDtypeStruct((bsz, l, d), F32),
        compiler_params=_cp(("parallel", "parallel")),
        name="final_norm",
    )(h, g.reshape(1, d))


def _rope_tables(l, n_ctx):
    rows = l // GRID_W
    row = jnp.repeat(jnp.arange(rows), GRID_W).astype(F32)
    col = jnp.tile(jnp.arange(GRID_W), rows).astype(F32)
    n_freq = DIFF_HD // 4
    freqs = ROPE_THETA ** (-jnp.arange(n_freq, dtype=F32) / n_freq)
    ang = jnp.concatenate([row[:, None] * freqs, col[:, None] * freqs], axis=-1)
    cos, sin = jnp.cos(ang), jnp.sin(ang)
    cos = jnp.concatenate([jnp.ones((n_ctx, DIFF_HD // 2), F32), cos], axis=0)
    sin = jnp.concatenate([jnp.zeros((n_ctx, DIFF_HD // 2), F32), sin], axis=0)
    cos_t = jnp.concatenate([cos, cos, cos, cos], axis=1)
    sin_t = jnp.concatenate([-sin, sin, -sin, sin], axis=1)
    return cos_t, sin_t


def _gla_layer(h, mods, i, j, p, n_ctx):
    hk = GLA_HEADS * GLA_DK
    hv = GLA_HEADS * GLA_DV
    w_in = p["gla_w_in"][j]
    n_main = 2 * hk + 2 * hv
    w_lo = jnp.pad(w_in[:, n_main:], ((0, 0), (0, LANE - 2 * GLA_RANK))).astype(BF16)
    qkvg, lo = inproj(h, p["norm1_g"][i], mods, i, 0, [w_in[:, :n_main].astype(BF16), w_lo], n_ctx)
    wa = p["gla_w_alpha"][j]
    wa_pad = jnp.stack([jnp.pad(wa[0], ((0, LANE - GLA_RANK), (0, 0))),
                        jnp.pad(wa[1], ((GLA_RANK, LANE - 2 * GLA_RANK), (0, 0)))])
    ba = p["gla_b_alpha"][j].reshape(2, 1, hk)
    o_f = gla_scan(qkvg, lo, wa_pad, ba, n_ctx, False)
    o_b = gla_scan(qkvg, lo, wa_pad, ba, n_ctx, True)
    tile = lambda: pl.BlockSpec((1, TM, hv), lambda b, t: (b, t, 0))
    specs = [tile(), tile(), pl.BlockSpec((1, TM, hv), lambda b, t: (b, t, 2)),
             pl.BlockSpec((1, GLA_DV), lambda b, t: (0, 0))]
    args = [o_f, o_b, qkvg, p["gla_norm_g"][j].reshape(1, GLA_DV)]
    return finish("gla", args, specs, p["gla_w_out"][j].astype(BF16), h, p["norm2_g"][i], mods, i, n_ctx)


def _diff_layer(h, mods, i, j, p, n_ctx, rope_tabs):
    wd = DIFF_HEADS * 2 * DIFF_HD
    cos_t, sin_t = rope_tabs
    lt = h.shape[1]
    qkv, = inproj(h, p["norm1_g"][i], mods, i, 0, [p["diff_w_in"][j].astype(BF16)], n_ctx,
                  out_dtype=BF16, rope=(cos_t, sin_t, 2 * wd, wd, DIFF_HD ** -0.5))
    lam_init = 0.8 - 0.6 * math.exp(-0.3 * i)
    o = diff_attention(qkv, p["diff_lambda"][j], n_ctx, lam_init)
    specs = [pl.BlockSpec((1, TM, wd), lambda b, t: (b, t, 0)),
             pl.BlockSpec((1, 2 * DIFF_HD), lambda b, t: (0, 0))]
    args = [o, p["diff_norm_g"][j].reshape(1, 2 * DIFF_HD)]
    return finish("diff", args, specs, p["diff_w_out"][j].astype(BF16), h, p["norm2_g"][i], mods, i, n_ctx,
                  prm={"lam_init": lam_init})


def _ssd_layer(h, mods, i, j, p, n_ctx):
    din = SSD_HEADS * SSD_P
    gn = SSD_G * SSD_N
    w_in = p["ssd_w_in"][j]
    g1 = p["norm1_g"][i]
    w_dt = w_in[:, 2 * din + 2 * gn:].reshape(-1, 2, SSD_HEADS)
    w_dt = jnp.pad(w_dt, ((0, 0), (0, 0), (0, LANE - SSD_HEADS))).reshape(-1, 2 * LANE)
    z, xbc, dt = inproj(h, g1, mods, i, 0, [w_in[:, :din].astype(BF16),
                                              w_in[:, din:2 * din + 2 * gn].astype(BF16), w_dt.astype(BF16)], n_ctx)
    pad_h = lambda a: jnp.pad(a, ((0, 0), (0, LANE - SSD_HEADS))).reshape(1, -1)
    dtb, alog = pad_h(p["ssd_dt_bias"][j]), pad_h(p["ssd_a_log"][j])
    xc = ssd_conv(xbc, p["ssd_conv_w"][j], p["ssd_conv_b"][j], n_ctx)
    head_of_col = jnp.arange(SSD_G)[:, None, None] * SSD_R + jnp.arange(SSD_R * SSD_P)[None, None, :] // SSD_P
    expand = (jnp.arange(LANE)[None, :, None] == head_of_col).astype(BF16)
    y_f = ssd_scan(xc, dt, dtb, alog, expand, n_ctx, False)
    y_b = ssd_scan(xc, dt, dtb, alog, expand, n_ctx, True)
    tile = lambda: pl.BlockSpec((1, TM, din), lambda b, t: (b, t, 0))
    row = lambda: pl.BlockSpec((1, din), lambda b, t: (0, 0))
    specs = [tile(), tile(), tile(), tile(), row(), row()]
    dexp = jnp.repeat(p["ssd_d"][j], SSD_P).reshape(1, din)
    args = [y_f, y_b, xc, z, dexp, p["ssd_norm_g"][j].reshape(1, din)]
    return finish("ssd", args, specs, p["ssd_w_out"][j].astype(BF16), h, p["norm2_g"][i], mods, i, n_ctx)


def kernel(x, c, ctx, c_ctx, norm1_g, norm2_g, w_mod, b_mod, peer_wq, peer_subkeys, peer_u, peer_v, gla_w_in, gla_w_alpha, gla_b_alpha, gla_norm_g, gla_w_out, diff_w_in, diff_lambda, diff_norm_g, diff_w_out, ssd_w_in, ssd_conv_w, ssd_conv_b, ssd_dt_bias, ssd_a_log, ssd_d, ssd_norm_g, ssd_w_out, final_g):
    p = dict(norm1_g=norm1_g, norm2_g=norm2_g, gla_w_in=gla_w_in, gla_w_alpha=gla_w_alpha, gla_b_alpha=gla_b_alpha,
             gla_norm_g=gla_norm_g, gla_w_out=gla_w_out, diff_w_in=diff_w_in, diff_lambda=diff_lambda,
             diff_norm_g=diff_norm_g, diff_w_out=diff_w_out, ssd_w_in=ssd_w_in, ssd_conv_w=ssd_conv_w,
             ssd_conv_b=ssd_conv_b, ssd_dt_bias=ssd_dt_bias, ssd_a_log=ssd_a_log, ssd_d=ssd_d,
             ssd_norm_g=ssd_norm_g, ssd_w_out=ssd_w_out)
    bsz, l, d = x.shape
    n_ctx = ctx.shape[1]
    depth = w_mod.shape[0]
    assert bsz <= 4 and n_ctx % TM == 0 and l % TM == 0 and l % GRID_W == 0
    cond8 = jnp.concatenate([c, jnp.zeros((4 - bsz, d), F32), c_ctx[None], jnp.zeros((3, d), F32)], axis=0)
    mods = mod_table(cond8, w_mod, b_mod)
    rope_tabs = _rope_tables(l, n_ctx)
    h = jnp.concatenate([ctx, x], axis=1)
    for i in range(depth):
        kind, j = i % N_MIXERS, i // N_MIXERS
        if kind == 0:
            h, f = _gla_layer(h, mods, i, j, p, n_ctx)
        elif kind == 1:
            h, f = _diff_layer(h, mods, i, j, p, n_ctx, rope_tabs)
        else:
            h, f = _ssd_layer(h, mods, i, j, p, n_ctx)
        h = peer_ffn(f, h, mods, i, peer_wq[i], peer_subkeys[i], peer_u[i], peer_v[i], n_ctx)
    return final_norm(h, final_g, n_ctx)
```

```python
import functools
import math

import jax
import jax.numpy as jnp
from jax import lax
from jax.experimental import pallas as pl
from jax.experimental.pallas import tpu as pltpu

F32 = jnp.float32
BF16 = jnp.bfloat16
I32 = jnp.int32

EPS = 1e-6
GRID_W = 64
ROPE_THETA = 10000.0
N_MIXERS = 3

GLA_HEADS, GLA_DK, GLA_DV, GLA_RANK, GLA_TAU = 4, 128, 256, 16, 16.0
GLA_CHUNK = 128
DIFF_HEADS, DIFF_HD = 8, 64
DIFF_HPS = 2
SSD_HEADS, SSD_P, SSD_N, SSD_G, SSD_CONV, SSD_CHUNK = 32, 64, 128, 4, 5, 128
SSD_R = SSD_HEADS // SSD_G
PEER_KEYS, PEER_HEADS, PEER_TOPK = 128, 8, 16

TM = 256
LANE = 128
ROUTE_TM = 1024
PEER_TM = 512
PEER_TE = 1024
G_PITCH = 72
VMEM_LIMIT = 56 * 1024 * 1024


def _cp(sem, vmem=VMEM_LIMIT):
    return pltpu.CompilerParams(dimension_semantics=sem, vmem_limit_bytes=vmem)


def _dot(a, b):
    return jnp.dot(a, b, preferred_element_type=F32)


def _dot_nt(a, b):
    return lax.dot_general(a, b, (((1,), (1,)), ((), ())), preferred_element_type=F32)


def _split2(x):
    hi = x.astype(BF16)
    lo = (x - hi.astype(F32)).astype(BF16)
    return hi, lo


def _dot3(a, b, dot=_dot):
    ah, al = _split2(a)
    bh, bl = _split2(b)
    return dot(ah, bh) + dot(ah, bl) + dot(al, bh)


def _dot_exact_rhs(w01, x):
    h1 = x.astype(BF16)
    r1 = x - h1.astype(F32)
    h2 = r1.astype(BF16)
    h3 = (r1 - h2.astype(F32)).astype(BF16)
    return _dot(w01, h1) + _dot(w01, h2) + _dot(w01, h3)


def _dot_01(x, w01):
    xh, xl = _split2(x)
    return _dot(xh, w01) + _dot(xl, w01)


def _silu(x):
    return x * (1.0 / (1.0 + jnp.exp(-x)))


def _softplus(x):
    return jnp.maximum(x, 0.0) + jnp.log1p(jnp.exp(-jnp.abs(x)))


def _rms(x, g):
    return x * lax.rsqrt(jnp.mean(x * x, axis=-1, keepdims=True) + EPS) * g


def _mod_row(b, j, n_ctx_tiles):
    return jnp.where(j < n_ctx_tiles, 4, b)


def _mod_kernel(cond_ref, w_ref, b_ref, o_ref):
    c = _silu(cond_ref[...])
    o_ref[0] = _dot3(c, w_ref[0]) + b_ref[0]


def mod_table(cond8, w_mod, b_mod):
    depth, d, n = w_mod.shape
    tn = 1024
    return pl.pallas_call(
        _mod_kernel,
        grid=(depth, n // tn),
        in_specs=[pl.BlockSpec((8, d), lambda i, j: (0, 0)),
                  pl.BlockSpec((1, d, tn), lambda i, j: (i, 0, j)),
                  pl.BlockSpec((1, 1, tn), lambda i, j: (i, 0, j))],
        out_specs=pl.BlockSpec((1, 8, tn), lambda i, j: (i, 0, j)),
        out_shape=jax.ShapeDtypeStruct((depth, 8, n), F32),
        compiler_params=_cp(("parallel", "parallel")),
        name="mod_table",
    )(cond8, w_mod, b_mod.reshape(depth, 1, n))


INPROJ_COLS = 1024


def _inproj_kernel(h_ref, g_ref, sh_ref, sc_ref, *rest, n_w, n_ctx_tiles, rope_cols, q_cols, q_scale):
    w_refs, rest = rest[:n_w], rest[n_w:]
    if rope_cols:
        cos_ref, sin_ref = rest[:2]
        rest = rest[2:]
    o_refs = rest
    b, j = pl.program_id(0), pl.program_id(1)
    row = _mod_row(b, j, n_ctx_tiles)
    a = _rms(h_ref[0], g_ref[...]) * (1.0 + sc_ref[0, pl.ds(row, 1), :]) + sh_ref[0, pl.ds(row, 1), :]
    a = a.astype(BF16)
    for k, (w_ref, o_ref) in enumerate(zip(w_refs, o_refs)):
        n = w_ref.shape[1]
        for c0 in range(0, n, INPROJ_COLS):
            c1 = min(n, c0 + INPROJ_COLS)
            y = _dot(a, w_ref[:, c0:c1])
            if k == 0 and c0 < rope_cols:
                tn = c1 - c0
                lane = lax.broadcasted_iota(I32, y.shape, 1)
                first = (lane & (DIFF_HD - 1)) < (DIFF_HD // 2)
                part = jnp.where(first, pltpu.roll(y, tn - DIFF_HD // 2, axis=1),
                                 pltpu.roll(y, DIFF_HD // 2, axis=1))
                cs = jnp.concatenate([cos_ref[0]] * (tn // LANE), axis=1)
                sn = jnp.concatenate([sin_ref[0]] * (tn // LANE), axis=1)
                y = y * cs + part * sn
                if c0 < q_cols:
                    y = y * q_scale
            o_ref[0, :, c0:c1] = y.astype(o_ref.dtype)


def inproj(h, g, mods, layer, k_shift, ws, n_ctx, out_dtype=F32, rope=None):
    bsz, lt, d = h.shape
    nt = lt // TM
    kern = functools.partial(_inproj_kernel, n_w=len(ws), n_ctx_tiles=n_ctx // TM, rope_cols=rope[2] if rope else 0,
                             q_cols=rope[3] if rope else 0, q_scale=rope[4] if rope else 1.0)
    in_specs = [pl.BlockSpec((1, TM, d), lambda b, j: (b, j, 0)),
                pl.BlockSpec((1, d), lambda b, j: (0, 0)),
                pl.BlockSpec((1, 8, d), lambda b, j: (layer, 0, k_shift)),
                pl.BlockSpec((1, 8, d), lambda b, j: (layer, 0, k_shift + 1))]
    in_specs += [pl.BlockSpec(w.shape, lambda b, j: (0, 0)) for w in ws]
    args = [h, g.reshape(1, d), mods, mods, *ws]
    if rope:
        in_specs += [pl.BlockSpec((1, TM, LANE), lambda b, j: (0, j, 0))] * 2
        args += [rope[0][None], rope[1][None]]
    return pl.pallas_call(
        kern,
        grid=(bsz, nt),
        in_specs=in_specs,
        out_specs=[pl.BlockSpec((1, TM, w.shape[1]), lambda b, j: (b, j, 0)) for w in ws],
        out_shape=[jax.ShapeDtypeStruct((bsz, lt, w.shape[1]), out_dtype) for w in ws],
        compiler_params=_cp(("parallel", "parallel")),
        name="inproj",
    )(*args)


def _post_gla(refs, prm):
    o_f, o_b, gate, ng = refs
    o = o_f[0] + o_b[0]
    g = gate[0]
    outs = []
    for hd in range(GLA_HEADS):
        sl = slice(hd * GLA_DV, (hd + 1) * GLA_DV)
        outs.append(_rms(o[:, sl], ng[...]) * _silu(g[:, sl]))
    return jnp.concatenate(outs, axis=1)


def _post_diff(refs, prm):
    o, ng = refs
    x = o[0]
    outs = []
    for hd in range(DIFF_HEADS):
        sl = slice(hd * 2 * DIFF_HD, (hd + 1) * 2 * DIFF_HD)
        outs.append(_rms(x[:, sl], ng[...]) * (1.0 - prm["lam_init"]))
    return jnp.concatenate(outs, axis=1)


def _post_ssd(refs, prm):
    y_f, y_b, xs, z, dexp, ng = refs
    y = (y_f[0] + y_b[0] + dexp[...] * xs[0]) * _silu(z[0])
    gs = y.shape[1] // SSD_G
    outs = []
    for gi in range(SSD_G):
        sl = slice(gi * gs, (gi + 1) * gs)
        outs.append(_rms(y[:, sl], ng[:, sl]))
    return jnp.concatenate(outs, axis=1)


_POST = {"gla": (_post_gla, 4), "diff": (_post_diff, 2), "ssd": (_post_ssd, 6)}


def _finish_kernel(*refs, kind, prm, n_ctx_tiles):
    post, n_in = _POST[kind]
    mix = refs[:n_in]
    w_ref, h_ref, gm_ref, g2_ref, sh_ref, sc_ref, hn_ref, f_ref = refs[n_in:]
    b, j = pl.program_id(0), pl.program_id(1)
    row = _mod_row(b, j, n_ctx_tiles)
    y = post(mix, prm)
    o = _dot(y.astype(BF16), w_ref[...])
    hn = h_ref[0] + gm_ref[0, pl.ds(row, 1), :] * o
    hn_ref[0] = hn
    f_ref[0] = _rms(hn, g2_ref[...]) * (1.0 + sc_ref[0, pl.ds(row, 1), :]) + sh_ref[0, pl.ds(row, 1), :]


def finish(kind, mix_args, mix_specs, w_out, h, g2, mods, layer, n_ctx, prm=None):
    bsz, lt, d = h.shape
    nt = lt // TM
    dm = w_out.shape[0]
    tile = lambda: pl.BlockSpec((1, TM, d), lambda b, j: (b, j, 0))
    modspec = lambda k: pl.BlockSpec((1, 8, d), lambda b, j: (layer, 0, k))
    kern = functools.partial(_finish_kernel, kind=kind, prm=prm or {}, n_ctx_tiles=n_ctx // TM)
    return pl.pallas_call(
        kern,
        grid=(bsz, nt),
        in_specs=list(mix_specs) + [pl.BlockSpec((dm, d), lambda b, j: (0, 0)), tile(), modspec(2),
                                    pl.BlockSpec((1, d), lambda b, j: (0, 0)), modspec(3), modspec(4)],
        out_specs=[tile(), tile()],
        out_shape=[jax.ShapeDtypeStruct((bsz, lt, d), F32)] * 2,
        compiler_params=_cp(("parallel", "parallel")),
        name="finish_" + kind,
    )(*mix_args, w_out, h, mods, g2.reshape(1, d), mods, mods)


def _gla_kernel(q_ref, k_ref, v_ref, lo_ref, wa_ref, ba_ref, o_ref, st_ref, *, rev):
    c = GLA_CHUNK
    hk = GLA_HEADS * GLA_DK
    heads = [slice(hd * GLA_DK, (hd + 1) * GLA_DK) for hd in range(GLA_HEADS)]

    @pl.when(pl.program_id(1) == 0)
    def _():
        st_ref[...] = jnp.zeros_like(st_ref)

    q = q_ref[0] * (GLA_DK ** -0.5)
    k = k_ref[0]
    v = v_ref[0]
    z = _dot3(lo_ref[0], wa_ref[0]) + ba_ref[0]
    la = (jnp.minimum(z, 0.0) - jnp.log1p(jnp.exp(-jnp.abs(z)))) * (1.0 / GLA_TAU)

    ri = lax.broadcasted_iota(I32, (c, c), 0)
    ci = lax.broadcasted_iota(I32, (c, c), 1)
    row = lax.broadcasted_iota(I32, (c, hk), 0)
    tri = jnp.where((ci >= ri) if rev else (ci <= ri), 1.0, 0.0).astype(BF16)
    bsum = _dot_exact_rhs(tri, la)
    tot = bsum[0:1, :] if rev else bsum[c - 1:c, :]

    q16, k16 = q.astype(BF16), k.astype(BF16)
    eye = ci == ri
    a = [jnp.where(eye, _dot_nt(q16[:, sl], k16[:, sl]), 0.0) for sl in heads]
    s = c // 2
    while s >= 1:
        pos = row & (2 * s - 1)
        q_half = (pos < s) if rev else (pos >= s)
        if s >= 4:
            blk = bsum.reshape(c // (2 * s), 2 * s, hk)
            rr = s if rev else s - 1
            ref = jnp.broadcast_to(blk[:, rr:rr + 1, :], blk.shape).reshape(c, hk)
            e = jnp.where(q_half, bsum - ref, ref - bsum)
        elif s == 2:
            pos4 = row & 3
            nxt, prv = pltpu.roll(la, c - 1, axis=0), pltpu.roll(la, 1, axis=0)
            if rev:
                e = jnp.where(pos4 == 0, la + nxt, jnp.where(pos4 == 1, la, jnp.where(pos4 == 2, 0.0, prv)))
            else:
                e = jnp.where(pos4 == 3, la + prv, jnp.where(pos4 == 2, la, jnp.where(pos4 == 1, 0.0, nxt)))
        else:
            e = jnp.where(q_half, la, 0.0)
        f = jnp.exp(jnp.minimum(e, 0.0))
        qd = jnp.where(q_half, q * f, 0.0).astype(BF16)
        kd = jnp.where(q_half, 0.0, k * f).astype(BF16)
        same_block = (ri & -(2 * s)) == (ci & -(2 * s))
        a = [a[hd] + jnp.where(same_block, _dot_nt(qd[:, sl], kd[:, sl]), 0.0) for hd, sl in enumerate(heads)]
        s //= 2

    qe = (q * jnp.exp(bsum)).astype(BF16)
    kt = (k * jnp.exp(tot - bsum)).astype(BF16)
    et = jnp.exp(tot)
    for hd, sl in enumerate(heads):
        vh = v[:, hd * GLA_DV:(hd + 1) * GLA_DV]
        st = st_ref[hd]
        o_ref[0, :, hd * GLA_DV:(hd + 1) * GLA_DV] = (_dot(a[hd].astype(BF16), vh.astype(BF16))
                                                       + _dot_nt(qe[:, sl], st.astype(BF16)))
        st_ref[hd] = st * et[:, sl] + _dot(vh.T.astype(BF16), kt[:, sl])


def _scan_chunk(c, n_ctx_chunks, n_chunks, rev):
    if not rev:
        return c
    return jnp.where(c < n_ctx_chunks, n_ctx_chunks - 1 - c, n_chunks - 1 - (c - n_ctx_chunks))


def gla_scan(qkvg, lo, wa_pad, ba, n_ctx, rev):
    bsz, lt, _ = qkvg.shape
    c = GLA_CHUNK
    nch, ncc = lt // c, n_ctx // c
    hk, hv = GLA_HEADS * GLA_DK, GLA_HEADS * GLA_DV
    d = 1 if rev else 0
    cm = functools.partial(_scan_chunk, n_ctx_chunks=ncc, n_chunks=nch, rev=rev)
    return pl.pallas_call(
        functools.partial(_gla_kernel, rev=rev),
        grid=(bsz, nch),
        in_specs=[pl.BlockSpec((1, c, hk), lambda b, i: (b, cm(i), 0)),
                  pl.BlockSpec((1, c, hk), lambda b, i: (b, cm(i), 1)),
                  pl.BlockSpec((1, c, hv), lambda b, i: (b, cm(i), 1)),
                  pl.BlockSpec((1, c, LANE), lambda b, i: (b, cm(i), 0)),
                  pl.BlockSpec((1, LANE, hk), lambda b, i: (d, 0, 0)),
                  pl.BlockSpec((1, 1, hk), lambda b, i: (d, 0, 0))],
        out_specs=pl.BlockSpec((1, c, hv), lambda b, i: (b, cm(i), 0)),
        out_shape=jax.ShapeDtypeStruct((bsz, lt, hv), F32),
        scratch_shapes=[pltpu.VMEM((GLA_HEADS, GLA_DV, GLA_DK), F32)],
        compiler_params=_cp(("parallel", "arbitrary")),
        name="gla_scan_bwd" if rev else "gla_scan_fwd",
    )(qkvg, qkvg, qkvg, lo, wa_pad, ba)


def _diff_kernel(lam_ref, q_ref, k_ref, v_ref, o_ref, *, n_ctx, lam_init):
    j = pl.program_id(2)
    lv = lam_ref[...]
    l1 = jnp.sum(lv[0:1] * lv[1:2], axis=1, keepdims=True)
    l2 = jnp.sum(lv[2:3] * lv[3:4], axis=1, keepdims=True)
    lam = jnp.exp(l1) - jnp.exp(l2) + lam_init

    lane = lax.broadcasted_iota(I32, (TM, LANE), 1)

    def attend(n_keys):
        for hh in range(DIFF_HPS):
            cols = slice(hh * LANE, (hh + 1) * LANE)
            q = q_ref[0, :, cols]
            zero = jnp.zeros_like(q)
            q0 = jnp.where(lane < DIFF_HD, q, zero)
            q1 = jnp.where(lane < DIFF_HD, zero, q)
            k = k_ref[0, :n_keys, cols]
            v = v_ref[0, :n_keys, cols]
            s0 = _dot_nt(q0, k)
            s1 = _dot_nt(q1, k)
            p0 = jnp.exp2(s0 - jnp.max(s0, axis=1, keepdims=True))
            p1 = jnp.exp2(s1 - jnp.max(s1, axis=1, keepdims=True))
            r0 = 1.0 / jnp.sum(p0, axis=1, keepdims=True)
            r1 = lam / jnp.sum(p1, axis=1, keepdims=True)
            o_ref[0, :, cols] = _dot(p0.astype(BF16), v) * r0 - _dot(p1.astype(BF16), v) * r1

    @pl.when(j * TM < n_ctx)
    def _():
        attend(n_ctx)

    @pl.when(j * TM >= n_ctx)
    def _():
        attend(k_ref.shape[1])


def diff_attention(qkv, lam_vecs, n_ctx, lam_init):
    bsz, lt, _ = qkv.shape
    nh = DIFF_HEADS // DIFF_HPS
    w = DIFF_HPS * LANE
    return pl.pallas_call(
        functools.partial(_diff_kernel, n_ctx=n_ctx, lam_init=lam_init),
        grid=(bsz, nh, lt // TM),
        in_specs=[pl.BlockSpec((4, DIFF_HD), lambda b, h, j: (0, 0)),
                  pl.BlockSpec((1, TM, w), lambda b, h, j: (b, j, h)),
                  pl.BlockSpec((1, lt, w), lambda b, h, j: (b, 0, nh + h)),
                  pl.BlockSpec((1, lt, w), lambda b, h, j: (b, 0, 2 * nh + h))],
        out_specs=pl.BlockSpec((1, TM, w), lambda b, h, j: (b, j, h)),
        out_shape=jax.ShapeDtypeStruct((bsz, lt, DIFF_HEADS * LANE), F32),
        compiler_params=_cp(("parallel", "parallel", "arbitrary")),
        name="diff_attention",
    )(lam_vecs, qkv, qkv, qkv)


def _conv_kernel(x_ref, p_ref, n_ref, w_ref, b_ref, o_ref, *, n_ctx_tiles, n_tiles):
    j = pl.program_id(1)
    first = jnp.logical_or(j == 0, j == n_ctx_tiles)
    last = jnp.logical_or(j == n_ctx_tiles - 1, j == n_tiles - 1)
    x = x_ref[0]
    prev = jnp.where(first, 0.0, p_ref[0])
    nxt = jnp.where(last, 0.0, n_ref[0])
    xe = jnp.concatenate([prev, x, nxt], axis=0)
    ne = xe.shape[0]
    half = SSD_CONV // 2
    y = b_ref[...] + w_ref[half:half + 1, :] * x
    for t in range(SSD_CONV):
        if t == half:
            continue
        sh = pltpu.roll(xe, (half - t) % ne, axis=0)[8:8 + TM]
        y = y + w_ref[t:t + 1, :] * sh
    o_ref[0] = _silu(y)


def ssd_conv(xbc, conv_w, conv_b, n_ctx):
    bsz, lt, ch = xbc.shape
    nt = lt // TM
    r8 = TM // 8
    return pl.pallas_call(
        functools.partial(_conv_kernel, n_ctx_tiles=n_ctx // TM, n_tiles=nt),
        grid=(bsz, nt),
        in_specs=[pl.BlockSpec((1, TM, ch), lambda b, j: (b, j, 0)),
                  pl.BlockSpec((1, 8, ch), lambda b, j: (b, jnp.maximum(j * r8 - 1, 0), 0)),
                  pl.BlockSpec((1, 8, ch), lambda b, j: (b, jnp.minimum((j + 1) * r8, lt // 8 - 1), 0)),
                  pl.BlockSpec((SSD_CONV, ch), lambda b, j: (0, 0)),
                  pl.BlockSpec((1, ch), lambda b, j: (0, 0))],
        out_specs=pl.BlockSpec((1, TM, ch), lambda b, j: (b, j, 0)),
        out_shape=jax.ShapeDtypeStruct((bsz, lt, ch), F32),
        compiler_params=_cp(("parallel", "parallel")),
        name="ssd_conv",
    )(xbc, xbc, xbc, conv_w, conv_b.reshape(1, ch))


def _ssd_kernel(xs_ref, bm_ref, cm_ref, dt_ref, dtb_ref, al_ref, ex_ref, y_ref, st_ref, *, rev):
    qn = SSD_CHUNK
    gw = SSD_R * SSD_P

    @pl.when(pl.program_id(1) == 0)
    def _():
        st_ref[...] = jnp.zeros_like(st_ref)

    dt = _softplus(dt_ref[0] + dtb_ref[...])
    da = dt * (-jnp.exp(al_ref[...]))
    ri = lax.broadcasted_iota(I32, (qn, qn), 0)
    ci = lax.broadcasted_iota(I32, (qn, qn), 1)
    causal = (ci >= ri) if rev else (ci <= ri)
    tri = jnp.where(causal, 1.0, 0.0).astype(BF16)
    acum = _dot_exact_rhs(tri, da)
    acum_t = acum.T
    dt_t = dt.T
    tot = acum[0:1, :] if rev else acum[qn - 1:qn, :]
    lane = lax.broadcasted_iota(I32, (qn, 2 * SSD_P), 1)
    e_acum = jnp.exp(acum)
    w_state = jnp.exp(tot - acum) * dt
    e_tot8 = jnp.broadcast_to(jnp.exp(tot), (8, LANE))

    for g in range(SSD_G):
        xs = xs_ref[0, :, g * gw:(g + 1) * gw]
        bm = bm_ref[0, :, g * SSD_N:(g + 1) * SSD_N]
        cmat = cm_ref[0, :, g * SSD_N:(g + 1) * SSD_N]
        ex = ex_ref[g]

        cb = _dot_nt(cmat.astype(BF16), bm.astype(BF16))
        xs16 = xs.astype(BF16)
        pieces = []
        for rp in range(SSD_R // 2):
            acc = None
            for sub in range(2):
                r = g * SSD_R + 2 * rp + sub
                seg = acum[:, r:r + 1] - acum_t[r:r + 1, :]
                w = cb * jnp.exp(jnp.minimum(seg, 0.0)) * dt_t[r:r + 1, :]
                w = jnp.where(causal, w, 0.0).astype(BF16)
                xpair = xs16[:, rp * 2 * SSD_P:(rp + 1) * 2 * SSD_P]
                keep = (lane < SSD_P) if sub == 0 else (lane >= SSD_P)
                part = _dot(w, jnp.where(keep, xpair, jnp.zeros_like(xpair)))
                acc = part if acc is None else acc + part
            pieces.append(acc)
        y = jnp.concatenate(pieces, axis=1)

        st = st_ref[g]
        e_i = _dot_01(e_acum, ex)
        y_ref[0, :, g * gw:(g + 1) * gw] = y + _dot(cmat.astype(BF16), st.astype(BF16)) * e_i
        wt = _dot_01(w_state, ex)
        e_tot = _dot_01(e_tot8, ex)[0:1, :]
        st_ref[g] = st * e_tot + _dot(bm.T.astype(BF16), (xs * wt).astype(BF16))


def ssd_scan(xc, dt, dtb, alog, expand, n_ctx, rev):
    bsz, lt, _ = xc.shape
    qn = SSD_CHUNK
    nch, ncc = lt // qn, n_ctx // qn
    d = 1 if rev else 0
    gw = SSD_R * SSD_P
    cm = functools.partial(_scan_chunk, n_ctx_chunks=ncc, n_chunks=nch, rev=rev)
    din, gn = SSD_HEADS * SSD_P, SSD_G * SSD_N
    return pl.pallas_call(
        functools.partial(_ssd_kernel, rev=rev),
        grid=(bsz, nch),
        in_specs=[pl.BlockSpec((1, qn, din), lambda b, i: (b, cm(i), 0)),
                  pl.BlockSpec((1, qn, gn), lambda b, i: (b, cm(i), din // gn)),
                  pl.BlockSpec((1, qn, gn), lambda b, i: (b, cm(i), din // gn + 1)),
                  pl.BlockSpec((1, qn, LANE), lambda b, i: (b, cm(i), d)),
                  pl.BlockSpec((1, LANE), lambda b, i: (0, d)),
                  pl.BlockSpec((1, LANE), lambda b, i: (0, d)),
                  pl.BlockSpec((SSD_G, LANE, gw), lambda b, i: (0, 0, 0))],
        out_specs=pl.BlockSpec((1, qn, din), lambda b, i: (b, cm(i), 0)),
        out_shape=jax.ShapeDtypeStruct((bsz, lt, din), F32),
        scratch_shapes=[pltpu.VMEM((SSD_G, SSD_N, gw), F32)],
        compiler_params=_cp(("parallel", "arbitrary")),
        name="ssd_scan_bwd" if rev else "ssd_scan_fwd",
    )(xc, xc, xc, dt, dtb, alog, expand)


SUBL = 8


def _vrow(ref, k):
    return ref[SUBL * k:SUBL * (k + 1), :]


def _top16(tasks):
    def body(r, carry):
        out = pl.ds(pl.multiple_of(r * SUBL, SUBL), SUBL)
        for s_ref, ids, vals_ref, idx_ref in tasks:
            rows = [_vrow(s_ref, k) for k in range(len(ids))]
            level = list(zip(rows, ids))
            while len(level) > 1:
                nxt = []
                for j in range(0, len(level) - 1, 2):
                    (va, ia), (vb, ib) = level[j], level[j + 1]
                    gt = vb > va
                    nxt.append((jnp.where(gt, vb, va), jnp.where(gt, ib, ia)))
                if len(level) % 2:
                    nxt.append(level[-1])
                level = nxt
            m, sel = level[0]
            for k, v in enumerate(rows):
                s_ref[SUBL * k:SUBL * (k + 1), :] = jnp.where(sel == ids[k], -jnp.inf, v)
            vals_ref[out, :] = m
            idx_ref[out, :] = sel
        return carry
    lax.fori_loop(0, PEER_TOPK, body, 0, unroll=2)


ROUTE_SUB = ROUTE_TM // LANE
assert ROUTE_SUB == SUBL


def _peer_route_kernel(f_ref, wh_ref, wl_ref, kh_ref, kl_ref, a_ref, b_ref, g_ref,
                       q_ref, s1_ref, s2_ref, c_ref, v1_ref, i1_ref, v2_ref, i2_ref, vb_ref, ib_ref):
    h = pl.program_id(1)
    nk = PEER_KEYS
    rows_q = 256

    @pl.when(h == 0)
    def _():
        for rc in range(ROUTE_TM // rows_q):
            rs = slice(rc * rows_q, (rc + 1) * rows_q)
            fh, fl = _split2(f_ref[rs, :])
            q_ref[rs, :] = _dot(fh, wh_ref[...]) + _dot(fh, wl_ref[...]) + _dot(fl, wh_ref[...])

    for z, s_ref in enumerate((s1_ref, s2_ref)):
        col = pl.multiple_of((h * 2 + z) * nk, nk)
        kh, kl = kh_ref[z], kl_ref[z]
        for c in range(ROUTE_SUB):
            qh, ql = _split2(q_ref[c * LANE:(c + 1) * LANE, pl.ds(col, nk)])
            st = _dot_nt(kh, qh) + _dot_nt(kh, ql) + _dot_nt(kl, qh)
            s_ref[pl.ds(c, nk, stride=ROUTE_SUB), :] = st
    keys = list(range(nk))
    _top16([(s1_ref, keys, v1_ref, i1_ref), (s2_ref, keys, v2_ref, i2_ref)])

    pairs = ([(0, q) for q in range(PEER_TOPK)] + [(p, q) for p in range(1, 8) for q in range(8)]
             + [(p, 0) for p in range(8, PEER_TOPK)])
    for j, (p, q) in enumerate(pairs):
        c_ref[SUBL * j:SUBL * (j + 1), :] = _vrow(v1_ref, p) + _vrow(v2_ref, q)
    _top16([(c_ref, [p * PEER_TOPK + q for p, q in pairs], vb_ref, ib_ref)])

    fold3 = (PEER_TOPK, ROUTE_SUB, LANE)
    sel = ib_ref[...].reshape(fold3)
    p, qq = sel >> 4, sel & (PEER_TOPK - 1)
    i1, i2 = i1_ref[...].reshape(fold3), i2_ref[...].reshape(fold3)
    a = jnp.zeros_like(sel)
    b = jnp.zeros_like(sel)
    for t in range(PEER_TOPK):
        a = jnp.where(p == t, i1[t:t + 1], a)
        b = jnp.where(qq == t, i2[t:t + 1], b)
    vb = vb_ref[...].reshape(fold3)
    e = jnp.exp(vb - vb[0:1])
    a_ref[0] = a
    b_ref[0] = b
    g_ref[0] = e / jnp.sum(e, axis=0, keepdims=True)


def peer_route(f2, wq_hi, wq_lo, sk_hi, sk_lo):
    t, d = f2.shape
    nq = wq_hi.shape[1]
    k = PEER_TOPK
    tm = ROUTE_TM
    fold = (ROUTE_SUB, LANE)
    outspec = pl.BlockSpec((1, k) + fold, lambda i, h: (h, 0, i, 0))
    oshape = (PEER_HEADS, k, t // LANE, LANE)
    a, b, g = pl.pallas_call(
        _peer_route_kernel,
        grid=(t // tm, PEER_HEADS),
        in_specs=[pl.BlockSpec((tm, d), lambda i, h: (i, 0)),
                  pl.BlockSpec((d, nq), lambda i, h: (0, 0)),
                  pl.BlockSpec((d, nq), lambda i, h: (0, 0)),
                  pl.BlockSpec(sk_hi.shape, lambda i, h: (0, 0, 0)),
                  pl.BlockSpec(sk_lo.shape, lambda i, h: (0, 0, 0))],
        out_specs=[outspec, outspec, outspec],
        out_shape=[jax.ShapeDtypeStruct(oshape, I32)] * 2 + [jax.ShapeDtypeStruct(oshape, F32)],
        scratch_shapes=[pltpu.VMEM((tm, nq), F32), pltpu.VMEM((PEER_KEYS * SUBL, LANE), F32),
                        pltpu.VMEM((PEER_KEYS * SUBL, LANE), F32), pltpu.VMEM((80 * SUBL, LANE), F32),
                        pltpu.VMEM((k * SUBL, LANE), F32), pltpu.VMEM((k * SUBL, LANE), I32),
                        pltpu.VMEM((k * SUBL, LANE), F32), pltpu.VMEM((k * SUBL, LANE), I32),
                        pltpu.VMEM((k * SUBL, LANE), F32), pltpu.VMEM((k * SUBL, LANE), I32)],
        compiler_params=_cp(("parallel", "arbitrary")),
        name="peer_route",
    )(f2, wq_hi, wq_lo, sk_hi, sk_lo)
    return a.reshape(PEER_HEADS, k, t), b.reshape(PEER_HEADS, k, t), g.reshape(PEER_HEADS, k, t)


def _peer_dense_kernel(f_ref, h_ref, gm_ref, a_ref, b_ref, g_ref, u_ref, v_ref, o_ref,
                       f16_ref, ar_ref, br_ref, gr_ref, gs_ref, acc_ref, *, tiles_per_batch, n_ctx_tiles):
    i, e = pl.program_id(0), pl.program_id(1)
    nk = PEER_KEYS
    half = nk // 2
    hi_mask = jnp.uint32(0xFFFF0000)

    @pl.when(e == 0)
    def _():
        f16_ref[...] = f_ref[...].astype(BF16)
        acc_ref[...] = jnp.zeros_like(acc_ref)
        ar_ref[...] = a_ref[...].T
        br_ref[...] = b_ref[...].T
        gr_ref[...] = g_ref[...].T
        r = lax.broadcasted_iota(I32, (nk, nk), 0)
        key1 = jnp.where(r < half, 2 * r, 2 * (r - half) + 1)
        key2 = r

        def per_token(t, carry):
            arow = ar_ref[pl.ds(t, 1), :]
            brow = br_ref[pl.ds(t, 1), :]
            grow = 0.5 * gr_ref[pl.ds(t, 1), :]
            ga = jnp.where(key1 == arow, grow, 0.0).astype(BF16)
            ob = jnp.where(key2 == brow, 1.0, 0.0).astype(BF16)
            gm = _dot_nt(ga, ob).astype(BF16).astype(F32)
            bits = lax.bitcast_convert_type(gm, jnp.uint32)
            off = pl.multiple_of(t * G_PITCH, 8)
            gs_ref[pl.ds(off, half), :] = bits[:half] | (bits[half:] >> 16)
            return carry
        lax.fori_loop(0, PEER_TM, per_token, 0, unroll=32)

    f16 = f16_ref[...]
    sub = 2 * nk
    ws = []
    for c in range(PEER_TE // sub):
        s = _dot_nt(f16, u_ref[c * sub:(c + 1) * sub, :])
        word = gs_ref[pl.ds(e * (PEER_TE // sub) + c, PEER_TM, stride=G_PITCH), :]
        gt = jnp.concatenate([lax.bitcast_convert_type(word & hi_mask, F32),
                              lax.bitcast_convert_type(word << 16, F32)], axis=1)
        act = s * (1.0 + lax.erf(s * (2.0 ** -0.5)))
        ws.append((act * gt).astype(BF16))
    acc_ref[...] += _dot(jnp.concatenate(ws, axis=1), v_ref[...])

    @pl.when(e == pl.num_programs(1) - 1)
    def _():
        for part in range(PEER_TM // TM):
            j = i * (PEER_TM // TM) + part
            row = _mod_row(j // tiles_per_batch, j % tiles_per_batch, n_ctx_tiles)
            sl = slice(part * TM, (part + 1) * TM)
            o_ref[sl, :] = h_ref[sl, :] + gm_ref[0, pl.ds(row, 1), :] * acc_ref[sl, :]


def peer_dense(f2, h2, mods, layer, a_t, b_t, g_t, u16, v16, tiles_per_batch, n_ctx):
    t, d = f2.shape
    ne = u16.shape[0]
    hk = PEER_HEADS * PEER_TOPK
    kern = functools.partial(_peer_dense_kernel, tiles_per_batch=tiles_per_batch, n_ctx_tiles=n_ctx // TM)
    tok = lambda: pl.BlockSpec((PEER_TM, d), lambda i, e: (i, 0))
    rt = lambda: pl.BlockSpec((hk, PEER_TM), lambda i, e: (0, i))
    return pl.pallas_call(
        kern,
        grid=(t // PEER_TM, ne // PEER_TE),
        in_specs=[tok(), tok(), pl.BlockSpec((1, 8, d), lambda i, e: (layer, 0, 5)), rt(), rt(), rt(),
                  pl.BlockSpec((PEER_TE, d), lambda i, e: (e, 0)),
                  pl.BlockSpec((PEER_TE, d), lambda i, e: (e, 0))],
        out_specs=tok(),
        out_shape=jax.ShapeDtypeStruct((t, d), F32),
        scratch_shapes=[pltpu.VMEM((PEER_TM, d), BF16), pltpu.VMEM((PEER_TM, hk), I32),
                        pltpu.VMEM((PEER_TM, hk), I32), pltpu.VMEM((PEER_TM, hk), F32),
                        pltpu.VMEM((PEER_TM * G_PITCH, PEER_KEYS), jnp.uint32), pltpu.VMEM((PEER_TM, d), F32)],
        compiler_params=_cp(("parallel", "arbitrary")),
        name="peer_dense",
    )(f2, h2, mods, a_t, b_t, g_t, u16, v16)


def peer_ffn(f, h, mods, layer, wq, subkeys, u, v, n_ctx):
    bsz, lt, d = h.shape
    t = bsz * lt
    f2, h2 = f.reshape(t, d), h.reshape(t, d)
    wq_hi, wq_lo = _split2(wq)
    sk_hi, sk_lo = _split2(subkeys)
    a, b, g = peer_route(f2, wq_hi, wq_lo, sk_hi, sk_lo)
    hk = PEER_HEADS * PEER_TOPK
    out = peer_dense(f2, h2, mods, layer, a.reshape(hk, t), b.reshape(hk, t), g.reshape(hk, t),
                     u.astype(BF16), v.astype(BF16), lt // TM, n_ctx)
    return out.reshape(bsz, lt, d)


def _final_kernel(h_ref, g_ref, o_ref):
    o_ref[0] = _rms(h_ref[0], g_ref[...])


def final_norm(h, g, n_ctx):
    bsz, lt, d = h.shape
    l = lt - n_ctx
    off = n_ctx // TM
    return pl.pallas_call(
        _final_kernel,
        grid=(bsz, l // TM),
        in_specs=[pl.BlockSpec((1, TM, d), lambda b, j: (b, j + off, 0)),
                  pl.BlockSpec((1, d), lambda b, j: (0, 0))],
        out_specs=pl.BlockSpec((1, TM, d), lambda b, j: (b, j, 0)),
        out_shape=jax.ShapeDtypeStruct((bsz, l, d), F32),
        compiler_params=_cp(("parallel", "parallel")),
        name="final_norm",
    )(h, g.reshape(1, d))


def _rope_tables(l, n_ctx):
    rows = l // GRID_W
    row = jnp.repeat(jnp.arange(rows), GRID_W).astype(F32)
    col = jnp.tile(jnp.arange(GRID_W), rows).astype(F32)
    n_freq = DIFF_HD // 4
    freqs = ROPE_THETA ** (-jnp.arange(n_freq, dtype=F32) / n_freq)
    ang = jnp.concatenate([row[:, None] * freqs, col[:, None] * freqs], axis=-1)
    cos, sin = jnp.cos(ang), jnp.sin(ang)
    cos = jnp.concatenate([jnp.ones((n_ctx, DIFF_HD // 2), F32), cos], axis=0)
    sin = jnp.concatenate([jnp.zeros((n_ctx, DIFF_HD // 2), F32), sin], axis=0)
    cos_t = jnp.concatenate([cos, cos, cos, cos], axis=1)
    sin_t = jnp.concatenate([-sin, sin, -sin, sin], axis=1)
    return cos_t, sin_t


def _gla_layer(h, mods, i, j, p, n_ctx):
    hk = GLA_HEADS * GLA_DK
    hv = GLA_HEADS * GLA_DV
    w_in = p["gla_w_in"][j]
    n_main = 2 * hk + 2 * hv
    w_lo = jnp.pad(w_in[:, n_main:], ((0, 0), (0, LANE - 2 * GLA_RANK))).astype(BF16)
    qkvg, lo = inproj(h, p["norm1_g"][i], mods, i, 0, [w_in[:, :n_main].astype(BF16), w_lo], n_ctx)
    wa = p["gla_w_alpha"][j]
    wa_pad = jnp.stack([jnp.pad(wa[0], ((0, LANE - GLA_RANK), (0, 0))),
                        jnp.pad(wa[1], ((GLA_RANK, LANE - 2 * GLA_RANK), (0, 0)))])
    ba = p["gla_b_alpha"][j].reshape(2, 1, hk)
    o_f = gla_scan(qkvg, lo, wa_pad, ba, n_ctx, False)
    o_b = gla_scan(qkvg, lo, wa_pad, ba, n_ctx, True)
    tile = lambda: pl.BlockSpec((1, TM, hv), lambda b, t: (b, t, 0))
    specs = [tile(), tile(), pl.BlockSpec((1, TM, hv), lambda b, t: (b, t, 2)),
             pl.BlockSpec((1, GLA_DV), lambda b, t: (0, 0))]
    args = [o_f, o_b, qkvg, p["gla_norm_g"][j].reshape(1, GLA_DV)]
    return finish("gla", args, specs, p["gla_w_out"][j].astype(BF16), h, p["norm2_g"][i], mods, i, n_ctx)


def _diff_layer(h, mods, i, j, p, n_ctx, rope_tabs):
    wd = DIFF_HEADS * 2 * DIFF_HD
    cos_t, sin_t = rope_tabs
    lt = h.shape[1]
    qkv, = inproj(h, p["norm1_g"][i], mods, i, 0, [p["diff_w_in"][j].astype(BF16)], n_ctx,
                  out_dtype=BF16, rope=(cos_t, sin_t, 2 * wd, wd, DIFF_HD ** -0.5 * math.log2(math.e)))
    lam_init = 0.8 - 0.6 * math.exp(-0.3 * i)
    o = diff_attention(qkv, p["diff_lambda"][j], n_ctx, lam_init)
    specs = [pl.BlockSpec((1, TM, wd), lambda b, t: (b, t, 0)),
             pl.BlockSpec((1, 2 * DIFF_HD), lambda b, t: (0, 0))]
    args = [o, p["diff_norm_g"][j].reshape(1, 2 * DIFF_HD)]
    return finish("diff", args, specs, p["diff_w_out"][j].astype(BF16), h, p["norm2_g"][i], mods, i, n_ctx,
                  prm={"lam_init": lam_init})


def _ssd_layer(h, mods, i, j, p, n_ctx):
    din = SSD_HEADS * SSD_P
    gn = SSD_G * SSD_N
    w_in = p["ssd_w_in"][j]
    g1 = p["norm1_g"][i]
    w_dt = w_in[:, 2 * din + 2 * gn:].reshape(-1, 2, SSD_HEADS)
    w_dt = jnp.pad(w_dt, ((0, 0), (0, 0), (0, LANE - SSD_HEADS))).reshape(-1, 2 * LANE)
    z, xbc, dt = inproj(h, g1, mods, i, 0, [w_in[:, :din].astype(BF16),
                                              w_in[:, din:2 * din + 2 * gn].astype(BF16), w_dt.astype(BF16)], n_ctx)
    pad_h = lambda a: jnp.pad(a, ((0, 0), (0, LANE - SSD_HEADS))).reshape(1, -1)
    dtb, alog = pad_h(p["ssd_dt_bias"][j]), pad_h(p["ssd_a_log"][j])
    xc = ssd_conv(xbc, p["ssd_conv_w"][j], p["ssd_conv_b"][j], n_ctx)
    head_of_col = jnp.arange(SSD_G)[:, None, None] * SSD_R + jnp.arange(SSD_R * SSD_P)[None, None, :] // SSD_P
    expand = (jnp.arange(LANE)[None, :, None] == head_of_col).astype(BF16)
    y_f = ssd_scan(xc, dt, dtb, alog, expand, n_ctx, False)
    y_b = ssd_scan(xc, dt, dtb, alog, expand, n_ctx, True)
    tile = lambda: pl.BlockSpec((1, TM, din), lambda b, t: (b, t, 0))
    row = lambda: pl.BlockSpec((1, din), lambda b, t: (0, 0))
    specs = [tile(), tile(), tile(), tile(), row(), row()]
    dexp = jnp.repeat(p["ssd_d"][j], SSD_P).reshape(1, din)
    args = [y_f, y_b, xc, z, dexp, p["ssd_norm_g"][j].reshape(1, din)]
    return finish("ssd", args, specs, p["ssd_w_out"][j].astype(BF16), h, p["norm2_g"][i], mods, i, n_ctx)


def kernel(x, c, ctx, c_ctx, norm1_g, norm2_g, w_mod, b_mod, peer_wq, peer_subkeys, peer_u, peer_v, gla_w_in, gla_w_alpha, gla_b_alpha, gla_norm_g, gla_w_out, diff_w_in, diff_lambda, diff_norm_g, diff_w_out, ssd_w_in, ssd_conv_w, ssd_conv_b, ssd_dt_bias, ssd_a_log, ssd_d, ssd_norm_g, ssd_w_out, final_g):
    p = dict(norm1_g=norm1_g, norm2_g=norm2_g, gla_w_in=gla_w_in, gla_w_alpha=gla_w_alpha, gla_b_alpha=gla_b_alpha,
             gla_norm_g=gla_norm_g, gla_w_out=gla_w_out, diff_w_in=diff_w_in, diff_lambda=diff_lambda,
             diff_norm_g=diff_norm_g, diff_w_out=diff_w_out, ssd_w_in=ssd_w_in, ssd_conv_w=ssd_conv_w,
             ssd_conv_b=ssd_conv_b, ssd_dt_bias=ssd_dt_bias, ssd_a_log=ssd_a_log, ssd_d=ssd_d,
             ssd_norm_g=ssd_norm_g, ssd_w_out=ssd_w_out)
    bsz, l, d = x.shape
    n_ctx = ctx.shape[1]
    depth = w_mod.shape[0]
    assert bsz <= 4 and n_ctx % TM == 0 and l % TM == 0 and l % GRID_W == 0
    cond8 = jnp.concatenate([c, jnp.zeros((4 - bsz, d), F32), c_ctx[None], jnp.zeros((3, d), F32)], axis=0)
    mods = mod_table(cond8, w_mod, b_mod)
    rope_tabs = _rope_tables(l, n_ctx)
    h = jnp.concatenate([ctx, x], axis=1)
    for i in range(depth):
        kind, j = i % N_MIXERS, i // N_MIXERS
        if kind == 0:
            h, f = _gla_layer(h, mods, i, j, p, n_ctx)
        elif kind == 1:
            h, f = _diff_layer(h, mods, i, j, p, n_ctx, rope_tabs)
        else:
            h, f = _ssd_layer(h, mods, i, j, p, n_ctx)
        h = peer_ffn(f, h, mods, i, peer_wq[i], peer_subkeys[i], peer_u[i], peer_v[i], n_ctx)
    return final_norm(h, final_g, n_ctx)
```

```python
import functools
import math

import jax
import jax.numpy as jnp
from jax import lax
from jax.experimental import pallas as pl
from jax.experimental.pallas import tpu as pltpu

F32 = jnp.float32
BF16 = jnp.bfloat16
I32 = jnp.int32

EPS = 1e-6
GRID_W = 64
ROPE_THETA = 10000.0
N_MIXERS = 3

GLA_HEADS, GLA_DK, GLA_DV, GLA_RANK, GLA_TAU = 4, 128, 256, 16, 16.0
GLA_CHUNK = 128
DIFF_HEADS, DIFF_HD = 8, 64
DIFF_HPS = 2
SSD_HEADS, SSD_P, SSD_N, SSD_G, SSD_CONV, SSD_CHUNK = 32, 64, 128, 4, 5, 128
SSD_R = SSD_HEADS // SSD_G
PEER_KEYS, PEER_HEADS, PEER_TOPK = 128, 8, 16

TM = 256
LANE = 128
ROUTE_TM = 1024
PEER_TM = 512
PEER_TE = 2048
G_PITCH = 72
VMEM_LIMIT = 56 * 1024 * 1024


def _cp(sem, vmem=VMEM_LIMIT):
    return pltpu.CompilerParams(dimension_semantics=sem, vmem_limit_bytes=vmem)


def _dot(a, b):
    return jnp.dot(a, b, preferred_element_type=F32)


def _dot_nt(a, b):
    return lax.dot_general(a, b, (((1,), (1,)), ((), ())), preferred_element_type=F32)


def _split2(x):
    hi = x.astype(BF16)
    lo = (x - hi.astype(F32)).astype(BF16)
    return hi, lo


def _dot3(a, b, dot=_dot):
    ah, al = _split2(a)
    bh, bl = _split2(b)
    return dot(ah, bh) + dot(ah, bl) + dot(al, bh)


def _dot_exact_rhs(w01, x):
    h1 = x.astype(BF16)
    r1 = x - h1.astype(F32)
    h2 = r1.astype(BF16)
    h3 = (r1 - h2.astype(F32)).astype(BF16)
    return _dot(w01, h1) + _dot(w01, h2) + _dot(w01, h3)


def _dot_01(x, w01):
    xh, xl = _split2(x)
    return _dot(xh, w01) + _dot(xl, w01)


def _silu(x):
    return x * (1.0 / (1.0 + jnp.exp(-x)))


def _softplus(x):
    return jnp.maximum(x, 0.0) + jnp.log1p(jnp.exp(-jnp.abs(x)))


def _rms(x, g):
    return x * lax.rsqrt(jnp.mean(x * x, axis=-1, keepdims=True) + EPS) * g


def _mod_row(b, j, n_ctx_tiles):
    return jnp.where(j < n_ctx_tiles, 4, b)


def _mod_kernel(cond_ref, w_ref, b_ref, o_ref):
    c = _silu(cond_ref[...])
    o_ref[0] = _dot3(c, w_ref[0]) + b_ref[0]


def mod_table(cond8, w_mod, b_mod):
    depth, d, n = w_mod.shape
    tn = 1024
    return pl.pallas_call(
        _mod_kernel,
        grid=(depth, n // tn),
        in_specs=[pl.BlockSpec((8, d), lambda i, j: (0, 0)),
                  pl.BlockSpec((1, d, tn), lambda i, j: (i, 0, j)),
                  pl.BlockSpec((1, 1, tn), lambda i, j: (i, 0, j))],
        out_specs=pl.BlockSpec((1, 8, tn), lambda i, j: (i, 0, j)),
        out_shape=jax.ShapeDtypeStruct((depth, 8, n), F32),
        compiler_params=_cp(("parallel", "parallel")),
        name="mod_table",
    )(cond8, w_mod, b_mod.reshape(depth, 1, n))


INPROJ_COLS = 1024


def _inproj_kernel(h_ref, g_ref, sh_ref, sc_ref, *rest, n_w, n_ctx_tiles, rope_cols, q_cols, q_scale):
    w_refs, rest = rest[:n_w], rest[n_w:]
    if rope_cols:
        cos_ref, sin_ref = rest[:2]
        rest = rest[2:]
    o_refs = rest
    b, j = pl.program_id(0), pl.program_id(1)
    row = _mod_row(b, j, n_ctx_tiles)
    a = _rms(h_ref[0], g_ref[...]) * (1.0 + sc_ref[0, pl.ds(row, 1), :]) + sh_ref[0, pl.ds(row, 1), :]
    a = a.astype(BF16)
    for k, (w_ref, o_ref) in enumerate(zip(w_refs, o_refs)):
        n = w_ref.shape[1]
        for c0 in range(0, n, INPROJ_COLS):
            c1 = min(n, c0 + INPROJ_COLS)
            y = _dot(a, w_ref[:, c0:c1])
            if k == 0 and c0 < rope_cols:
                tn = c1 - c0
                lane = lax.broadcasted_iota(I32, y.shape, 1)
                first = (lane & (DIFF_HD - 1)) < (DIFF_HD // 2)
                part = jnp.where(first, pltpu.roll(y, tn - DIFF_HD // 2, axis=1),
                                 pltpu.roll(y, DIFF_HD // 2, axis=1))
                cs = jnp.concatenate([cos_ref[0]] * (tn // LANE), axis=1)
                sn = jnp.concatenate([sin_ref[0]] * (tn // LANE), axis=1)
                y = y * cs + part * sn
                if c0 < q_cols:
                    y = y * q_scale
            o_ref[0, :, c0:c1] = y.astype(o_ref.dtype)


def inproj(h, g, mods, layer, k_shift, ws, n_ctx, out_dtype=F32, rope=None):
    bsz, lt, d = h.shape
    nt = lt // TM
    kern = functools.partial(_inproj_kernel, n_w=len(ws), n_ctx_tiles=n_ctx // TM, rope_cols=rope[2] if rope else 0,
                             q_cols=rope[3] if rope else 0, q_scale=rope[4] if rope else 1.0)
    in_specs = [pl.BlockSpec((1, TM, d), lambda b, j: (b, j, 0)),
                pl.BlockSpec((1, d), lambda b, j: (0, 0)),
                pl.BlockSpec((1, 8, d), lambda b, j: (layer, 0, k_shift)),
                pl.BlockSpec((1, 8, d), lambda b, j: (layer, 0, k_shift + 1))]
    in_specs += [pl.BlockSpec(w.shape, lambda b, j: (0, 0)) for w in ws]
    args = [h, g.reshape(1, d), mods, mods, *ws]
    if rope:
        in_specs += [pl.BlockSpec((1, TM, LANE), lambda b, j: (0, j, 0))] * 2
        args += [rope[0][None], rope[1][None]]
    return pl.pallas_call(
        kern,
        grid=(bsz, nt),
        in_specs=in_specs,
        out_specs=[pl.BlockSpec((1, TM, w.shape[1]), lambda b, j: (b, j, 0)) for w in ws],
        out_shape=[jax.ShapeDtypeStruct((bsz, lt, w.shape[1]), out_dtype) for w in ws],
        compiler_params=_cp(("parallel", "parallel")),
        name="inproj",
    )(*args)


def _post_gla(refs, prm):
    o_f, o_b, gate, ng = refs
    o = o_f[0] + o_b[0]
    g = gate[0]
    outs = []
    for hd in range(GLA_HEADS):
        sl = slice(hd * GLA_DV, (hd + 1) * GLA_DV)
        outs.append(_rms(o[:, sl], ng[...]) * _silu(g[:, sl]))
    return jnp.concatenate(outs, axis=1)


def _post_diff(refs, prm):
    o, ng = refs
    x = o[0]
    outs = []
    for hd in range(DIFF_HEADS):
        sl = slice(hd * 2 * DIFF_HD, (hd + 1) * 2 * DIFF_HD)
        outs.append(_rms(x[:, sl], ng[...]) * (1.0 - prm["lam_init"]))
    return jnp.concatenate(outs, axis=1)


def _post_ssd(refs, prm):
    y_f, y_b, xs, z, dexp, ng = refs
    y = (y_f[0] + y_b[0] + dexp[...] * xs[0]) * _silu(z[0])
    gs = y.shape[1] // SSD_G
    outs = []
    for gi in range(SSD_G):
        sl = slice(gi * gs, (gi + 1) * gs)
        outs.append(_rms(y[:, sl], ng[:, sl]))
    return jnp.concatenate(outs, axis=1)


_POST = {"gla": (_post_gla, 4), "diff": (_post_diff, 2), "ssd": (_post_ssd, 6)}


def _finish_kernel(*refs, kind, prm, n_ctx_tiles):
    post, n_in = _POST[kind]
    mix = refs[:n_in]
    w_ref, h_ref, gm_ref, g2_ref, sh_ref, sc_ref, hn_ref, f_ref = refs[n_in:]
    b, j = pl.program_id(0), pl.program_id(1)
    row = _mod_row(b, j, n_ctx_tiles)
    y = post(mix, prm)
    o = _dot(y.astype(BF16), w_ref[...])
    hn = h_ref[0] + gm_ref[0, pl.ds(row, 1), :] * o
    hn_ref[0] = hn
    f_ref[0] = _rms(hn, g2_ref[...]) * (1.0 + sc_ref[0, pl.ds(row, 1), :]) + sh_ref[0, pl.ds(row, 1), :]


def finish(kind, mix_args, mix_specs, w_out, h, g2, mods, layer, n_ctx, prm=None):
    bsz, lt, d = h.shape
    nt = lt // TM
    dm = w_out.shape[0]
    tile = lambda: pl.BlockSpec((1, TM, d), lambda b, j: (b, j, 0))
    modspec = lambda k: pl.BlockSpec((1, 8, d), lambda b, j: (layer, 0, k))
    kern = functools.partial(_finish_kernel, kind=kind, prm=prm or {}, n_ctx_tiles=n_ctx // TM)
    return pl.pallas_call(
        kern,
        grid=(bsz, nt),
        in_specs=list(mix_specs) + [pl.BlockSpec((dm, d), lambda b, j: (0, 0)), tile(), modspec(2),
                                    pl.BlockSpec((1, d), lambda b, j: (0, 0)), modspec(3), modspec(4)],
        out_specs=[tile(), tile()],
        out_shape=[jax.ShapeDtypeStruct((bsz, lt, d), F32)] * 2,
        compiler_params=_cp(("parallel", "parallel")),
        name="finish_" + kind,
    )(*mix_args, w_out, h, mods, g2.reshape(1, d), mods, mods)


def _gla_kernel(q_ref, k_ref, v_ref, lo_ref, wa_ref, ba_ref, o_ref, st_ref, *, rev):
    c = GLA_CHUNK
    hk = GLA_HEADS * GLA_DK
    heads = [slice(hd * GLA_DK, (hd + 1) * GLA_DK) for hd in range(GLA_HEADS)]

    @pl.when(pl.program_id(1) == 0)
    def _():
        st_ref[...] = jnp.zeros_like(st_ref)

    q = q_ref[0] * (GLA_DK ** -0.5)
    k = k_ref[0]
    v = v_ref[0]
    z = _dot3(lo_ref[0], wa_ref[0]) + ba_ref[0]
    la = (jnp.minimum(z, 0.0) - jnp.log1p(jnp.exp(-jnp.abs(z)))) * (1.0 / GLA_TAU)

    ri = lax.broadcasted_iota(I32, (c, c), 0)
    ci = lax.broadcasted_iota(I32, (c, c), 1)
    row = lax.broadcasted_iota(I32, (c, hk), 0)
    tri = jnp.where((ci >= ri) if rev else (ci <= ri), 1.0, 0.0).astype(BF16)
    bsum = _dot_exact_rhs(tri, la)
    tot = bsum[0:1, :] if rev else bsum[c - 1:c, :]

    q16, k16 = q.astype(BF16), k.astype(BF16)
    eye = ci == ri
    a = [jnp.where(eye, _dot_nt(q16[:, sl], k16[:, sl]), 0.0) for sl in heads]
    s = c // 2
    while s >= 1:
        pos = row & (2 * s - 1)
        q_half = (pos < s) if rev else (pos >= s)
        if s >= 4:
            blk = bsum.reshape(c // (2 * s), 2 * s, hk)
            rr = s if rev else s - 1
            ref = jnp.broadcast_to(blk[:, rr:rr + 1, :], blk.shape).reshape(c, hk)
            e = jnp.where(q_half, bsum - ref, ref - bsum)
        elif s == 2:
            pos4 = row & 3
            nxt, prv = pltpu.roll(la, c - 1, axis=0), pltpu.roll(la, 1, axis=0)
            if rev:
                e = jnp.where(pos4 == 0, la + nxt, jnp.where(pos4 == 1, la, jnp.where(pos4 == 2, 0.0, prv)))
            else:
                e = jnp.where(pos4 == 3, la + prv, jnp.where(pos4 == 2, la, jnp.where(pos4 == 1, 0.0, nxt)))
        else:
            e = jnp.where(q_half, la, 0.0)
        f = jnp.exp(e)
        qd = jnp.where(q_half, q * f, 0.0).astype(BF16)
        kd = jnp.where(q_half, 0.0, k * f).astype(BF16)
        same_block = (ri & -(2 * s)) == (ci & -(2 * s))
        a = [a[hd] + jnp.where(same_block, _dot_nt(qd[:, sl], kd[:, sl]), 0.0) for hd, sl in enumerate(heads)]
        s //= 2

    qe = (q * jnp.exp(bsum)).astype(BF16)
    kt = (k * jnp.exp(tot - bsum)).astype(BF16)
    et = jnp.exp(tot)
    for hd, sl in enumerate(heads):
        vh = v[:, hd * GLA_DV:(hd + 1) * GLA_DV]
        st = st_ref[hd]
        o_ref[0, :, hd * GLA_DV:(hd + 1) * GLA_DV] = (_dot(a[hd].astype(BF16), vh.astype(BF16))
                                                       + _dot_nt(qe[:, sl], st.astype(BF16)))
        st_ref[hd] = st * et[:, sl] + _dot(vh.T.astype(BF16), kt[:, sl])


def _scan_chunk(c, n_ctx_chunks, n_chunks, rev):
    if not rev:
        return c
    return jnp.where(c < n_ctx_chunks, n_ctx_chunks - 1 - c, n_chunks - 1 - (c - n_ctx_chunks))


def gla_scan(qkvg, lo, wa_pad, ba, n_ctx, rev):
    bsz, lt, _ = qkvg.shape
    c = GLA_CHUNK
    nch, ncc = lt // c, n_ctx // c
    hk, hv = GLA_HEADS * GLA_DK, GLA_HEADS * GLA_DV
    d = 1 if rev else 0
    cm = functools.partial(_scan_chunk, n_ctx_chunks=ncc, n_chunks=nch, rev=rev)
    return pl.pallas_call(
        functools.partial(_gla_kernel, rev=rev),
        grid=(bsz, nch),
        in_specs=[pl.BlockSpec((1, c, hk), lambda b, i: (b, cm(i), 0)),
                  pl.BlockSpec((1, c, hk), lambda b, i: (b, cm(i), 1)),
                  pl.BlockSpec((1, c, hv), lambda b, i: (b, cm(i), 1)),
                  pl.BlockSpec((1, c, LANE), lambda b, i: (b, cm(i), 0)),
                  pl.BlockSpec((1, LANE, hk), lambda b, i: (d, 0, 0)),
                  pl.BlockSpec((1, 1, hk), lambda b, i: (d, 0, 0))],
        out_specs=pl.BlockSpec((1, c, hv), lambda b, i: (b, cm(i), 0)),
        out_shape=jax.ShapeDtypeStruct((bsz, lt, hv), F32),
        scratch_shapes=[pltpu.VMEM((GLA_HEADS, GLA_DV, GLA_DK), F32)],
        compiler_params=_cp(("parallel", "arbitrary")),
        name="gla_scan_bwd" if rev else "gla_scan_fwd",
    )(qkvg, qkvg, qkvg, lo, wa_pad, ba)


def _diff_kernel(lam_ref, q_ref, k_ref, v_ref, o_ref, *, n_ctx, lam_init):
    j = pl.program_id(2)
    lv = lam_ref[...]
    l1 = jnp.sum(lv[0:1] * lv[1:2], axis=1, keepdims=True)
    l2 = jnp.sum(lv[2:3] * lv[3:4], axis=1, keepdims=True)
    lam = jnp.exp(l1) - jnp.exp(l2) + lam_init

    lane = lax.broadcasted_iota(I32, (TM, LANE), 1)

    def attend(n_keys):
        for hh in range(DIFF_HPS):
            cols = slice(hh * LANE, (hh + 1) * LANE)
            q = q_ref[0, :, cols]
            zero = jnp.zeros_like(q)
            q0 = jnp.where(lane < DIFF_HD, q, zero)
            q1 = jnp.where(lane < DIFF_HD, zero, q)
            k = k_ref[0, :n_keys, cols]
            v = v_ref[0, :n_keys, cols]
            s0 = _dot_nt(q0, k)
            s1 = _dot_nt(q1, k)
            p0 = jnp.exp2(s0 - jnp.max(s0, axis=1, keepdims=True))
            p1 = jnp.exp2(s1 - jnp.max(s1, axis=1, keepdims=True))
            r0 = 1.0 / jnp.sum(p0, axis=1, keepdims=True)
            r1 = lam / jnp.sum(p1, axis=1, keepdims=True)
            o_ref[0, :, cols] = _dot(p0.astype(BF16), v) * r0 - _dot(p1.astype(BF16), v) * r1

    @pl.when(j * TM < n_ctx)
    def _():
        attend(n_ctx)

    @pl.when(j * TM >= n_ctx)
    def _():
        attend(k_ref.shape[1])


def diff_attention(qkv, lam_vecs, n_ctx, lam_init):
    bsz, lt, _ = qkv.shape
    nh = DIFF_HEADS // DIFF_HPS
    w = DIFF_HPS * LANE
    return pl.pallas_call(
        functools.partial(_diff_kernel, n_ctx=n_ctx, lam_init=lam_init),
        grid=(bsz, nh, lt // TM),
        in_specs=[pl.BlockSpec((4, DIFF_HD), lambda b, h, j: (0, 0)),
                  pl.BlockSpec((1, TM, w), lambda b, h, j: (b, j, h)),
                  pl.BlockSpec((1, lt, w), lambda b, h, j: (b, 0, nh + h)),
                  pl.BlockSpec((1, lt, w), lambda b, h, j: (b, 0, 2 * nh + h))],
        out_specs=pl.BlockSpec((1, TM, w), lambda b, h, j: (b, j, h)),
        out_shape=jax.ShapeDtypeStruct((bsz, lt, DIFF_HEADS * LANE), F32),
        compiler_params=_cp(("parallel", "parallel", "arbitrary")),
        name="diff_attention",
    )(lam_vecs, qkv, qkv, qkv)


def _conv_kernel(x_ref, p_ref, n_ref, w_ref, b_ref, o_ref, *, n_ctx_tiles, n_tiles):
    j = pl.program_id(1)
    first = jnp.logical_or(j == 0, j == n_ctx_tiles)
    last = jnp.logical_or(j == n_ctx_tiles - 1, j == n_tiles - 1)
    x = x_ref[0]
    prev = jnp.where(first, 0.0, p_ref[0])
    nxt = jnp.where(last, 0.0, n_ref[0])
    xe = jnp.concatenate([prev, x, nxt], axis=0)
    ne = xe.shape[0]
    half = SSD_CONV // 2
    y = b_ref[...] + w_ref[half:half + 1, :] * x
    for t in range(SSD_CONV):
        if t == half:
            continue
        sh = pltpu.roll(xe, (half - t) % ne, axis=0)[8:8 + TM]
        y = y + w_ref[t:t + 1, :] * sh
    o_ref[0] = _silu(y)


def ssd_conv(xbc, conv_w, conv_b, n_ctx):
    bsz, lt, ch = xbc.shape
    nt = lt // TM
    r8 = TM // 8
    return pl.pallas_call(
        functools.partial(_conv_kernel, n_ctx_tiles=n_ctx // TM, n_tiles=nt),
        grid=(bsz, nt),
        in_specs=[pl.BlockSpec((1, TM, ch), lambda b, j: (b, j, 0)),
                  pl.BlockSpec((1, 8, ch), lambda b, j: (b, jnp.maximum(j * r8 - 1, 0), 0)),
                  pl.BlockSpec((1, 8, ch), lambda b, j: (b, jnp.minimum((j + 1) * r8, lt // 8 - 1), 0)),
                  pl.BlockSpec((SSD_CONV, ch), lambda b, j: (0, 0)),
                  pl.BlockSpec((1, ch), lambda b, j: (0, 0))],
        out_specs=pl.BlockSpec((1, TM, ch), lambda b, j: (b, j, 0)),
        out_shape=jax.ShapeDtypeStruct((bsz, lt, ch), F32),
        compiler_params=_cp(("parallel", "parallel")),
        name="ssd_conv",
    )(xbc, xbc, xbc, conv_w, conv_b.reshape(1, ch))


def _ssd_kernel(xs_ref, bm_ref, cm_ref, dt_ref, dtb_ref, al_ref, ex_ref, y_ref, st_ref, *, rev):
    qn = SSD_CHUNK
    gw = SSD_R * SSD_P

    @pl.when(pl.program_id(1) == 0)
    def _():
        st_ref[...] = jnp.zeros_like(st_ref)

    dt = _softplus(dt_ref[0] + dtb_ref[...])
    da = dt * (-jnp.exp(al_ref[...]))
    ri = lax.broadcasted_iota(I32, (qn, qn), 0)
    ci = lax.broadcasted_iota(I32, (qn, qn), 1)
    causal = (ci >= ri) if rev else (ci <= ri)
    tri = jnp.where(causal, 1.0, 0.0).astype(BF16)
    acum = _dot_exact_rhs(tri, da)
    acum_t = acum.T
    dt_t = dt.T
    tot = acum[0:1, :] if rev else acum[qn - 1:qn, :]
    lane = lax.broadcasted_iota(I32, (qn, 2 * SSD_P), 1)
    e_acum = jnp.exp(acum)
    w_state = jnp.exp(tot - acum) * dt
    e_tot8 = jnp.broadcast_to(jnp.exp(tot), (8, LANE))

    for g in range(SSD_G):
        xs = xs_ref[0, :, g * gw:(g + 1) * gw]
        bm = bm_ref[0, :, g * SSD_N:(g + 1) * SSD_N]
        cmat = cm_ref[0, :, g * SSD_N:(g + 1) * SSD_N]
        ex = ex_ref[g]

        cb = _dot_nt(cmat.astype(BF16), bm.astype(BF16))
        xs16 = xs.astype(BF16)
        pieces = []
        for rp in range(SSD_R // 2):
            acc = None
            for sub in range(2):
                r = g * SSD_R + 2 * rp + sub
                seg = acum[:, r:r + 1] - acum_t[r:r + 1, :]
                w = cb * jnp.exp(jnp.minimum(seg, 0.0)) * dt_t[r:r + 1, :]
                w = jnp.where(causal, w, 0.0).astype(BF16)
                xpair = xs16[:, rp * 2 * SSD_P:(rp + 1) * 2 * SSD_P]
                keep = (lane < SSD_P) if sub == 0 else (lane >= SSD_P)
                part = _dot(w, jnp.where(keep, xpair, jnp.zeros_like(xpair)))
                acc = part if acc is None else acc + part
            pieces.append(acc)
        y = jnp.concatenate(pieces, axis=1)

        st = st_ref[g]
        e_i = _dot_01(e_acum, ex)
        y_ref[0, :, g * gw:(g + 1) * gw] = y + _dot(cmat.astype(BF16), st.astype(BF16)) * e_i
        wt = _dot_01(w_state, ex)
        e_tot = _dot_01(e_tot8, ex)[0:1, :]
        st_ref[g] = st * e_tot + _dot(bm.T.astype(BF16), (xs * wt).astype(BF16))


def ssd_scan(xc, dt, dtb, alog, expand, n_ctx, rev):
    bsz, lt, _ = xc.shape
    qn = SSD_CHUNK
    nch, ncc = lt // qn, n_ctx // qn
    d = 1 if rev else 0
    gw = SSD_R * SSD_P
    cm = functools.partial(_scan_chunk, n_ctx_chunks=ncc, n_chunks=nch, rev=rev)
    din, gn = SSD_HEADS * SSD_P, SSD_G * SSD_N
    return pl.pallas_call(
        functools.partial(_ssd_kernel, rev=rev),
        grid=(bsz, nch),
        in_specs=[pl.BlockSpec((1, qn, din), lambda b, i: (b, cm(i), 0)),
                  pl.BlockSpec((1, qn, gn), lambda b, i: (b, cm(i), din // gn)),
                  pl.BlockSpec((1, qn, gn), lambda b, i: (b, cm(i), din // gn + 1)),
                  pl.BlockSpec((1, qn, LANE), lambda b, i: (b, cm(i), d)),
                  pl.BlockSpec((1, LANE), lambda b, i: (0, d)),
                  pl.BlockSpec((1, LANE), lambda b, i: (0, d)),
                  pl.BlockSpec((SSD_G, LANE, gw), lambda b, i: (0, 0, 0))],
        out_specs=pl.BlockSpec((1, qn, din), lambda b, i: (b, cm(i), 0)),
        out_shape=jax.ShapeDtypeStruct((bsz, lt, din), F32),
        scratch_shapes=[pltpu.VMEM((SSD_G, SSD_N, gw), F32)],
        compiler_params=_cp(("parallel", "arbitrary")),
        name="ssd_scan_bwd" if rev else "ssd_scan_fwd",
    )(xc, xc, xc, dt, dtb, alog, expand)


SUBL = 8


def _vrow(ref, k):
    return ref[SUBL * k:SUBL * (k + 1), :]


def _top16(tasks):
    def body(r, carry):
        out = pl.ds(pl.multiple_of(r * SUBL, SUBL), SUBL)
        for s_ref, ids, vals_ref, idx_ref in tasks:
            rows = [_vrow(s_ref, k) for k in range(len(ids))]
            level = list(zip(rows, ids))
            while len(level) > 1:
                nxt = []
                for j in range(0, len(level) - 1, 2):
                    (va, ia), (vb, ib) = level[j], level[j + 1]
                    gt = vb > va
                    nxt.append((jnp.where(gt, vb, va), jnp.where(gt, ib, ia)))
                if len(level) % 2:
                    nxt.append(level[-1])
                level = nxt
            m, sel = level[0]
            for k, v in enumerate(rows):
                s_ref[SUBL * k:SUBL * (k + 1), :] = jnp.where(sel == ids[k], -jnp.inf, v)
            vals_ref[out, :] = m
            idx_ref[out, :] = sel
        return carry
    lax.fori_loop(0, PEER_TOPK, body, 0, unroll=2)


ROUTE_SUB = ROUTE_TM // LANE
assert ROUTE_SUB == SUBL
N_PAIRS = sum(1 for p in range(PEER_TOPK) for q in range(PEER_TOPK) if (p + 1) * (q + 1) <= PEER_TOPK)


def _peer_route_kernel(f_ref, wh_ref, wl_ref, kh_ref, kl_ref, a_ref, b_ref, g_ref,
                       q_ref, s1_ref, s2_ref, c_ref, v1_ref, i1_ref, v2_ref, i2_ref, vb_ref, ib_ref):
    h = pl.program_id(1)
    nk = PEER_KEYS
    rows_q = 256

    @pl.when(h == 0)
    def _():
        for rc in range(ROUTE_TM // rows_q):
            rs = slice(rc * rows_q, (rc + 1) * rows_q)
            fh, fl = _split2(f_ref[rs, :])
            q_ref[rs, :] = _dot(fh, wh_ref[...]) + _dot(fh, wl_ref[...]) + _dot(fl, wh_ref[...])

    for z, s_ref in enumerate((s1_ref, s2_ref)):
        col = pl.multiple_of((h * 2 + z) * nk, nk)
        kh, kl = kh_ref[z], kl_ref[z]
        for c in range(ROUTE_SUB):
            qh, ql = _split2(q_ref[c * LANE:(c + 1) * LANE, pl.ds(col, nk)])
            st = _dot_nt(kh, qh) + _dot_nt(kh, ql) + _dot_nt(kl, qh)
            s_ref[pl.ds(c, nk, stride=ROUTE_SUB), :] = st
    keys = list(range(nk))
    _top16([(s1_ref, keys, v1_ref, i1_ref), (s2_ref, keys, v2_ref, i2_ref)])

    pairs = [(p, q) for p in range(PEER_TOPK) for q in range(PEER_TOPK) if (p + 1) * (q + 1) <= PEER_TOPK]
    for j, (p, q) in enumerate(pairs):
        c_ref[SUBL * j:SUBL * (j + 1), :] = _vrow(v1_ref, p) + _vrow(v2_ref, q)
    _top16([(c_ref, [p * PEER_TOPK + q for p, q in pairs], vb_ref, ib_ref)])

    fold3 = (PEER_TOPK, ROUTE_SUB, LANE)
    sel = ib_ref[...].reshape(fold3)
    p, qq = sel >> 4, sel & (PEER_TOPK - 1)
    i1, i2 = i1_ref[...].reshape(fold3), i2_ref[...].reshape(fold3)
    a = jnp.zeros_like(sel)
    b = jnp.zeros_like(sel)
    for t in range(PEER_TOPK):
        a = jnp.where(p == t, i1[t:t + 1], a)
        b = jnp.where(qq == t, i2[t:t + 1], b)
    vb = vb_ref[...].reshape(fold3)
    e = jnp.exp(vb - vb[0:1])
    a_ref[0] = a
    b_ref[0] = b
    g_ref[0] = e / jnp.sum(e, axis=0, keepdims=True)


def peer_route(f2, wq_hi, wq_lo, sk_hi, sk_lo):
    t, d = f2.shape
    nq = wq_hi.shape[1]
    k = PEER_TOPK
    tm = ROUTE_TM
    fold = (ROUTE_SUB, LANE)
    outspec = pl.BlockSpec((1, k) + fold, lambda i, h: (h, 0, i, 0))
    oshape = (PEER_HEADS, k, t // LANE, LANE)
    a, b, g = pl.pallas_call(
        _peer_route_kernel,
        grid=(t // tm, PEER_HEADS),
        in_specs=[pl.BlockSpec((tm, d), lambda i, h: (i, 0)),
                  pl.BlockSpec((d, nq), lambda i, h: (0, 0)),
                  pl.BlockSpec((d, nq), lambda i, h: (0, 0)),
                  pl.BlockSpec(sk_hi.shape, lambda i, h: (0, 0, 0)),
                  pl.BlockSpec(sk_lo.shape, lambda i, h: (0, 0, 0))],
        out_specs=[outspec, outspec, outspec],
        out_shape=[jax.ShapeDtypeStruct(oshape, I32)] * 2 + [jax.ShapeDtypeStruct(oshape, F32)],
        scratch_shapes=[pltpu.VMEM((tm, nq), F32), pltpu.VMEM((PEER_KEYS * SUBL, LANE), F32),
                        pltpu.VMEM((PEER_KEYS * SUBL, LANE), F32), pltpu.VMEM((N_PAIRS * SUBL, LANE), F32),
                        pltpu.VMEM((k * SUBL, LANE), F32), pltpu.VMEM((k * SUBL, LANE), I32),
                        pltpu.VMEM((k * SUBL, LANE), F32), pltpu.VMEM((k * SUBL, LANE), I32),
                        pltpu.VMEM((k * SUBL, LANE), F32), pltpu.VMEM((k * SUBL, LANE), I32)],
        compiler_params=_cp(("parallel", "arbitrary")),
        name="peer_route",
    )(f2, wq_hi, wq_lo, sk_hi, sk_lo)
    return a.reshape(PEER_HEADS, k, t), b.reshape(PEER_HEADS, k, t), g.reshape(PEER_HEADS, k, t)


def _peer_dense_kernel(f_ref, h_ref, gm_ref, a_ref, b_ref, g_ref, u_ref, v_ref, o_ref,
                       f16_ref, ar_ref, br_ref, gr_ref, gs_ref, acc_ref, *, tiles_per_batch, n_ctx_tiles):
    i, e = pl.program_id(0), pl.program_id(1)
    nk = PEER_KEYS
    half = nk // 2
    hi_mask = jnp.uint32(0xFFFF0000)

    @pl.when(e == 0)
    def _():
        f16_ref[...] = f_ref[...].astype(BF16)
        acc_ref[...] = jnp.zeros_like(acc_ref)
        ar_ref[...] = a_ref[...].T
        br_ref[...] = b_ref[...].T
        gr_ref[...] = g_ref[...].T
        r = lax.broadcasted_iota(I32, (nk, nk), 0)
        key1 = jnp.where(r < half, 2 * r, 2 * (r - half) + 1)
        key2 = r

        def per_token(t, carry):
            arow = ar_ref[pl.ds(t, 1), :]
            brow = br_ref[pl.ds(t, 1), :]
            grow = 0.5 * gr_ref[pl.ds(t, 1), :]
            ga = jnp.where(key1 == arow, grow, 0.0).astype(BF16)
            ob = jnp.where(key2 == brow, 1.0, 0.0).astype(BF16)
            gm = _dot_nt(ga, ob).astype(BF16).astype(F32)
            bits = lax.bitcast_convert_type(gm, jnp.uint32)
            off = pl.multiple_of(t * G_PITCH, 8)
            gs_ref[pl.ds(off, half), :] = bits[:half] | (bits[half:] >> 16)
            return carry
        lax.fori_loop(0, PEER_TM, per_token, 0, unroll=64)

    f16 = f16_ref[...]
    sub = 2 * nk
    ws = []
    for c in range(PEER_TE // sub):
        s = _dot_nt(f16, u_ref[c * sub:(c + 1) * sub, :])
        word = gs_ref[pl.ds(e * (PEER_TE // sub) + c, PEER_TM, stride=G_PITCH), :]
        gt = jnp.concatenate([lax.bitcast_convert_type(word & hi_mask, F32),
                              lax.bitcast_convert_type(word << 16, F32)], axis=1)
        act = s * (1.0 + lax.erf(s * (2.0 ** -0.5)))
        ws.append((act * gt).astype(BF16))
    acc_ref[...] += _dot(jnp.concatenate(ws, axis=1), v_ref[...])

    @pl.when(e == pl.num_programs(1) - 1)
    def _():
        for part in range(PEER_TM // TM):
            j = i * (PEER_TM // TM) + part
            row = _mod_row(j // tiles_per_batch, j % tiles_per_batch, n_ctx_tiles)
            sl = slice(part * TM, (part + 1) * TM)
            o_ref[sl, :] = h_ref[sl, :] + gm_ref[0, pl.ds(row, 1), :] * acc_ref[sl, :]


def peer_dense(f2, h2, mods, layer, a_t, b_t, g_t, u16, v16, tiles_per_batch, n_ctx):
    t, d = f2.shape
    ne = u16.shape[0]
    hk = PEER_HEADS * PEER_TOPK
    kern = functools.partial(_peer_dense_kernel, tiles_per_batch=tiles_per_batch, n_ctx_tiles=n_ctx // TM)
    tok = lambda: pl.BlockSpec((PEER_TM, d), lambda i, e: (i, 0))
    rt = lambda: pl.BlockSpec((hk, PEER_TM), lambda i, e: (0, i))
    return pl.pallas_call(
        kern,
        grid=(t // PEER_TM, ne // PEER_TE),
        in_specs=[tok(), tok(), pl.BlockSpec((1, 8, d), lambda i, e: (layer, 0, 5)), rt(), rt(), rt(),
                  pl.BlockSpec((PEER_TE, d), lambda i, e: (e, 0)),
                  pl.BlockSpec((PEER_TE, d), lambda i, e: (e, 0))],
        out_specs=tok(),
        out_shape=jax.ShapeDtypeStruct((t, d), F32),
        scratch_shapes=[pltpu.VMEM((PEER_TM, d), BF16), pltpu.VMEM((PEER_TM, hk), I32),
                        pltpu.VMEM((PEER_TM, hk), I32), pltpu.VMEM((PEER_TM, hk), F32),
                        pltpu.VMEM((PEER_TM * G_PITCH, PEER_KEYS), jnp.uint32), pltpu.VMEM((PEER_TM, d), F32)],
        compiler_params=_cp(("parallel", "arbitrary")),
        name="peer_dense",
    )(f2, h2, mods, a_t, b_t, g_t, u16, v16)


def peer_ffn(f, h, mods, layer, wq, subkeys, u, v, n_ctx):
    bsz, lt, d = h.shape
    t = bsz * lt
    f2, h2 = f.reshape(t, d), h.reshape(t, d)
    wq_hi, wq_lo = _split2(wq)
    sk_hi, sk_lo = _split2(subkeys)
    a, b, g = peer_route(f2, wq_hi, wq_lo, sk_hi, sk_lo)
    hk = PEER_HEADS * PEER_TOPK
    out = peer_dense(f2, h2, mods, layer, a.reshape(hk, t), b.reshape(hk, t), g.reshape(hk, t),
                     u.astype(BF16), v.astype(BF16), lt // TM, n_ctx)
    return out.reshape(bsz, lt, d)


def _final_kernel(h_ref, g_ref, o_ref):
    o_ref[0] = _rms(h_ref[0], g_ref[...])


def final_norm(h, g, n_ctx):
    bsz, lt, d = h.shape
    l = lt - n_ctx
    off = n_ctx // TM
    return pl.pallas_call(
        _final_kernel,
        grid=(bsz, l // TM),
        in_specs=[pl.BlockSpec((1, TM, d), lambda b, j: (b, j + off, 0)),
                  pl.BlockSpec((1, d), lambda b, j: (0, 0))],
        out_specs=pl.BlockSpec((1, TM, d), lambda b, j: (b, j, 0)),
        out_shape=jax.ShapeDtypeStruct((bsz, l, d), F32),
        compiler_params=_cp(("parallel", "parallel")),
        name="final_norm",
    )(h, g.reshape(1, d))


def _rope_tables(l, n_ctx):
    rows = l // GRID_W
    row = jnp.repeat(jnp.arange(rows), GRID_W).astype(F32)
    col = jnp.tile(jnp.arange(GRID_W), rows).astype(F32)
    n_freq = DIFF_HD // 4
    freqs = ROPE_THETA ** (-jnp.arange(n_freq, dtype=F32) / n_freq)
    ang = jnp.concatenate([row[:, None] * freqs, col[:, None] * freqs], axis=-1)
    cos, sin = jnp.cos(ang), jnp.sin(ang)
    cos = jnp.concatenate([jnp.ones((n_ctx, DIFF_HD // 2), F32), cos], axis=0)
    sin = jnp.concatenate([jnp.zeros((n_ctx, DIFF_HD // 2), F32), sin], axis=0)
    cos_t = jnp.concatenate([cos, cos, cos, cos], axis=1)
    sin_t = jnp.concatenate([-sin, sin, -sin, sin], axis=1)
    return cos_t, sin_t


def _gla_layer(h, mods, i, j, p, n_ctx):
    hk = GLA_HEADS * GLA_DK
    hv = GLA_HEADS * GLA_DV
    w_in = p["gla_w_in"][j]
    n_main = 2 * hk + 2 * hv
    w_lo = jnp.pad(w_in[:, n_main:], ((0, 0), (0, LANE - 2 * GLA_RANK))).astype(BF16)
    qkvg, lo = inproj(h, p["norm1_g"][i], mods, i, 0, [w_in[:, :n_main].astype(BF16), w_lo], n_ctx)
    wa = p["gla_w_alpha"][j]
    wa_pad = jnp.stack([jnp.pad(wa[0], ((0, LANE - GLA_RANK), (0, 0))),
                        jnp.pad(wa[1], ((GLA_RANK, LANE - 2 * GLA_RANK), (0, 0)))])
    ba = p["gla_b_alpha"][j].reshape(2, 1, hk)
    o_f = gla_scan(qkvg, lo, wa_pad, ba, n_ctx, False)
    o_b = gla_scan(qkvg, lo, wa_pad, ba, n_ctx, True)
    tile = lambda: pl.BlockSpec((1, TM, hv), lambda b, t: (b, t, 0))
    specs = [tile(), tile(), pl.BlockSpec((1, TM, hv), lambda b, t: (b, t, 2)),
             pl.BlockSpec((1, GLA_DV), lambda b, t: (0, 0))]
    args = [o_f, o_b, qkvg, p["gla_norm_g"][j].reshape(1, GLA_DV)]
    return finish("gla", args, specs, p["gla_w_out"][j].astype(BF16), h, p["norm2_g"][i], mods, i, n_ctx)


def _diff_layer(h, mods, i, j, p, n_ctx, rope_tabs):
    wd = DIFF_HEADS * 2 * DIFF_HD
    cos_t, sin_t = rope_tabs
    lt = h.shape[1]
    qkv, = inproj(h, p["norm1_g"][i], mods, i, 0, [p["diff_w_in"][j].astype(BF16)], n_ctx,
                  out_dtype=BF16, rope=(cos_t, sin_t, 2 * wd, wd, DIFF_HD ** -0.5 * math.log2(math.e)))
    lam_init = 0.8 - 0.6 * math.exp(-0.3 * i)
    o = diff_attention(qkv, p["diff_lambda"][j], n_ctx, lam_init)
    specs = [pl.BlockSpec((1, TM, wd), lambda b, t: (b, t, 0)),
             pl.BlockSpec((1, 2 * DIFF_HD), lambda b, t: (0, 0))]
    args = [o, p["diff_norm_g"][j].reshape(1, 2 * DIFF_HD)]
    return finish("diff", args, specs, p["diff_w_out"][j].astype(BF16), h, p["norm2_g"][i], mods, i, n_ctx,
                  prm={"lam_init": lam_init})


def _ssd_layer(h, mods, i, j, p, n_ctx):
    din = SSD_HEADS * SSD_P
    gn = SSD_G * SSD_N
    w_in = p["ssd_w_in"][j]
    g1 = p["norm1_g"][i]
    w_dt = w_in[:, 2 * din + 2 * gn:].reshape(-1, 2, SSD_HEADS)
    w_dt = jnp.pad(w_dt, ((0, 0), (0, 0), (0, LANE - SSD_HEADS))).reshape(-1, 2 * LANE)
    z, xbc, dt = inproj(h, g1, mods, i, 0, [w_in[:, :din].astype(BF16),
                                              w_in[:, din:2 * din + 2 * gn].astype(BF16), w_dt.astype(BF16)], n_ctx)
    pad_h = lambda a: jnp.pad(a, ((0, 0), (0, LANE - SSD_HEADS))).reshape(1, -1)
    dtb, alog = pad_h(p["ssd_dt_bias"][j]), pad_h(p["ssd_a_log"][j])
    xc = ssd_conv(xbc, p["ssd_conv_w"][j], p["ssd_conv_b"][j], n_ctx)
    head_of_col = jnp.arange(SSD_G)[:, None, None] * SSD_R + jnp.arange(SSD_R * SSD_P)[None, None, :] // SSD_P
    expand = (jnp.arange(LANE)[None, :, None] == head_of_col).astype(BF16)
    y_f = ssd_scan(xc, dt, dtb, alog, expand, n_ctx, False)
    y_b = ssd_scan(xc, dt, dtb, alog, expand, n_ctx, True)
    tile = lambda: pl.BlockSpec((1, TM, din), lambda b, t: (b, t, 0))
    row = lambda: pl.BlockSpec((1, din), lambda b, t: (0, 0))
    specs = [tile(), tile(), tile(), tile(), row(), row()]
    dexp = jnp.repeat(p["ssd_d"][j], SSD_P).reshape(1, din)
    args = [y_f, y_b, xc, z, dexp, p["ssd_norm_g"][j].reshape(1, din)]
    return finish("ssd", args, specs, p["ssd_w_out"][j].astype(BF16), h, p["norm2_g"][i], mods, i, n_ctx)


def kernel(x, c, ctx, c_ctx, norm1_g, norm2_g, w_mod, b_mod, peer_wq, peer_subkeys, peer_u, peer_v, gla_w_in, gla_w_alpha, gla_b_alpha, gla_norm_g, gla_w_out, diff_w_in, diff_lambda, diff_norm_g, diff_w_out, ssd_w_in, ssd_conv_w, ssd_conv_b, ssd_dt_bias, ssd_a_log, ssd_d, ssd_norm_g, ssd_w_out, final_g):
    p = dict(norm1_g=norm1_g, norm2_g=norm2_g, gla_w_in=gla_w_in, gla_w_alpha=gla_w_alpha, gla_b_alpha=gla_b_alpha,
             gla_norm_g=gla_norm_g, gla_w_out=gla_w_out, diff_w_in=diff_w_in, diff_lambda=diff_lambda,
             diff_norm_g=diff_norm_g, diff_w_out=diff_w_out, ssd_w_in=ssd_w_in, ssd_conv_w=ssd_conv_w,
             ssd_conv_b=ssd_conv_b, ssd_dt_bias=ssd_dt_bias, ssd_a_log=ssd_a_log, ssd_d=ssd_d,
             ssd_norm_g=ssd_norm_g, ssd_w_out=ssd_w_out)
    bsz, l, d = x.shape
    n_ctx = ctx.shape[1]
    depth = w_mod.shape[0]
    assert bsz <= 4 and n_ctx % TM == 0 and l % TM == 0 and l % GRID_W == 0
    cond8 = jnp.concatenate([c, jnp.zeros((4 - bsz, d), F32), c_ctx[None], jnp.zeros((3, d), F32)], axis=0)
    mods = mod_table(cond8, w_mod, b_mod)
    rope_tabs = _rope_tables(l, n_ctx)
    h = jnp.concatenate([ctx, x], axis=1)
    for i in range(depth):
        kind, j = i % N_MIXERS, i // N_MIXERS
        if kind == 0:
            h, f = _gla_layer(h, mods, i, j, p, n_ctx)
        elif kind == 1:
            h, f = _diff_layer(h, mods, i, j, p, n_ctx, rope_tabs)
        else:
            h, f = _ssd_layer(h, mods, i, j, p, n_ctx)
        h = peer_ffn(f, h, mods, i, peer_wq[i], peer_subkeys[i], peer_u[i], peer_v[i], n_ctx)
    return final_norm(h, final_g, n_ctx)
```

```python
import functools
import math

import jax
import jax.numpy as jnp
from jax import lax
from jax.experimental import pallas as pl
from jax.experimental.pallas import tpu as pltpu

F32 = jnp.float32
BF16 = jnp.bfloat16
I32 = jnp.int32

EPS = 1e-6
GRID_W = 64
ROPE_THETA = 10000.0
N_MIXERS = 3

GLA_HEADS, GLA_DK, GLA_DV, GLA_RANK, GLA_TAU = 4, 128, 256, 16, 16.0
GLA_CHUNK = 128
DIFF_HEADS, DIFF_HD = 8, 64
DIFF_HPS = 2
SSD_HEADS, SSD_P, SSD_N, SSD_G, SSD_CONV, SSD_CHUNK = 32, 64, 128, 4, 5, 128
SSD_R = SSD_HEADS // SSD_G
PEER_KEYS, PEER_HEADS, PEER_TOPK = 128, 8, 16

TM = 256
LANE = 128
ROUTE_TM = 1024
PEER_TM = 512
PEER_TE = 2048
G_PITCH = 72
VMEM_LIMIT = 56 * 1024 * 1024


def _cp(sem, vmem=VMEM_LIMIT):
    return pltpu.CompilerParams(dimension_semantics=sem, vmem_limit_bytes=vmem)


def _dot(a, b):
    return jnp.dot(a, b, preferred_element_type=F32)


def _dot_nt(a, b):
    return lax.dot_general(a, b, (((1,), (1,)), ((), ())), preferred_element_type=F32)


def _split2(x):
    hi = x.astype(BF16)
    lo = (x - hi.astype(F32)).astype(BF16)
    return hi, lo


def _dot3(a, b, dot=_dot):
    ah, al = _split2(a)
    bh, bl = _split2(b)
    return dot(ah, bh) + dot(ah, bl) + dot(al, bh)


def _dot_exact_rhs(w01, x):
    h1 = x.astype(BF16)
    r1 = x - h1.astype(F32)
    h2 = r1.astype(BF16)
    h3 = (r1 - h2.astype(F32)).astype(BF16)
    return _dot(w01, h1) + _dot(w01, h2) + _dot(w01, h3)


def _dot_01(x, w01):
    xh, xl = _split2(x)
    return _dot(xh, w01) + _dot(xl, w01)


def _silu(x):
    return x * (1.0 / (1.0 + jnp.exp(-x)))


def _softplus(x):
    return jnp.maximum(x, 0.0) + jnp.log1p(jnp.exp(-jnp.abs(x)))


def _rms(x, g):
    return x * lax.rsqrt(jnp.mean(x * x, axis=-1, keepdims=True) + EPS) * g


def _mod_row(b, j, n_ctx_tiles):
    return jnp.where(j < n_ctx_tiles, 4, b)


def _mod_kernel(cond_ref, w_ref, b_ref, o_ref):
    c = _silu(cond_ref[...])
    o_ref[0] = _dot3(c, w_ref[0]) + b_ref[0]


def mod_table(cond8, w_mod, b_mod):
    depth, d, n = w_mod.shape
    tn = 1024
    return pl.pallas_call(
        _mod_kernel,
        grid=(depth, n // tn),
        in_specs=[pl.BlockSpec((8, d), lambda i, j: (0, 0)),
                  pl.BlockSpec((1, d, tn), lambda i, j: (i, 0, j)),
                  pl.BlockSpec((1, 1, tn), lambda i, j: (i, 0, j))],
        out_specs=pl.BlockSpec((1, 8, tn), lambda i, j: (i, 0, j)),
        out_shape=jax.ShapeDtypeStruct((depth, 8, n), F32),
        compiler_params=_cp(("parallel", "parallel")),
        name="mod_table",
    )(cond8, w_mod, b_mod.reshape(depth, 1, n))


INPROJ_COLS = 1024


def _inproj_kernel(h_ref, g_ref, sh_ref, sc_ref, *rest, n_w, n_ctx_tiles, rope_cols, q_cols, q_scale):
    w_refs, rest = rest[:n_w], rest[n_w:]
    if rope_cols:
        cos_ref, sin_ref = rest[:2]
        rest = rest[2:]
    o_refs = rest
    b, j = pl.program_id(0), pl.program_id(1)
    row = _mod_row(b, j, n_ctx_tiles)
    a = _rms(h_ref[0], g_ref[...]) * (1.0 + sc_ref[0, pl.ds(row, 1), :]) + sh_ref[0, pl.ds(row, 1), :]
    a = a.astype(BF16)
    for k, (w_ref, o_ref) in enumerate(zip(w_refs, o_refs)):
        n = w_ref.shape[1]
        for c0 in range(0, n, INPROJ_COLS):
            c1 = min(n, c0 + INPROJ_COLS)
            y = _dot(a, w_ref[:, c0:c1])
            if k == 0 and c0 < rope_cols:
                tn = c1 - c0
                lane = lax.broadcasted_iota(I32, y.shape, 1)
                first = (lane & (DIFF_HD - 1)) < (DIFF_HD // 2)
                part = jnp.where(first, pltpu.roll(y, tn - DIFF_HD // 2, axis=1),
                                 pltpu.roll(y, DIFF_HD // 2, axis=1))
                cs = jnp.concatenate([cos_ref[0]] * (tn // LANE), axis=1)
                sn = jnp.concatenate([sin_ref[0]] * (tn // LANE), axis=1)
                y = y * cs + part * sn
                if c0 < q_cols:
                    y = y * q_scale
            o_ref[0, :, c0:c1] = y.astype(o_ref.dtype)


def inproj(h, g, mods, layer, k_shift, ws, n_ctx, out_dtype=F32, rope=None):
    bsz, lt, d = h.shape
    nt = lt // TM
    kern = functools.partial(_inproj_kernel, n_w=len(ws), n_ctx_tiles=n_ctx // TM, rope_cols=rope[2] if rope else 0,
                             q_cols=rope[3] if rope else 0, q_scale=rope[4] if rope else 1.0)
    in_specs = [pl.BlockSpec((1, TM, d), lambda b, j: (b, j, 0)),
                pl.BlockSpec((1, d), lambda b, j: (0, 0)),
                pl.BlockSpec((1, 8, d), lambda b, j: (layer, 0, k_shift)),
                pl.BlockSpec((1, 8, d), lambda b, j: (layer, 0, k_shift + 1))]
    in_specs += [pl.BlockSpec(w.shape, lambda b, j: (0, 0)) for w in ws]
    args = [h, g.reshape(1, d), mods, mods, *ws]
    out_specs = [pl.BlockSpec((1, TM, w.shape[1]), lambda b, j: (b, j, 0)) for w in ws]
    out_shape = [jax.ShapeDtypeStruct((bsz, lt, w.shape[1]), out_dtype) for w in ws]
    if rope:
        in_specs += [pl.BlockSpec((1, TM, LANE), lambda b, j: (0, j, 0))] * 2
        args += [rope[0][None], rope[1][None]]
    return pl.pallas_call(
        kern,
        grid=(bsz, nt),
        in_specs=in_specs,
        out_specs=out_specs,
        out_shape=out_shape,
        compiler_params=_cp(("parallel", "parallel")),
        name="inproj",
    )(*args)


def _post_gla(refs, prm):
    o_f, o_b, gate, ng = refs
    o = o_f[0] + o_b[0]
    g = gate[0]
    outs = []
    for hd in range(GLA_HEADS):
        sl = slice(hd * GLA_DV, (hd + 1) * GLA_DV)
        outs.append(_rms(o[:, sl], ng[...]) * _silu(g[:, sl]))
    return jnp.concatenate(outs, axis=1)


def _post_diff(refs, prm):
    o, ng = refs
    x = o[0]
    outs = []
    for hd in range(DIFF_HEADS):
        sl = slice(hd * 2 * DIFF_HD, (hd + 1) * 2 * DIFF_HD)
        outs.append(_rms(x[:, sl], ng[...]) * (1.0 - prm["lam_init"]))
    return jnp.concatenate(outs, axis=1)


def _post_ssd(refs, prm):
    y_f, y_b, xs, z, dexp, ng = refs
    y = (y_f[0] + y_b[0] + dexp[...] * xs[0]) * _silu(z[0])
    gs = y.shape[1] // SSD_G
    outs = []
    for gi in range(SSD_G):
        sl = slice(gi * gs, (gi + 1) * gs)
        outs.append(_rms(y[:, sl], ng[:, sl]))
    return jnp.concatenate(outs, axis=1)


_POST = {"gla": (_post_gla, 4), "diff": (_post_diff, 2), "ssd": (_post_ssd, 6)}


def _finish_kernel(*refs, kind, prm, n_ctx_tiles):
    post, n_in = _POST[kind]
    mix = refs[:n_in]
    w_ref, h_ref, gm_ref, g2_ref, sh_ref, sc_ref, hn_ref, f_ref = refs[n_in:]
    b, j = pl.program_id(0), pl.program_id(1)
    row = _mod_row(b, j, n_ctx_tiles)
    y = post(mix, prm)
    o = _dot(y.astype(BF16), w_ref[...])
    hn = h_ref[0] + gm_ref[0, pl.ds(row, 1), :] * o
    hn_ref[0] = hn
    f_ref[0] = _rms(hn, g2_ref[...]) * (1.0 + sc_ref[0, pl.ds(row, 1), :]) + sh_ref[0, pl.ds(row, 1), :]


def finish(kind, mix_args, mix_specs, w_out, h, g2, mods, layer, n_ctx, prm=None):
    bsz, lt, d = h.shape
    nt = lt // TM
    dm = w_out.shape[0]
    tile = lambda: pl.BlockSpec((1, TM, d), lambda b, j: (b, j, 0))
    modspec = lambda k: pl.BlockSpec((1, 8, d), lambda b, j: (layer, 0, k))
    kern = functools.partial(_finish_kernel, kind=kind, prm=prm or {}, n_ctx_tiles=n_ctx // TM)
    return pl.pallas_call(
        kern,
        grid=(bsz, nt),
        in_specs=list(mix_specs) + [pl.BlockSpec((dm, d), lambda b, j: (0, 0)), tile(), modspec(2),
                                    pl.BlockSpec((1, d), lambda b, j: (0, 0)), modspec(3), modspec(4)],
        out_specs=[tile(), tile()],
        out_shape=[jax.ShapeDtypeStruct((bsz, lt, d), F32)] * 2,
        compiler_params=_cp(("parallel", "parallel")),
        name="finish_" + kind,
    )(*mix_args, w_out, h, mods, g2.reshape(1, d), mods, mods)


def _gla_kernel(qf_ref, kf_ref, vf_ref, lof_ref, qb_ref, kb_ref, vb_ref, lob_ref, wa_ref, ba_ref,
                of_ref, ob_ref, stf_ref, stb_ref):
    @pl.when(pl.program_id(1) == 0)
    def _():
        stf_ref[...] = jnp.zeros_like(stf_ref)
        stb_ref[...] = jnp.zeros_like(stb_ref)

    _gla_chunk(qf_ref, kf_ref, vf_ref, lof_ref, wa_ref[0], ba_ref[0], of_ref, stf_ref, rev=False)
    _gla_chunk(qb_ref, kb_ref, vb_ref, lob_ref, wa_ref[1], ba_ref[1], ob_ref, stb_ref, rev=True)


def _gla_chunk(q_ref, k_ref, v_ref, lo_ref, wa, ba, o_ref, st_ref, *, rev):
    c = GLA_CHUNK
    hk = GLA_HEADS * GLA_DK
    heads = [slice(hd * GLA_DK, (hd + 1) * GLA_DK) for hd in range(GLA_HEADS)]

    q = q_ref[0] * (GLA_DK ** -0.5)
    k = k_ref[0]
    v = v_ref[0]
    z = _dot3(lo_ref[0], wa) + ba
    la = (jnp.minimum(z, 0.0) - jnp.log1p(jnp.exp(-jnp.abs(z)))) * (1.0 / GLA_TAU)

    ri = lax.broadcasted_iota(I32, (c, c), 0)
    ci = lax.broadcasted_iota(I32, (c, c), 1)
    row = lax.broadcasted_iota(I32, (c, hk), 0)
    tri = jnp.where((ci >= ri) if rev else (ci <= ri), 1.0, 0.0).astype(BF16)
    bsum = _dot_exact_rhs(tri, la)
    tot = bsum[0:1, :] if rev else bsum[c - 1:c, :]

    q16, k16 = q.astype(BF16), k.astype(BF16)
    eye = ci == ri
    a = [jnp.where(eye, _dot_nt(q16[:, sl], k16[:, sl]), 0.0) for sl in heads]
    s = c // 2
    while s >= 1:
        pos = row & (2 * s - 1)
        q_half = (pos < s) if rev else (pos >= s)
        if s >= 4:
            blk = bsum.reshape(c // (2 * s), 2 * s, hk)
            rr = s if rev else s - 1
            ref = jnp.broadcast_to(blk[:, rr:rr + 1, :], blk.shape).reshape(c, hk)
            e = jnp.where(q_half, bsum - ref, ref - bsum)
        elif s == 2:
            pos4 = row & 3
            nxt, prv = pltpu.roll(la, c - 1, axis=0), pltpu.roll(la, 1, axis=0)
            if rev:
                e = jnp.where(pos4 == 0, la + nxt, jnp.where(pos4 == 1, la, jnp.where(pos4 == 2, 0.0, prv)))
            else:
                e = jnp.where(pos4 == 3, la + prv, jnp.where(pos4 == 2, la, jnp.where(pos4 == 1, 0.0, nxt)))
        else:
            e = jnp.where(q_half, la, 0.0)
        f = jnp.exp(e)
        qd = jnp.where(q_half, q * f, 0.0).astype(BF16)
        kd = jnp.where(q_half, 0.0, k * f).astype(BF16)
        same_block = (ri & -(2 * s)) == (ci & -(2 * s))
        a = [a[hd] + jnp.where(same_block, _dot_nt(qd[:, sl], kd[:, sl]), 0.0) for hd, sl in enumerate(heads)]
        s //= 2

    qe = (q * jnp.exp(bsum)).astype(BF16)
    kt = (k * jnp.exp(tot - bsum)).astype(BF16)
    et = jnp.exp(tot)
    for hd, sl in enumerate(heads):
        vh = v[:, hd * GLA_DV:(hd + 1) * GLA_DV]
        st = st_ref[hd]
        o_ref[0, :, hd * GLA_DV:(hd + 1) * GLA_DV] = (_dot(a[hd].astype(BF16), vh.astype(BF16))
                                                       + _dot_nt(qe[:, sl], st.astype(BF16)))
        st_ref[hd] = st * et[:, sl] + _dot(vh.T.astype(BF16), kt[:, sl])


def _scan_chunk(c, n_ctx_chunks, n_chunks, rev):
    if not rev:
        return c
    return jnp.where(c < n_ctx_chunks, n_ctx_chunks - 1 - c, n_chunks - 1 - (c - n_ctx_chunks))


def gla_scan(qkvg, lo, wa_pad, ba, n_ctx):
    bsz, lt, _ = qkvg.shape
    c = GLA_CHUNK
    nch, ncc = lt // c, n_ctx // c
    hk, hv = GLA_HEADS * GLA_DK, GLA_HEADS * GLA_DV

    def chunk_specs(rev):
        cm = functools.partial(_scan_chunk, n_ctx_chunks=ncc, n_chunks=nch, rev=rev)
        ins = [pl.BlockSpec((1, c, hk), lambda b, i: (b, cm(i), 0)),
               pl.BlockSpec((1, c, hk), lambda b, i: (b, cm(i), 1)),
               pl.BlockSpec((1, c, hv), lambda b, i: (b, cm(i), 1)),
               pl.BlockSpec((1, c, LANE), lambda b, i: (b, cm(i), 0))]
        return ins, pl.BlockSpec((1, c, hv), lambda b, i: (b, cm(i), 0))

    (in_f, out_f), (in_b, out_b) = chunk_specs(False), chunk_specs(True)
    state = pltpu.VMEM((GLA_HEADS, GLA_DV, GLA_DK), F32)
    return pl.pallas_call(
        _gla_kernel,
        grid=(bsz, nch),
        in_specs=in_f + in_b + [pl.BlockSpec((2, LANE, hk), lambda b, i: (0, 0, 0)),
                                pl.BlockSpec((2, 1, hk), lambda b, i: (0, 0, 0))],
        out_specs=[out_f, out_b],
        out_shape=[jax.ShapeDtypeStruct((bsz, lt, hv), F32)] * 2,
        scratch_shapes=[state, state],
        compiler_params=_cp(("parallel", "arbitrary")),
        name="gla_scan",
    )(qkvg, qkvg, qkvg, lo, qkvg, qkvg, qkvg, lo, wa_pad, ba)


def _diff_kernel(lam_ref, q_ref, k_ref, v_ref, o_ref, *, n_ctx, lam_init):
    j = pl.program_id(2)
    lv = lam_ref[...]
    l1 = jnp.sum(lv[0:1] * lv[1:2], axis=1, keepdims=True)
    l2 = jnp.sum(lv[2:3] * lv[3:4], axis=1, keepdims=True)
    lam = jnp.exp(l1) - jnp.exp(l2) + lam_init

    lane = lax.broadcasted_iota(I32, (TM, LANE), 1)

    def attend(n_keys):
        for hh in range(DIFF_HPS):
            cols = slice(hh * LANE, (hh + 1) * LANE)
            q = q_ref[0, :, cols]
            zero = jnp.zeros_like(q)
            q0 = jnp.where(lane < DIFF_HD, q, zero)
            q1 = jnp.where(lane < DIFF_HD, zero, q)
            k = k_ref[0, :n_keys, cols]
            v = v_ref[0, :n_keys, cols]
            s0 = _dot_nt(q0, k)
            s1 = _dot_nt(q1, k)
            p0 = jnp.exp2(s0 - jnp.max(s0, axis=1, keepdims=True))
            p1 = jnp.exp2(s1 - jnp.max(s1, axis=1, keepdims=True))
            r0 = 1.0 / jnp.sum(p0, axis=1, keepdims=True)
            r1 = lam / jnp.sum(p1, axis=1, keepdims=True)
            o_ref[0, :, cols] = _dot(p0.astype(BF16), v) * r0 - _dot(p1.astype(BF16), v) * r1

    @pl.when(j * TM < n_ctx)
    def _():
        attend(n_ctx)

    @pl.when(j * TM >= n_ctx)
    def _():
        attend(k_ref.shape[1])


def diff_attention(qkv, lam_vecs, n_ctx, lam_init):
    bsz, lt, _ = qkv.shape
    nh = DIFF_HEADS // DIFF_HPS
    w = DIFF_HPS * LANE
    return pl.pallas_call(
        functools.partial(_diff_kernel, n_ctx=n_ctx, lam_init=lam_init),
        grid=(bsz, nh, lt // TM),
        in_specs=[pl.BlockSpec((4, DIFF_HD), lambda b, h, j: (0, 0)),
                  pl.BlockSpec((1, TM, w), lambda b, h, j: (b, j, h)),
                  pl.BlockSpec((1, lt, w), lambda b, h, j: (b, 0, nh + h)),
                  pl.BlockSpec((1, lt, w), lambda b, h, j: (b, 0, 2 * nh + h))],
        out_specs=pl.BlockSpec((1, TM, w), lambda b, h, j: (b, j, h)),
        out_shape=jax.ShapeDtypeStruct((bsz, lt, DIFF_HEADS * LANE), F32),
        compiler_params=_cp(("parallel", "parallel", "arbitrary")),
        name="diff_attention",
    )(lam_vecs, qkv, qkv, qkv)


def _conv_kernel(x_ref, p_ref, n_ref, w_ref, b_ref, o_ref, *, n_ctx_tiles, n_tiles):
    j = pl.program_id(1)
    first = jnp.logical_or(j == 0, j == n_ctx_tiles)
    last = jnp.logical_or(j == n_ctx_tiles - 1, j == n_tiles - 1)
    x = x_ref[0]
    prev = jnp.where(first, 0.0, p_ref[0])
    nxt = jnp.where(last, 0.0, n_ref[0])
    xe = jnp.concatenate([prev, x, nxt], axis=0)
    ne = xe.shape[0]
    half = SSD_CONV // 2
    y = b_ref[...] + w_ref[half:half + 1, :] * x
    for t in range(SSD_CONV):
        if t == half:
            continue
        sh = pltpu.roll(xe, (half - t) % ne, axis=0)[8:8 + TM]
        y = y + w_ref[t:t + 1, :] * sh
    o_ref[0] = _silu(y)


def ssd_conv(xbc, conv_w, conv_b, n_ctx):
    bsz, lt, ch = xbc.shape
    nt = lt // TM
    r8 = TM // 8
    return pl.pallas_call(
        functools.partial(_conv_kernel, n_ctx_tiles=n_ctx // TM, n_tiles=nt),
        grid=(bsz, nt),
        in_specs=[pl.BlockSpec((1, TM, ch), lambda b, j: (b, j, 0)),
                  pl.BlockSpec((1, 8, ch), lambda b, j: (b, jnp.maximum(j * r8 - 1, 0), 0)),
                  pl.BlockSpec((1, 8, ch), lambda b, j: (b, jnp.minimum((j + 1) * r8, lt // 8 - 1), 0)),
                  pl.BlockSpec((SSD_CONV, ch), lambda b, j: (0, 0)),
                  pl.BlockSpec((1, ch), lambda b, j: (0, 0))],
        out_specs=pl.BlockSpec((1, TM, ch), lambda b, j: (b, j, 0)),
        out_shape=jax.ShapeDtypeStruct((bsz, lt, ch), F32),
        compiler_params=_cp(("parallel", "parallel")),
        name="ssd_conv",
    )(xbc, xbc, xbc, conv_w, conv_b.reshape(1, ch))


def _ssd_kernel(xsf_ref, bmf_ref, cmf_ref, dtf_ref, xsb_ref, bmb_ref, cmb_ref, dtb_ref, bias_ref, al_ref, ex_ref,
                yf_ref, yb_ref, stf_ref, stb_ref):
    @pl.when(pl.program_id(1) == 0)
    def _():
        stf_ref[...] = jnp.zeros_like(stf_ref)
        stb_ref[...] = jnp.zeros_like(stb_ref)

    _ssd_chunk(xsf_ref, bmf_ref, cmf_ref, dtf_ref, bias_ref[:, :LANE], al_ref[:, :LANE], ex_ref, yf_ref, stf_ref,
               rev=False)
    _ssd_chunk(xsb_ref, bmb_ref, cmb_ref, dtb_ref, bias_ref[:, LANE:], al_ref[:, LANE:], ex_ref, yb_ref, stb_ref,
               rev=True)


def _ssd_chunk(xs_ref, bm_ref, cm_ref, dt_ref, dt_bias, a_log, ex_ref, y_ref, st_ref, *, rev):
    qn = SSD_CHUNK
    gw = SSD_R * SSD_P

    dt = _softplus(dt_ref[0] + dt_bias)
    da = dt * (-jnp.exp(a_log))
    ri = lax.broadcasted_iota(I32, (qn, qn), 0)
    ci = lax.broadcasted_iota(I32, (qn, qn), 1)
    causal = (ci >= ri) if rev else (ci <= ri)
    tri = jnp.where(causal, 1.0, 0.0).astype(BF16)
    acum = _dot_exact_rhs(tri, da)
    acum_t = acum.T
    dt_t = dt.T
    tot = acum[0:1, :] if rev else acum[qn - 1:qn, :]
    lane = lax.broadcasted_iota(I32, (qn, 2 * SSD_P), 1)
    e_acum = jnp.exp(acum)
    w_state = jnp.exp(tot - acum) * dt
    e_tot8 = jnp.broadcast_to(jnp.exp(tot), (8, LANE))

    for g in range(SSD_G):
        xs = xs_ref[0, :, g * gw:(g + 1) * gw]
        bm = bm_ref[0, :, g * SSD_N:(g + 1) * SSD_N]
        cmat = cm_ref[0, :, g * SSD_N:(g + 1) * SSD_N]
        ex = ex_ref[g]

        cb = _dot_nt(cmat.astype(BF16), bm.astype(BF16))
        xs16 = xs.astype(BF16)
        pieces = []
        for rp in range(SSD_R // 2):
            acc = None
            for sub in range(2):
                r = g * SSD_R + 2 * rp + sub
                seg = acum[:, r:r + 1] - acum_t[r:r + 1, :]
                w = cb * jnp.exp(jnp.minimum(seg, 0.0)) * dt_t[r:r + 1, :]
                w = jnp.where(causal, w, 0.0).astype(BF16)
                xpair = xs16[:, rp * 2 * SSD_P:(rp + 1) * 2 * SSD_P]
                keep = (lane < SSD_P) if sub == 0 else (lane >= SSD_P)
                part = _dot(w, jnp.where(keep, xpair, jnp.zeros_like(xpair)))
                acc = part if acc is None else acc + part
            pieces.append(acc)
        y = jnp.concatenate(pieces, axis=1)

        st = st_ref[g]
        e_i = _dot_01(e_acum, ex)
        y_ref[0, :, g * gw:(g + 1) * gw] = y + _dot(cmat.astype(BF16), st.astype(BF16)) * e_i
        wt = _dot_01(w_state, ex)
        e_tot = _dot_01(e_tot8, ex)[0:1, :]
        st_ref[g] = st * e_tot + _dot(bm.T.astype(BF16), (xs * wt).astype(BF16))


def ssd_scan(xc, dt, dtb, alog, expand, n_ctx):
    bsz, lt, _ = xc.shape
    qn = SSD_CHUNK
    nch, ncc = lt // qn, n_ctx // qn
    gw = SSD_R * SSD_P
    din, gn = SSD_HEADS * SSD_P, SSD_G * SSD_N

    def chunk_specs(rev):
        cm = functools.partial(_scan_chunk, n_ctx_chunks=ncc, n_chunks=nch, rev=rev)
        d = 1 if rev else 0
        ins = [pl.BlockSpec((1, qn, din), lambda b, i: (b, cm(i), 0)),
               pl.BlockSpec((1, qn, gn), lambda b, i: (b, cm(i), din // gn)),
               pl.BlockSpec((1, qn, gn), lambda b, i: (b, cm(i), din // gn + 1)),
               pl.BlockSpec((1, qn, LANE), lambda b, i: (b, cm(i), d))]
        return ins, pl.BlockSpec((1, qn, din), lambda b, i: (b, cm(i), 0))

    (in_f, out_f), (in_b, out_b) = chunk_specs(False), chunk_specs(True)
    state = pltpu.VMEM((SSD_G, SSD_N, gw), F32)
    return pl.pallas_call(
        _ssd_kernel,
        grid=(bsz, nch),
        in_specs=in_f + in_b + [pl.BlockSpec((1, 2 * LANE), lambda b, i: (0, 0)),
                                pl.BlockSpec((1, 2 * LANE), lambda b, i: (0, 0)),
                                pl.BlockSpec((SSD_G, LANE, gw), lambda b, i: (0, 0, 0))],
        out_specs=[out_f, out_b],
        out_shape=[jax.ShapeDtypeStruct((bsz, lt, din), F32)] * 2,
        scratch_shapes=[state, state],
        compiler_params=_cp(("parallel", "arbitrary")),
        name="ssd_scan",
    )(xc, xc, xc, dt, xc, xc, xc, dt, dtb, alog, expand)


SUBL = 8


def _vrow(ref, k):
    return ref[SUBL * k:SUBL * (k + 1), :]


def _top16(tasks):
    def body(r, carry):
        out = pl.ds(pl.multiple_of(r * SUBL, SUBL), SUBL)
        for s_ref, ids, vals_ref, idx_ref in tasks:
            rows = [_vrow(s_ref, k) for k in range(len(ids))]
            level = list(zip(rows, ids))
            while len(level) > 1:
                nxt = []
                for j in range(0, len(level) - 1, 2):
                    (va, ia), (vb, ib) = level[j], level[j + 1]
                    gt = vb > va
                    nxt.append((jnp.where(gt, vb, va), jnp.where(gt, ib, ia)))
                if len(level) % 2:
                    nxt.append(level[-1])
                level = nxt
            m, sel = level[0]
            for k, v in enumerate(rows):
                s_ref[SUBL * k:SUBL * (k + 1), :] = jnp.where(sel == ids[k], -jnp.inf, v)
            vals_ref[out, :] = m
            idx_ref[out, :] = sel
        return carry
    lax.fori_loop(0, PEER_TOPK, body, 0, unroll=2)


ROUTE_SUB = ROUTE_TM // LANE
assert ROUTE_SUB == SUBL
N_PAIRS = sum(1 for p in range(PEER_TOPK) for q in range(PEER_TOPK) if (p + 1) * (q + 1) <= PEER_TOPK)


def _peer_route_kernel(f_ref, wh_ref, wl_ref, kh_ref, kl_ref, a_ref, b_ref, g_ref,
                       q_ref, s1_ref, s2_ref, c_ref, v1_ref, i1_ref, v2_ref, i2_ref, vb_ref, ib_ref):
    h = pl.program_id(1)
    nk = PEER_KEYS
    rows_q = 256

    @pl.when(h == 0)
    def _():
        for rc in range(ROUTE_TM // rows_q):
            rs = slice(rc * rows_q, (rc + 1) * rows_q)
            fh, fl = _split2(f_ref[rs, :])
            q_ref[rs, :] = _dot(fh, wh_ref[...]) + _dot(fh, wl_ref[...]) + _dot(fl, wh_ref[...])

    for z, s_ref in enumerate((s1_ref, s2_ref)):
        col = pl.multiple_of((h * 2 + z) * nk, nk)
        kh, kl = kh_ref[z], kl_ref[z]
        for c in range(ROUTE_SUB):
            qh, ql = _split2(q_ref[c * LANE:(c + 1) * LANE, pl.ds(col, nk)])
            st = _dot_nt(kh, qh) + _dot_nt(kh, ql) + _dot_nt(kl, qh)
            s_ref[pl.ds(c, nk, stride=ROUTE_SUB), :] = st
    keys = list(range(nk))
    _top16([(s1_ref, keys, v1_ref, i1_ref), (s2_ref, keys, v2_ref, i2_ref)])

    pairs = [(p, q) for p in range(PEER_TOPK) for q in range(PEER_TOPK) if (p + 1) * (q + 1) <= PEER_TOPK]
    for j, (p, q) in enumerate(pairs):
        c_ref[SUBL * j:SUBL * (j + 1), :] = _vrow(v1_ref, p) + _vrow(v2_ref, q)
    _top16([(c_ref, [p * PEER_TOPK + q for p, q in pairs], vb_ref, ib_ref)])

    fold3 = (PEER_TOPK, ROUTE_SUB, LANE)
    sel = ib_ref[...].reshape(fold3)
    p, qq = sel >> 4, sel & (PEER_TOPK - 1)
    i1, i2 = i1_ref[...].reshape(fold3), i2_ref[...].reshape(fold3)
    a = jnp.zeros_like(sel)
    b = jnp.zeros_like(sel)
    for t in range(PEER_TOPK):
        a = jnp.where(p == t, i1[t:t + 1], a)
        b = jnp.where(qq == t, i2[t:t + 1], b)
    vb = vb_ref[...].reshape(fold3)
    e = jnp.exp(vb - vb[0:1])
    a_ref[0] = a
    b_ref[0] = b
    g_ref[0] = e / jnp.sum(e, axis=0, keepdims=True)


def peer_route(f2, wq_hi, wq_lo, sk_hi, sk_lo):
    t, d = f2.shape
    nq = wq_hi.shape[1]
    k = PEER_TOPK
    tm = ROUTE_TM
    fold = (ROUTE_SUB, LANE)
    outspec = pl.BlockSpec((1, k) + fold, lambda i, h: (h, 0, i, 0))
    oshape = (PEER_HEADS, k, t // LANE, LANE)
    a, b, g = pl.pallas_call(
        _peer_route_kernel,
        grid=(t // tm, PEER_HEADS),
        in_specs=[pl.BlockSpec((tm, d), lambda i, h: (i, 0)),
                  pl.BlockSpec((d, nq), lambda i, h: (0, 0)),
                  pl.BlockSpec((d, nq), lambda i, h: (0, 0)),
                  pl.BlockSpec(sk_hi.shape, lambda i, h: (0, 0, 0)),
                  pl.BlockSpec(sk_lo.shape, lambda i, h: (0, 0, 0))],
        out_specs=[outspec, outspec, outspec],
        out_shape=[jax.ShapeDtypeStruct(oshape, I32)] * 2 + [jax.ShapeDtypeStruct(oshape, F32)],
        scratch_shapes=[pltpu.VMEM((tm, nq), F32), pltpu.VMEM((PEER_KEYS * SUBL, LANE), F32),
                        pltpu.VMEM((PEER_KEYS * SUBL, LANE), F32), pltpu.VMEM((N_PAIRS * SUBL, LANE), F32),
                        pltpu.VMEM((k * SUBL, LANE), F32), pltpu.VMEM((k * SUBL, LANE), I32),
                        pltpu.VMEM((k * SUBL, LANE), F32), pltpu.VMEM((k * SUBL, LANE), I32),
                        pltpu.VMEM((k * SUBL, LANE), F32), pltpu.VMEM((k * SUBL, LANE), I32)],
        compiler_params=_cp(("parallel", "arbitrary")),
        name="peer_route",
    )(f2, wq_hi, wq_lo, sk_hi, sk_lo)
    return a.reshape(PEER_HEADS, k, t), b.reshape(PEER_HEADS, k, t), g.reshape(PEER_HEADS, k, t)


def _peer_dense_kernel(f_ref, h_ref, gm_ref, a_ref, b_ref, g_ref, u_ref, v_ref, o_ref,
                       f16_ref, ar_ref, br_ref, gr_ref, gs_ref, acc_ref, *, tiles_per_batch, n_ctx_tiles):
    i, e = pl.program_id(0), pl.program_id(1)
    nk = PEER_KEYS
    half = nk // 2
    hi_mask = jnp.uint32(0xFFFF0000)

    @pl.when(e == 0)
    def _():
        f16_ref[...] = f_ref[...].astype(BF16)
        acc_ref[...] = jnp.zeros_like(acc_ref)
        ar_ref[...] = a_ref[...].T
        br_ref[...] = b_ref[...].T
        gr_ref[...] = g_ref[...].T
        r = lax.broadcasted_iota(I32, (nk, nk), 0)
        key1 = jnp.where(r < half, 2 * r, 2 * (r - half) + 1)
        key2 = r

        def per_token(t, carry):
            arow = ar_ref[pl.ds(t, 1), :]
            brow = br_ref[pl.ds(t, 1), :]
            grow = 0.5 * gr_ref[pl.ds(t, 1), :]
            ga = jnp.where(key1 == arow, grow, 0.0).astype(BF16)
            ob = jnp.where(key2 == brow, 1.0, 0.0).astype(BF16)
            gm = _dot_nt(ga, ob).astype(BF16).astype(F32)
            bits = lax.bitcast_convert_type(gm, jnp.uint32)
            off = pl.multiple_of(t * G_PITCH, 8)
            gs_ref[pl.ds(off, half), :] = bits[:half] | (bits[half:] >> 16)
            return carry
        lax.fori_loop(0, PEER_TM, per_token, 0, unroll=64)

    f16 = f16_ref[...]
    sub = 2 * nk
    ws = []
    for c in range(PEER_TE // sub):
        s = _dot_nt(f16, u_ref[c * sub:(c + 1) * sub, :])
        word = gs_ref[pl.ds(e * (PEER_TE // sub) + c, PEER_TM, stride=G_PITCH), :]
        gt = jnp.concatenate([lax.bitcast_convert_type(word & hi_mask, F32),
                              lax.bitcast_convert_type(word << 16, F32)], axis=1)
        act = s * (1.0 + lax.erf(s * (2.0 ** -0.5)))
        ws.append((act * gt).astype(BF16))
    acc_ref[...] += _dot(jnp.concatenate(ws, axis=1), v_ref[...])

    @pl.when(e == pl.num_programs(1) - 1)
    def _():
        for part in range(PEER_TM // TM):
            j = i * (PEER_TM // TM) + part
            row = _mod_row(j // tiles_per_batch, j % tiles_per_batch, n_ctx_tiles)
            sl = slice(part * TM, (part + 1) * TM)
            o_ref[sl, :] = h_ref[sl, :] + gm_ref[0, pl.ds(row, 1), :] * acc_ref[sl, :]


def peer_dense(f2, h2, mods, layer, a_t, b_t, g_t, u16, v16, tiles_per_batch, n_ctx):
    t, d = f2.shape
    ne = u16.shape[0]
    hk = PEER_HEADS * PEER_TOPK
    kern = functools.partial(_peer_dense_kernel, tiles_per_batch=tiles_per_batch, n_ctx_tiles=n_ctx // TM)
    tok = lambda: pl.BlockSpec((PEER_TM, d), lambda i, e: (i, 0))
    rt = lambda: pl.BlockSpec((hk, PEER_TM), lambda i, e: (0, i))
    return pl.pallas_call(
        kern,
        grid=(t // PEER_TM, ne // PEER_TE),
        in_specs=[tok(), tok(), pl.BlockSpec((1, 8, d), lambda i, e: (layer, 0, 5)), rt(), rt(), rt(),
                  pl.BlockSpec((PEER_TE, d), lambda i, e: (e, 0)),
                  pl.BlockSpec((PEER_TE, d), lambda i, e: (e, 0))],
        out_specs=tok(),
        out_shape=jax.ShapeDtypeStruct((t, d), F32),
        scratch_shapes=[pltpu.VMEM((PEER_TM, d), BF16), pltpu.VMEM((PEER_TM, hk), I32),
                        pltpu.VMEM((PEER_TM, hk), I32), pltpu.VMEM((PEER_TM, hk), F32),
                        pltpu.VMEM((PEER_TM * G_PITCH, PEER_KEYS), jnp.uint32), pltpu.VMEM((PEER_TM, d), F32)],
        compiler_params=_cp(("parallel", "arbitrary")),
        name="peer_dense",
    )(f2, h2, mods, a_t, b_t, g_t, u16, v16)


def peer_ffn(f, h, mods, layer, wq, subkeys, u, v, n_ctx):
    bsz, lt, d = h.shape
    t = bsz * lt
    f2, h2 = f.reshape(t, d), h.reshape(t, d)
    wq_hi, wq_lo = _split2(wq)
    sk_hi, sk_lo = _split2(subkeys)
    a, b, g = peer_route(f2, wq_hi, wq_lo, sk_hi, sk_lo)
    hk = PEER_HEADS * PEER_TOPK
    out = peer_dense(f2, h2, mods, layer, a.reshape(hk, t), b.reshape(hk, t), g.reshape(hk, t),
                     u.astype(BF16), v.astype(BF16), lt // TM, n_ctx)
    return out.reshape(bsz, lt, d)


def _final_kernel(h_ref, g_ref, o_ref):
    o_ref[0] = _rms(h_ref[0], g_ref[...])


def final_norm(h, g, n_ctx):
    bsz, lt, d = h.shape
    l = lt - n_ctx
    off = n_ctx // TM
    return pl.pallas_call(
        _final_kernel,
        grid=(bsz, l // TM),
        in_specs=[pl.BlockSpec((1, TM, d), lambda b, j: (b, j + off, 0)),
                  pl.BlockSpec((1, d), lambda b, j: (0, 0))],
        out_specs=pl.BlockSpec((1, TM, d), lambda b, j: (b, j, 0)),
        out_shape=jax.ShapeDtypeStruct((bsz, l, d), F32),
        compiler_params=_cp(("parallel", "parallel")),
        name="final_norm",
    )(h, g.reshape(1, d))


def _rope_tables(l, n_ctx):
    rows = l // GRID_W
    row = jnp.repeat(jnp.arange(rows), GRID_W).astype(F32)
    col = jnp.tile(jnp.arange(GRID_W), rows).astype(F32)
    n_freq = DIFF_HD // 4
    freqs = ROPE_THETA ** (-jnp.arange(n_freq, dtype=F32) / n_freq)
    ang = jnp.concatenate([row[:, None] * freqs, col[:, None] * freqs], axis=-1)
    cos, sin = jnp.cos(ang), jnp.sin(ang)
    cos = jnp.concatenate([jnp.ones((n_ctx, DIFF_HD // 2), F32), cos], axis=0)
    sin = jnp.concatenate([jnp.zeros((n_ctx, DIFF_HD // 2), F32), sin], axis=0)
    cos_t = jnp.concatenate([cos, cos, cos, cos], axis=1)
    sin_t = jnp.concatenate([-sin, sin, -sin, sin], axis=1)
    return cos_t, sin_t


def _gla_layer(h, mods, i, j, p, n_ctx):
    hk = GLA_HEADS * GLA_DK
    hv = GLA_HEADS * GLA_DV
    w_in = p["gla_w_in"][j]
    n_main = 2 * hk + 2 * hv
    w_lo = jnp.pad(w_in[:, n_main:], ((0, 0), (0, LANE - 2 * GLA_RANK))).astype(BF16)
    qkvg, lo = inproj(h, p["norm1_g"][i], mods, i, 0, [w_in[:, :n_main].astype(BF16), w_lo], n_ctx)
    wa = p["gla_w_alpha"][j]
    wa_pad = jnp.stack([jnp.pad(wa[0], ((0, LANE - GLA_RANK), (0, 0))),
                        jnp.pad(wa[1], ((GLA_RANK, LANE - 2 * GLA_RANK), (0, 0)))])
    ba = p["gla_b_alpha"][j].reshape(2, 1, hk)
    o_f, o_b = gla_scan(qkvg, lo, wa_pad, ba, n_ctx)
    tile = lambda: pl.BlockSpec((1, TM, hv), lambda b, t: (b, t, 0))
    specs = [tile(), tile(), pl.BlockSpec((1, TM, hv), lambda b, t: (b, t, 2)),
             pl.BlockSpec((1, GLA_DV), lambda b, t: (0, 0))]
    args = [o_f, o_b, qkvg, p["gla_norm_g"][j].reshape(1, GLA_DV)]
    return finish("gla", args, specs, p["gla_w_out"][j].astype(BF16), h, p["norm2_g"][i], mods, i, n_ctx)


def _diff_layer(h, mods, i, j, p, n_ctx, rope_tabs):
    wd = DIFF_HEADS * 2 * DIFF_HD
    cos_t, sin_t = rope_tabs
    lt = h.shape[1]
    qkv, = inproj(h, p["norm1_g"][i], mods, i, 0, [p["diff_w_in"][j].astype(BF16)], n_ctx,
                  out_dtype=BF16, rope=(cos_t, sin_t, 2 * wd, wd, DIFF_HD ** -0.5 * math.log2(math.e)))
    lam_init = 0.8 - 0.6 * math.exp(-0.3 * i)
    o = diff_attention(qkv, p["diff_lambda"][j], n_ctx, lam_init)
    specs = [pl.BlockSpec((1, TM, wd), lambda b, t: (b, t, 0)),
             pl.BlockSpec((1, 2 * DIFF_HD), lambda b, t: (0, 0))]
    args = [o, p["diff_norm_g"][j].reshape(1, 2 * DIFF_HD)]
    return finish("diff", args, specs, p["diff_w_out"][j].astype(BF16), h, p["norm2_g"][i], mods, i, n_ctx,
                  prm={"lam_init": lam_init})


def _ssd_layer(h, mods, i, j, p, n_ctx):
    din = SSD_HEADS * SSD_P
    gn = SSD_G * SSD_N
    w_in = p["ssd_w_in"][j]
    g1 = p["norm1_g"][i]
    w_dt = w_in[:, 2 * din + 2 * gn:].reshape(-1, 2, SSD_HEADS)
    w_dt = jnp.pad(w_dt, ((0, 0), (0, 0), (0, LANE - SSD_HEADS))).reshape(-1, 2 * LANE)
    z, xbc, dt = inproj(h, g1, mods, i, 0, [w_in[:, :din].astype(BF16),
                                              w_in[:, din:2 * din + 2 * gn].astype(BF16), w_dt.astype(BF16)], n_ctx)
    pad_h = lambda a: jnp.pad(a, ((0, 0), (0, LANE - SSD_HEADS))).reshape(1, -1)
    dtb, alog = pad_h(p["ssd_dt_bias"][j]), pad_h(p["ssd_a_log"][j])
    xc = ssd_conv(xbc, p["ssd_conv_w"][j], p["ssd_conv_b"][j], n_ctx)
    head_of_col = jnp.arange(SSD_G)[:, None, None] * SSD_R + jnp.arange(SSD_R * SSD_P)[None, None, :] // SSD_P
    expand = (jnp.arange(LANE)[None, :, None] == head_of_col).astype(BF16)
    y_f, y_b = ssd_scan(xc, dt, dtb, alog, expand, n_ctx)
    tile = lambda: pl.BlockSpec((1, TM, din), lambda b, t: (b, t, 0))
    row = lambda: pl.BlockSpec((1, din), lambda b, t: (0, 0))
    specs = [tile(), tile(), tile(), tile(), row(), row()]
    dexp = jnp.repeat(p["ssd_d"][j], SSD_P).reshape(1, din)
    args = [y_f, y_b, xc, z, dexp, p["ssd_norm_g"][j].reshape(1, din)]
    return finish("ssd", args, specs, p["ssd_w_out"][j].astype(BF16), h, p["norm2_g"][i], mods, i, n_ctx)


def kernel(x, c, ctx, c_ctx, norm1_g, norm2_g, w_mod, b_mod, peer_wq, peer_subkeys, peer_u, peer_v, gla_w_in, gla_w_alpha, gla_b_alpha, gla_norm_g, gla_w_out, diff_w_in, diff_lambda, diff_norm_g, diff_w_out, ssd_w_in, ssd_conv_w, ssd_conv_b, ssd_dt_bias, ssd_a_log, ssd_d, ssd_norm_g, ssd_w_out, final_g):
    p = dict(norm1_g=norm1_g, norm2_g=norm2_g, gla_w_in=gla_w_in, gla_w_alpha=gla_w_alpha, gla_b_alpha=gla_b_alpha,
             gla_norm_g=gla_norm_g, gla_w_out=gla_w_out, diff_w_in=diff_w_in, diff_lambda=diff_lambda,
             diff_norm_g=diff_norm_g, diff_w_out=diff_w_out, ssd_w_in=ssd_w_in, ssd_conv_w=ssd_conv_w,
             ssd_conv_b=ssd_conv_b, ssd_dt_bias=ssd_dt_bias, ssd_a_log=ssd_a_log, ssd_d=ssd_d,
             ssd_norm_g=ssd_norm_g, ssd_w_out=ssd_w_out)
    bsz, l, d = x.shape
    n_ctx = ctx.shape[1]
    depth = w_mod.shape[0]
    assert bsz <= 4 and n_ctx % TM == 0 and l % TM == 0 and l % GRID_W == 0
    cond8 = jnp.concatenate([c, jnp.zeros((4 - bsz, d), F32), c_ctx[None], jnp.zeros((3, d), F32)], axis=0)
    mods = mod_table(cond8, w_mod, b_mod)
    rope_tabs = _rope_tables(l, n_ctx)
    h = jnp.concatenate([ctx, x], axis=1)
    for i in range(depth):
        kind, j = i % N_MIXERS, i // N_MIXERS
        if kind == 0:
            h, f = _gla_layer(h, mods, i, j, p, n_ctx)
        elif kind == 1:
            h, f = _diff_layer(h, mods, i, j, p, n_ctx, rope_tabs)
        else:
            h, f = _ssd_layer(h, mods, i, j, p, n_ctx)
        h = peer_ffn(f, h, mods, i, peer_wq[i], peer_subkeys[i], peer_u[i], peer_v[i], n_ctx)
    return final_norm(h, final_g, n_ctx)
```

```python
import functools
import math

import jax
import jax.numpy as jnp
from jax import lax
from jax.experimental import pallas as pl
from jax.experimental.pallas import tpu as pltpu

F32 = jnp.float32
BF16 = jnp.bfloat16
I32 = jnp.int32

EPS = 1e-6
GRID_W = 64
ROPE_THETA = 10000.0
N_MIXERS = 3

GLA_HEADS, GLA_DK, GLA_DV, GLA_RANK, GLA_TAU = 4, 128, 256, 16, 16.0
GLA_CHUNK = 128
DIFF_HEADS, DIFF_HD = 8, 64
DIFF_HPS = 2
SSD_HEADS, SSD_P, SSD_N, SSD_G, SSD_CONV, SSD_CHUNK = 32, 64, 128, 4, 5, 128
SSD_R = SSD_HEADS // SSD_G
PEER_KEYS, PEER_HEADS, PEER_TOPK = 128, 8, 16

TM = 256
LANE = 128
ROUTE_TM = 1024
PEER_TM = 512
PEER_TE = 2048
G_PITCH = 72
VMEM_LIMIT = 56 * 1024 * 1024


def _cp(sem, vmem=VMEM_LIMIT):
    return pltpu.CompilerParams(dimension_semantics=sem, vmem_limit_bytes=vmem)


def _dot(a, b):
    return jnp.dot(a, b, preferred_element_type=F32)


def _dot_nt(a, b):
    return lax.dot_general(a, b, (((1,), (1,)), ((), ())), preferred_element_type=F32)


def _split2(x):
    hi = x.astype(BF16)
    lo = (x - hi.astype(F32)).astype(BF16)
    return hi, lo


def _dot3(a, b, dot=_dot):
    ah, al = _split2(a)
    bh, bl = _split2(b)
    return dot(ah, bh) + dot(ah, bl) + dot(al, bh)


def _dot_exact_rhs(w01, x):
    h1 = x.astype(BF16)
    r1 = x - h1.astype(F32)
    h2 = r1.astype(BF16)
    h3 = (r1 - h2.astype(F32)).astype(BF16)
    return _dot(w01, h1) + _dot(w01, h2) + _dot(w01, h3)


def _dot_01(x, w01):
    xh, xl = _split2(x)
    return _dot(xh, w01) + _dot(xl, w01)


def _silu(x):
    return x * (1.0 / (1.0 + jnp.exp(-x)))


def _softplus(x):
    return jnp.maximum(x, 0.0) + jnp.log1p(jnp.exp(-jnp.abs(x)))


def _rms(x, g):
    return x * lax.rsqrt(jnp.mean(x * x, axis=-1, keepdims=True) + EPS) * g


def _mod_row(b, j, n_ctx_tiles):
    return jnp.where(j < n_ctx_tiles, 4, b)


def _mod_kernel(cond_ref, w_ref, b_ref, o_ref):
    c = _silu(cond_ref[...])
    o_ref[0] = _dot3(c, w_ref[0]) + b_ref[0]


def mod_table(cond8, w_mod, b_mod):
    depth, d, n = w_mod.shape
    tn = 1024
    return pl.pallas_call(
        _mod_kernel,
        grid=(depth, n // tn),
        in_specs=[pl.BlockSpec((8, d), lambda i, j: (0, 0)),
                  pl.BlockSpec((1, d, tn), lambda i, j: (i, 0, j)),
                  pl.BlockSpec((1, 1, tn), lambda i, j: (i, 0, j))],
        out_specs=pl.BlockSpec((1, 8, tn), lambda i, j: (i, 0, j)),
        out_shape=jax.ShapeDtypeStruct((depth, 8, n), F32),
        compiler_params=_cp(("parallel", "parallel")),
        name="mod_table",
    )(cond8, w_mod, b_mod.reshape(depth, 1, n))


INPROJ_COLS = 1024


def _inproj_kernel(h_ref, g_ref, sh_ref, sc_ref, *rest, n_w, n_ctx_tiles, rope_cols, q_cols, q_scale):
    w_refs, rest = rest[:n_w], rest[n_w:]
    if rope_cols:
        cos_ref, sin_ref = rest[:2]
        rest = rest[2:]
    o_refs = rest
    b, j = pl.program_id(0), pl.program_id(1)
    row = _mod_row(b, j, n_ctx_tiles)
    a = _rms(h_ref[0], g_ref[...]) * (1.0 + sc_ref[0, pl.ds(row, 1), :]) + sh_ref[0, pl.ds(row, 1), :]
    a = a.astype(BF16)
    for k, (w_ref, o_ref) in enumerate(zip(w_refs, o_refs)):
        n = w_ref.shape[1]
        for c0 in range(0, n, INPROJ_COLS):
            c1 = min(n, c0 + INPROJ_COLS)
            y = _dot(a, w_ref[:, c0:c1])
            if k == 0 and c0 < rope_cols:
                tn = c1 - c0
                lane = lax.broadcasted_iota(I32, y.shape, 1)
                first = (lane & (DIFF_HD - 1)) < (DIFF_HD // 2)
                part = jnp.where(first, pltpu.roll(y, tn - DIFF_HD // 2, axis=1),
                                 pltpu.roll(y, DIFF_HD // 2, axis=1))
                cs = jnp.concatenate([cos_ref[0]] * (tn // LANE), axis=1)
                sn = jnp.concatenate([sin_ref[0]] * (tn // LANE), axis=1)
                y = y * cs + part * sn
                if c0 < q_cols:
                    y = y * q_scale
            o_ref[0, :, c0:c1] = y.astype(o_ref.dtype)


def inproj(h, g, mods, layer, k_shift, ws, n_ctx, out_dtype=F32, rope=None):
    bsz, lt, d = h.shape
    nt = lt // TM
    kern = functools.partial(_inproj_kernel, n_w=len(ws), n_ctx_tiles=n_ctx // TM, rope_cols=rope[2] if rope else 0,
                             q_cols=rope[3] if rope else 0, q_scale=rope[4] if rope else 1.0)
    in_specs = [pl.BlockSpec((1, TM, d), lambda b, j: (b, j, 0)),
                pl.BlockSpec((1, d), lambda b, j: (0, 0)),
                pl.BlockSpec((1, 8, d), lambda b, j: (layer, 0, k_shift)),
                pl.BlockSpec((1, 8, d), lambda b, j: (layer, 0, k_shift + 1))]
    in_specs += [pl.BlockSpec(w.shape, lambda b, j: (0, 0)) for w in ws]
    args = [h, g.reshape(1, d), mods, mods, *ws]
    out_specs = [pl.BlockSpec((1, TM, w.shape[1]), lambda b, j: (b, j, 0)) for w in ws]
    out_shape = [jax.ShapeDtypeStruct((bsz, lt, w.shape[1]), out_dtype) for w in ws]
    if rope:
        in_specs += [pl.BlockSpec((1, TM, LANE), lambda b, j: (0, j, 0))] * 2
        args += [rope[0][None], rope[1][None]]
    return pl.pallas_call(
        kern,
        grid=(bsz, nt),
        in_specs=in_specs,
        out_specs=out_specs,
        out_shape=out_shape,
        compiler_params=_cp(("parallel", "parallel")),
        name="inproj",
    )(*args)


def _post_gla(refs, prm):
    o_f, o_b, gate, ng = refs
    o = o_f[0] + o_b[0]
    g = gate[0]
    outs = []
    for hd in range(GLA_HEADS):
        sl = slice(hd * GLA_DV, (hd + 1) * GLA_DV)
        outs.append(_rms(o[:, sl], ng[...]) * _silu(g[:, sl]))
    return jnp.concatenate(outs, axis=1)


def _post_diff(refs, prm):
    o, ng = refs
    x = o[0]
    outs = []
    for hd in range(DIFF_HEADS):
        sl = slice(hd * 2 * DIFF_HD, (hd + 1) * 2 * DIFF_HD)
        outs.append(_rms(x[:, sl], ng[...]) * (1.0 - prm["lam_init"]))
    return jnp.concatenate(outs, axis=1)


def _post_ssd(refs, prm):
    y_f, y_b, xs, z, dexp, ng = refs
    y = (y_f[0] + y_b[0] + dexp[...] * xs[0]) * _silu(z[0])
    gs = y.shape[1] // SSD_G
    outs = []
    for gi in range(SSD_G):
        sl = slice(gi * gs, (gi + 1) * gs)
        outs.append(_rms(y[:, sl], ng[:, sl]))
    return jnp.concatenate(outs, axis=1)


_POST = {"gla": (_post_gla, 4), "diff": (_post_diff, 2), "ssd": (_post_ssd, 6)}


def _finish_kernel(*refs, kind, prm, n_ctx_tiles):
    post, n_in = _POST[kind]
    mix = refs[:n_in]
    w_ref, h_ref, gm_ref, g2_ref, sh_ref, sc_ref, hn_ref, f_ref = refs[n_in:]
    b, j = pl.program_id(0), pl.program_id(1)
    row = _mod_row(b, j, n_ctx_tiles)
    y = post(mix, prm)
    o = _dot(y.astype(BF16), w_ref[...])
    hn = h_ref[0] + gm_ref[0, pl.ds(row, 1), :] * o
    hn_ref[0] = hn
    f_ref[0] = _rms(hn, g2_ref[...]) * (1.0 + sc_ref[0, pl.ds(row, 1), :]) + sh_ref[0, pl.ds(row, 1), :]


def finish(kind, mix_args, mix_specs, w_out, h, g2, mods, layer, n_ctx, prm=None):
    bsz, lt, d = h.shape
    nt = lt // TM
    dm = w_out.shape[0]
    tile = lambda: pl.BlockSpec((1, TM, d), lambda b, j: (b, j, 0))
    modspec = lambda k: pl.BlockSpec((1, 8, d), lambda b, j: (layer, 0, k))
    kern = functools.partial(_finish_kernel, kind=kind, prm=prm or {}, n_ctx_tiles=n_ctx // TM)
    return pl.pallas_call(
        kern,
        grid=(bsz, nt),
        in_specs=list(mix_specs) + [pl.BlockSpec((dm, d), lambda b, j: (0, 0)), tile(), modspec(2),
                                    pl.BlockSpec((1, d), lambda b, j: (0, 0)), modspec(3), modspec(4)],
        out_specs=[tile(), tile()],
        out_shape=[jax.ShapeDtypeStruct((bsz, lt, d), F32)] * 2,
        compiler_params=_cp(("parallel", "parallel")),
        name="finish_" + kind,
    )(*mix_args, w_out, h, mods, g2.reshape(1, d), mods, mods)


def _gla_kernel(qf_ref, kf_ref, vf_ref, lof_ref, qb_ref, kb_ref, vb_ref, lob_ref, wa_ref, ba_ref,
                of_ref, ob_ref, stf_ref, stb_ref):
    @pl.when(pl.program_id(1) == 0)
    def _():
        stf_ref[...] = jnp.zeros_like(stf_ref)
        stb_ref[...] = jnp.zeros_like(stb_ref)

    _gla_chunk(qf_ref, kf_ref, vf_ref, lof_ref, wa_ref[0], ba_ref[0], of_ref, stf_ref, rev=False)
    _gla_chunk(qb_ref, kb_ref, vb_ref, lob_ref, wa_ref[1], ba_ref[1], ob_ref, stb_ref, rev=True)


def _gla_chunk(q_ref, k_ref, v_ref, lo_ref, wa, ba, o_ref, st_ref, *, rev):
    c = GLA_CHUNK
    hk = GLA_HEADS * GLA_DK
    heads = [slice(hd * GLA_DK, (hd + 1) * GLA_DK) for hd in range(GLA_HEADS)]

    q = q_ref[0] * (GLA_DK ** -0.5)
    k = k_ref[0]
    v = v_ref[0]
    z = _dot3(lo_ref[0], wa) + ba
    la = (jnp.minimum(z, 0.0) - jnp.log1p(jnp.exp(-jnp.abs(z)))) * (1.0 / GLA_TAU)

    ri = lax.broadcasted_iota(I32, (c, c), 0)
    ci = lax.broadcasted_iota(I32, (c, c), 1)
    row = lax.broadcasted_iota(I32, (c, hk), 0)
    tri = jnp.where((ci >= ri) if rev else (ci <= ri), 1.0, 0.0).astype(BF16)
    bsum = _dot_exact_rhs(tri, la)
    tot = bsum[0:1, :] if rev else bsum[c - 1:c, :]

    q16, k16 = q.astype(BF16), k.astype(BF16)
    eye = ci == ri
    a = [jnp.where(eye, _dot_nt(q16[:, sl], k16[:, sl]), 0.0) for sl in heads]
    s = c // 2
    while s >= 1:
        pos = row & (2 * s - 1)
        q_half = (pos < s) if rev else (pos >= s)
        if s >= 4:
            blk = bsum.reshape(c // (2 * s), 2 * s, hk)
            rr = s if rev else s - 1
            ref = jnp.broadcast_to(blk[:, rr:rr + 1, :], blk.shape).reshape(c, hk)
            e = jnp.where(q_half, bsum - ref, ref - bsum)
        elif s == 2:
            pos4 = row & 3
            nxt, prv = pltpu.roll(la, c - 1, axis=0), pltpu.roll(la, 1, axis=0)
            if rev:
                e = jnp.where(pos4 == 0, la + nxt, jnp.where(pos4 == 1, la, jnp.where(pos4 == 2, 0.0, prv)))
            else:
                e = jnp.where(pos4 == 3, la + prv, jnp.where(pos4 == 2, la, jnp.where(pos4 == 1, 0.0, nxt)))
        else:
            e = jnp.where(q_half, la, 0.0)
        f = jnp.exp(e)
        qd = jnp.where(q_half, q * f, 0.0).astype(BF16)
        kd = jnp.where(q_half, 0.0, k * f).astype(BF16)
        same_block = (ri & -(2 * s)) == (ci & -(2 * s))
        a = [a[hd] + jnp.where(same_block, _dot_nt(qd[:, sl], kd[:, sl]), 0.0) for hd, sl in enumerate(heads)]
        s //= 2

    qe = (q * jnp.exp(bsum)).astype(BF16)
    kt = (k * jnp.exp(tot - bsum)).astype(BF16)
    et = jnp.exp(tot)
    for hd, sl in enumerate(heads):
        vh = v[:, hd * GLA_DV:(hd + 1) * GLA_DV]
        st = st_ref[hd]
        o_ref[0, :, hd * GLA_DV:(hd + 1) * GLA_DV] = (_dot(a[hd].astype(BF16), vh.astype(BF16))
                                                       + _dot_nt(qe[:, sl], st.astype(BF16)))
        st_ref[hd] = st * et[:, sl] + _dot(vh.T.astype(BF16), kt[:, sl])


def _scan_chunk(c, n_ctx_chunks, n_chunks, rev):
    if not rev:
        return c
    return jnp.where(c < n_ctx_chunks, n_ctx_chunks - 1 - c, n_chunks - 1 - (c - n_ctx_chunks))


def gla_scan(qkvg, lo, wa_pad, ba, n_ctx):
    bsz, lt, _ = qkvg.shape
    c = GLA_CHUNK
    nch, ncc = lt // c, n_ctx // c
    hk, hv = GLA_HEADS * GLA_DK, GLA_HEADS * GLA_DV

    def chunk_specs(rev):
        cm = functools.partial(_scan_chunk, n_ctx_chunks=ncc, n_chunks=nch, rev=rev)
        ins = [pl.BlockSpec((1, c, hk), lambda b, i: (b, cm(i), 0)),
               pl.BlockSpec((1, c, hk), lambda b, i: (b, cm(i), 1)),
               pl.BlockSpec((1, c, hv), lambda b, i: (b, cm(i), 1)),
               pl.BlockSpec((1, c, LANE), lambda b, i: (b, cm(i), 0))]
        return ins, pl.BlockSpec((1, c, hv), lambda b, i: (b, cm(i), 0))

    (in_f, out_f), (in_b, out_b) = chunk_specs(False), chunk_specs(True)
    state = pltpu.VMEM((GLA_HEADS, GLA_DV, GLA_DK), F32)
    return pl.pallas_call(
        _gla_kernel,
        grid=(bsz, nch),
        in_specs=in_f + in_b + [pl.BlockSpec((2, LANE, hk), lambda b, i: (0, 0, 0)),
                                pl.BlockSpec((2, 1, hk), lambda b, i: (0, 0, 0))],
        out_specs=[out_f, out_b],
        out_shape=[jax.ShapeDtypeStruct((bsz, lt, hv), F32)] * 2,
        scratch_shapes=[state, state],
        compiler_params=_cp(("parallel", "arbitrary")),
        name="gla_scan",
    )(qkvg, qkvg, qkvg, lo, qkvg, qkvg, qkvg, lo, wa_pad, ba)


def _diff_kernel(lam_ref, q_ref, k_ref, v_ref, o_ref, *, n_ctx, lam_init):
    j = pl.program_id(2)
    lv = lam_ref[...]
    l1 = jnp.sum(lv[0:1] * lv[1:2], axis=1, keepdims=True)
    l2 = jnp.sum(lv[2:3] * lv[3:4], axis=1, keepdims=True)
    lam = jnp.exp(l1) - jnp.exp(l2) + lam_init

    lane = lax.broadcasted_iota(I32, (TM, LANE), 1)

    def attend(n_keys):
        for hh in range(DIFF_HPS):
            cols = slice(hh * LANE, (hh + 1) * LANE)
            q = q_ref[0, :, cols]
            zero = jnp.zeros_like(q)
            q0 = jnp.where(lane < DIFF_HD, q, zero)
            q1 = jnp.where(lane < DIFF_HD, zero, q)
            k = k_ref[0, :n_keys, cols]
            v = v_ref[0, :n_keys, cols]
            s0 = _dot_nt(q0, k)
            s1 = _dot_nt(q1, k)
            p0 = jnp.exp2(s0 - jnp.max(s0, axis=1, keepdims=True))
            p1 = jnp.exp2(s1 - jnp.max(s1, axis=1, keepdims=True))
            r0 = 1.0 / jnp.sum(p0, axis=1, keepdims=True)
            r1 = lam / jnp.sum(p1, axis=1, keepdims=True)
            o_ref[0, :, cols] = _dot(p0.astype(BF16), v) * r0 - _dot(p1.astype(BF16), v) * r1

    @pl.when(j * TM < n_ctx)
    def _():
        attend(n_ctx)

    @pl.when(j * TM >= n_ctx)
    def _():
        attend(k_ref.shape[1])


def diff_attention(qkv, lam_vecs, n_ctx, lam_init):
    bsz, lt, _ = qkv.shape
    nh = DIFF_HEADS // DIFF_HPS
    w = DIFF_HPS * LANE
    return pl.pallas_call(
        functools.partial(_diff_kernel, n_ctx=n_ctx, lam_init=lam_init),
        grid=(bsz, nh, lt // TM),
        in_specs=[pl.BlockSpec((4, DIFF_HD), lambda b, h, j: (0, 0)),
                  pl.BlockSpec((1, TM, w), lambda b, h, j: (b, j, h)),
                  pl.BlockSpec((1, lt, w), lambda b, h, j: (b, 0, nh + h)),
                  pl.BlockSpec((1, lt, w), lambda b, h, j: (b, 0, 2 * nh + h))],
        out_specs=pl.BlockSpec((1, TM, w), lambda b, h, j: (b, j, h)),
        out_shape=jax.ShapeDtypeStruct((bsz, lt, DIFF_HEADS * LANE), F32),
        compiler_params=_cp(("parallel", "parallel", "arbitrary")),
        name="diff_attention",
    )(lam_vecs, qkv, qkv, qkv)


def _conv_kernel(x_ref, p_ref, n_ref, w_ref, b_ref, o_ref, *, n_ctx_tiles, n_tiles):
    j = pl.program_id(1)
    first = jnp.logical_or(j == 0, j == n_ctx_tiles)
    last = jnp.logical_or(j == n_ctx_tiles - 1, j == n_tiles - 1)
    x = x_ref[0]
    prev = jnp.where(first, 0.0, p_ref[0])
    nxt = jnp.where(last, 0.0, n_ref[0])
    xe = jnp.concatenate([prev, x, nxt], axis=0)
    ne = xe.shape[0]
    half = SSD_CONV // 2
    y = b_ref[...] + w_ref[half:half + 1, :] * x
    for t in range(SSD_CONV):
        if t == half:
            continue
        sh = pltpu.roll(xe, (half - t) % ne, axis=0)[8:8 + TM]
        y = y + w_ref[t:t + 1, :] * sh
    o_ref[0] = _silu(y)


def ssd_conv(xbc, conv_w, conv_b, n_ctx):
    bsz, lt, ch = xbc.shape
    nt = lt // TM
    r8 = TM // 8
    return pl.pallas_call(
        functools.partial(_conv_kernel, n_ctx_tiles=n_ctx // TM, n_tiles=nt),
        grid=(bsz, nt),
        in_specs=[pl.BlockSpec((1, TM, ch), lambda b, j: (b, j, 0)),
                  pl.BlockSpec((1, 8, ch), lambda b, j: (b, jnp.maximum(j * r8 - 1, 0), 0)),
                  pl.BlockSpec((1, 8, ch), lambda b, j: (b, jnp.minimum((j + 1) * r8, lt // 8 - 1), 0)),
                  pl.BlockSpec((SSD_CONV, ch), lambda b, j: (0, 0)),
                  pl.BlockSpec((1, ch), lambda b, j: (0, 0))],
        out_specs=pl.BlockSpec((1, TM, ch), lambda b, j: (b, j, 0)),
        out_shape=jax.ShapeDtypeStruct((bsz, lt, ch), F32),
        compiler_params=_cp(("parallel", "parallel")),
        name="ssd_conv",
    )(xbc, xbc, xbc, conv_w, conv_b.reshape(1, ch))


def _ssd_kernel(xsf_ref, bmf_ref, cmf_ref, dtf_ref, xsb_ref, bmb_ref, cmb_ref, dtb_ref, bias_ref, al_ref, ex_ref,
                yf_ref, yb_ref, stf_ref, stb_ref):
    @pl.when(pl.program_id(1) == 0)
    def _():
        stf_ref[...] = jnp.zeros_like(stf_ref)
        stb_ref[...] = jnp.zeros_like(stb_ref)

    _ssd_chunk(xsf_ref, bmf_ref, cmf_ref, dtf_ref, bias_ref[:, :LANE], al_ref[:, :LANE], ex_ref, yf_ref, stf_ref,
               rev=False)
    _ssd_chunk(xsb_ref, bmb_ref, cmb_ref, dtb_ref, bias_ref[:, LANE:], al_ref[:, LANE:], ex_ref, yb_ref, stb_ref,
               rev=True)


def _ssd_chunk(xs_ref, bm_ref, cm_ref, dt_ref, dt_bias, a_log, ex_ref, y_ref, st_ref, *, rev):
    qn = SSD_CHUNK
    gw = SSD_R * SSD_P

    dt = _softplus(dt_ref[0] + dt_bias)
    da = dt * (-jnp.exp(a_log))
    ri = lax.broadcasted_iota(I32, (qn, qn), 0)
    ci = lax.broadcasted_iota(I32, (qn, qn), 1)
    causal = (ci >= ri) if rev else (ci <= ri)
    tri = jnp.where(causal, 1.0, 0.0).astype(BF16)
    acum = _dot_exact_rhs(tri, da)
    acum_t = acum.T
    dt_t = dt.T
    tot = acum[0:1, :] if rev else acum[qn - 1:qn, :]
    lane = lax.broadcasted_iota(I32, (qn, 2 * SSD_P), 1)
    e_acum = jnp.exp(acum)
    w_state = jnp.exp(tot - acum) * dt
    e_tot8 = jnp.broadcast_to(jnp.exp(tot), (8, LANE))

    for g in range(SSD_G):
        xs = xs_ref[0, :, g * gw:(g + 1) * gw]
        bm = bm_ref[0, :, g * SSD_N:(g + 1) * SSD_N]
        cmat = cm_ref[0, :, g * SSD_N:(g + 1) * SSD_N]
        ex = ex_ref[g]

        cb = _dot_nt(cmat.astype(BF16), bm.astype(BF16))
        xs16 = xs.astype(BF16)
        pieces = []
        for rp in range(SSD_R // 2):
            acc = None
            for sub in range(2):
                r = g * SSD_R + 2 * rp + sub
                seg = acum[:, r:r + 1] - acum_t[r:r + 1, :]
                w = cb * jnp.exp(jnp.minimum(seg, 0.0)) * dt_t[r:r + 1, :]
                w = jnp.where(causal, w, 0.0).astype(BF16)
                xpair = xs16[:, rp * 2 * SSD_P:(rp + 1) * 2 * SSD_P]
                keep = (lane < SSD_P) if sub == 0 else (lane >= SSD_P)
                part = _dot(w, jnp.where(keep, xpair, jnp.zeros_like(xpair)))
                acc = part if acc is None else acc + part
            pieces.append(acc)
        y = jnp.concatenate(pieces, axis=1)

        st = st_ref[g]
        e_i = _dot_01(e_acum, ex)
        y_ref[0, :, g * gw:(g + 1) * gw] = y + _dot(cmat.astype(BF16), st.astype(BF16)) * e_i
        wt = _dot_01(w_state, ex)
        e_tot = _dot_01(e_tot8, ex)[0:1, :]
        st_ref[g] = st * e_tot + _dot(bm.T.astype(BF16), (xs * wt).astype(BF16))


def ssd_scan(xc, dt, dtb, alog, expand, n_ctx):
    bsz, lt, _ = xc.shape
    qn = SSD_CHUNK
    nch, ncc = lt // qn, n_ctx // qn
    gw = SSD_R * SSD_P
    din, gn = SSD_HEADS * SSD_P, SSD_G * SSD_N

    def chunk_specs(rev):
        cm = functools.partial(_scan_chunk, n_ctx_chunks=ncc, n_chunks=nch, rev=rev)
        d = 1 if rev else 0
        ins = [pl.BlockSpec((1, qn, din), lambda b, i: (b, cm(i), 0)),
               pl.BlockSpec((1, qn, gn), lambda b, i: (b, cm(i), din // gn)),
               pl.BlockSpec((1, qn, gn), lambda b, i: (b, cm(i), din // gn + 1)),
               pl.BlockSpec((1, qn, LANE), lambda b, i: (b, cm(i), d))]
        return ins, pl.BlockSpec((1, qn, din), lambda b, i: (b, cm(i), 0))

    (in_f, out_f), (in_b, out_b) = chunk_specs(False), chunk_specs(True)
    state = pltpu.VMEM((SSD_G, SSD_N, gw), F32)
    return pl.pallas_call(
        _ssd_kernel,
        grid=(bsz, nch),
        in_specs=in_f + in_b + [pl.BlockSpec((1, 2 * LANE), lambda b, i: (0, 0)),
                                pl.BlockSpec((1, 2 * LANE), lambda b, i: (0, 0)),
                                pl.BlockSpec((SSD_G, LANE, gw), lambda b, i: (0, 0, 0))],
        out_specs=[out_f, out_b],
        out_shape=[jax.ShapeDtypeStruct((bsz, lt, din), F32)] * 2,
        scratch_shapes=[state, state],
        compiler_params=_cp(("parallel", "arbitrary")),
        name="ssd_scan",
    )(xc, xc, xc, dt, xc, xc, xc, dt, dtb, alog, expand)


SUBL = 8


def _vrow(ref, k):
    return ref[SUBL * k:SUBL * (k + 1), :]


def _top16(tasks):
    def body(r, carry):
        out = pl.ds(pl.multiple_of(r * SUBL, SUBL), SUBL)
        for s_ref, ids, vals_ref, idx_ref in tasks:
            rows = [_vrow(s_ref, k) for k in range(len(ids))]
            level = list(zip(rows, ids))
            while len(level) > 1:
                nxt = []
                for j in range(0, len(level) - 1, 2):
                    (va, ia), (vb, ib) = level[j], level[j + 1]
                    gt = vb > va
                    nxt.append((jnp.where(gt, vb, va), jnp.where(gt, ib, ia)))
                if len(level) % 2:
                    nxt.append(level[-1])
                level = nxt
            m, sel = level[0]
            for k, v in enumerate(rows):
                s_ref[SUBL * k:SUBL * (k + 1), :] = jnp.where(sel == ids[k], -jnp.inf, v)
            vals_ref[out, :] = m
            idx_ref[out, :] = sel
        return carry
    lax.fori_loop(0, PEER_TOPK, body, 0, unroll=2)


ROUTE_SUB = ROUTE_TM // LANE
assert ROUTE_SUB == SUBL
N_PAIRS = sum(1 for p in range(PEER_TOPK) for q in range(PEER_TOPK) if (p + 1) * (q + 1) <= PEER_TOPK)


def _peer_route_kernel(f_ref, wh_ref, wl_ref, kh_ref, kl_ref, a_ref, b_ref, g_ref,
                       q_ref, s1_ref, s2_ref, c_ref, v1_ref, i1_ref, v2_ref, i2_ref, vb_ref, ib_ref):
    h = pl.program_id(1)
    nk = PEER_KEYS
    rows_q = 256

    @pl.when(h == 0)
    def _():
        for rc in range(ROUTE_TM // rows_q):
            rs = slice(rc * rows_q, (rc + 1) * rows_q)
            fh, fl = _split2(f_ref[rs, :])
            q_ref[rs, :] = _dot(fh, wh_ref[...]) + _dot(fh, wl_ref[...]) + _dot(fl, wh_ref[...])

    for z, s_ref in enumerate((s1_ref, s2_ref)):
        col = pl.multiple_of((h * 2 + z) * nk, nk)
        kh, kl = kh_ref[z], kl_ref[z]
        for c in range(ROUTE_SUB):
            qh, ql = _split2(q_ref[c * LANE:(c + 1) * LANE, pl.ds(col, nk)])
            st = _dot_nt(kh, qh) + _dot_nt(kh, ql) + _dot_nt(kl, qh)
            s_ref[pl.ds(c, nk, stride=ROUTE_SUB), :] = st
    keys = list(range(nk))
    _top16([(s1_ref, keys, v1_ref, i1_ref), (s2_ref, keys, v2_ref, i2_ref)])

    pairs = [(p, q) for p in range(PEER_TOPK) for q in range(PEER_TOPK) if (p + 1) * (q + 1) <= PEER_TOPK]
    for j, (p, q) in enumerate(pairs):
        c_ref[SUBL * j:SUBL * (j + 1), :] = _vrow(v1_ref, p) + _vrow(v2_ref, q)
    _top16([(c_ref, [p * PEER_TOPK + q for p, q in pairs], vb_ref, ib_ref)])

    fold3 = (PEER_TOPK, ROUTE_SUB, LANE)
    sel = ib_ref[...].reshape(fold3)
    p, qq = sel >> 4, sel & (PEER_TOPK - 1)
    i1, i2 = i1_ref[...].reshape(fold3), i2_ref[...].reshape(fold3)
    a = jnp.zeros_like(sel)
    b = jnp.zeros_like(sel)
    for t in range(PEER_TOPK):
        a = jnp.where(p == t, i1[t:t + 1], a)
        b = jnp.where(qq == t, i2[t:t + 1], b)
    vb = vb_ref[...].reshape(fold3)
    e = jnp.exp(vb - vb[0:1])
    a_ref[0] = a
    b_ref[0] = b
    g_ref[0] = e / jnp.sum(e, axis=0, keepdims=True)


def peer_route(f2, wq_hi, wq_lo, sk_hi, sk_lo):
    t, d = f2.shape
    nq = wq_hi.shape[1]
    k = PEER_TOPK
    tm = ROUTE_TM
    fold = (ROUTE_SUB, LANE)
    outspec = pl.BlockSpec((1, k) + fold, lambda i, h: (h, 0, i, 0))
    oshape = (PEER_HEADS, k, t // LANE, LANE)
    a, b, g = pl.pallas_call(
        _peer_route_kernel,
        grid=(t // tm, PEER_HEADS),
        in_specs=[pl.BlockSpec((tm, d), lambda i, h: (i, 0)),
                  pl.BlockSpec((d, nq), lambda i, h: (0, 0)),
                  pl.BlockSpec((d, nq), lambda i, h: (0, 0)),
                  pl.BlockSpec(sk_hi.shape, lambda i, h: (0, 0, 0)),
                  pl.BlockSpec(sk_lo.shape, lambda i, h: (0, 0, 0))],
        out_specs=[outspec, outspec, outspec],
        out_shape=[jax.ShapeDtypeStruct(oshape, I32)] * 2 + [jax.ShapeDtypeStruct(oshape, F32)],
        scratch_shapes=[pltpu.VMEM((tm, nq), F32), pltpu.VMEM((PEER_KEYS * SUBL, LANE), F32),
                        pltpu.VMEM((PEER_KEYS * SUBL, LANE), F32), pltpu.VMEM((N_PAIRS * SUBL, LANE), F32),
                        pltpu.VMEM((k * SUBL, LANE), F32), pltpu.VMEM((k * SUBL, LANE), I32),
                        pltpu.VMEM((k * SUBL, LANE), F32), pltpu.VMEM((k * SUBL, LANE), I32),
                        pltpu.VMEM((k * SUBL, LANE), F32), pltpu.VMEM((k * SUBL, LANE), I32)],
        compiler_params=_cp(("parallel", "arbitrary")),
        name="peer_route",
    )(f2, wq_hi, wq_lo, sk_hi, sk_lo)
    return a.reshape(PEER_HEADS, k, t), b.reshape(PEER_HEADS, k, t), g.reshape(PEER_HEADS, k, t)


def _peer_dense_kernel(f_ref, h_ref, gm_ref, a_ref, b_ref, g_ref, u_ref, v_ref, o_ref,
                       f16_ref, ar_ref, br_ref, gr_ref, gs_ref, acc_ref, *, tiles_per_batch, n_ctx_tiles):
    i, e = pl.program_id(0), pl.program_id(1)
    nk = PEER_KEYS
    half = nk // 2
    hi_mask = jnp.uint32(0xFFFF0000)

    @pl.when(e == 0)
    def _():
        f16_ref[...] = f_ref[...].astype(BF16)
        acc_ref[...] = jnp.zeros_like(acc_ref)
        ar_ref[...] = a_ref[...].T
        br_ref[...] = b_ref[...].T
        gr_ref[...] = g_ref[...].T
        r = lax.broadcasted_iota(I32, (nk, nk), 0)
        key1 = jnp.where(r < half, 2 * r, 2 * (r - half) + 1)
        key2 = r

        def per_token(t, carry):
            arow = ar_ref[pl.ds(t, 1), :]
            brow = br_ref[pl.ds(t, 1), :]
            grow = 0.5 * gr_ref[pl.ds(t, 1), :]
            ga = jnp.where(key1 == arow, grow, 0.0).astype(BF16)
            ob = jnp.where(key2 == brow, 1.0, 0.0).astype(BF16)
            gm = _dot_nt(ga, ob).astype(BF16).astype(F32)
            bits = lax.bitcast_convert_type(gm, jnp.uint32)
            off = pl.multiple_of(t * G_PITCH, 8)
            gs_ref[pl.ds(off, half), :] = bits[:half] | (bits[half:] >> 16)
            return carry
        lax.fori_loop(0, PEER_TM, per_token, 0, unroll=64)

    f16 = f16_ref[...]
    sub = 2 * nk
    ws = []
    for c in range(PEER_TE // sub):
        s = _dot_nt(f16, u_ref[0, c * sub:(c + 1) * sub, :])
        word = gs_ref[pl.ds(e * (PEER_TE // sub) + c, PEER_TM, stride=G_PITCH), :]
        gt = jnp.concatenate([lax.bitcast_convert_type(word & hi_mask, F32),
                              lax.bitcast_convert_type(word << 16, F32)], axis=1)
        act = s * (1.0 + lax.erf(s * (2.0 ** -0.5)))
        ws.append((act * gt).astype(BF16))
    acc_ref[...] += _dot(jnp.concatenate(ws, axis=1), v_ref[0])

    @pl.when(e == pl.num_programs(1) - 1)
    def _():
        for part in range(PEER_TM // TM):
            j = i * (PEER_TM // TM) + part
            row = _mod_row(j // tiles_per_batch, j % tiles_per_batch, n_ctx_tiles)
            sl = slice(part * TM, (part + 1) * TM)
            o_ref[sl, :] = h_ref[sl, :] + gm_ref[0, pl.ds(row, 1), :] * acc_ref[sl, :]


def peer_dense(f2, h2, mods, layer, a_t, b_t, g_t, u16, v16, tiles_per_batch, n_ctx):
    t, d = f2.shape
    ne = u16.shape[1]
    hk = PEER_HEADS * PEER_TOPK
    kern = functools.partial(_peer_dense_kernel, tiles_per_batch=tiles_per_batch, n_ctx_tiles=n_ctx // TM)
    tok = lambda: pl.BlockSpec((PEER_TM, d), lambda i, e: (i, 0))
    rt = lambda: pl.BlockSpec((hk, PEER_TM), lambda i, e: (0, i))
    return pl.pallas_call(
        kern,
        grid=(t // PEER_TM, ne // PEER_TE),
        in_specs=[tok(), tok(), pl.BlockSpec((1, 8, d), lambda i, e: (layer, 0, 5)), rt(), rt(), rt(),
                  pl.BlockSpec((1, PEER_TE, d), lambda i, e: (layer, e, 0)),
                  pl.BlockSpec((1, PEER_TE, d), lambda i, e: (layer, e, 0))],
        out_specs=tok(),
        out_shape=jax.ShapeDtypeStruct((t, d), F32),
        scratch_shapes=[pltpu.VMEM((PEER_TM, d), BF16), pltpu.VMEM((PEER_TM, hk), I32),
                        pltpu.VMEM((PEER_TM, hk), I32), pltpu.VMEM((PEER_TM, hk), F32),
                        pltpu.VMEM((PEER_TM * G_PITCH, PEER_KEYS), jnp.uint32), pltpu.VMEM((PEER_TM, d), F32)],
        compiler_params=_cp(("parallel", "arbitrary")),
        name="peer_dense",
    )(f2, h2, mods, a_t, b_t, g_t, u16, v16)


def peer_ffn(f, h, mods, layer, wq, subkeys, u16, v16, n_ctx):
    bsz, lt, d = h.shape
    t = bsz * lt
    f2, h2 = f.reshape(t, d), h.reshape(t, d)
    wq_hi, wq_lo = _split2(wq)
    sk_hi, sk_lo = _split2(subkeys)
    a, b, g = peer_route(f2, wq_hi, wq_lo, sk_hi, sk_lo)
    hk = PEER_HEADS * PEER_TOPK
    out = peer_dense(f2, h2, mods, layer, a.reshape(hk, t), b.reshape(hk, t), g.reshape(hk, t),
                     u16, v16, lt // TM, n_ctx)
    return out.reshape(bsz, lt, d)


def _final_kernel(h_ref, g_ref, o_ref):
    o_ref[0] = _rms(h_ref[0], g_ref[...])


def final_norm(h, g, n_ctx):
    bsz, lt, d = h.shape
    l = lt - n_ctx
    off = n_ctx // TM
    return pl.pallas_call(
        _final_kernel,
        grid=(bsz, l // TM),
        in_specs=[pl.BlockSpec((1, TM, d), lambda b, j: (b, j + off, 0)),
                  pl.BlockSpec((1, d), lambda b, j: (0, 0))],
        out_specs=pl.BlockSpec((1, TM, d), lambda b, j: (b, j, 0)),
        out_shape=jax.ShapeDtypeStruct((bsz, l, d), F32),
        compiler_params=_cp(("parallel", "parallel")),
        name="final_norm",
    )(h, g.reshape(1, d))


def _rope_tables(l, n_ctx):
    rows = l // GRID_W
    row = jnp.repeat(jnp.arange(rows), GRID_W).astype(F32)
    col = jnp.tile(jnp.arange(GRID_W), rows).astype(F32)
    n_freq = DIFF_HD // 4
    freqs = ROPE_THETA ** (-jnp.arange(n_freq, dtype=F32) / n_freq)
    ang = jnp.concatenate([row[:, None] * freqs, col[:, None] * freqs], axis=-1)
    cos, sin = jnp.cos(ang), jnp.sin(ang)
    cos = jnp.concatenate([jnp.ones((n_ctx, DIFF_HD // 2), F32), cos], axis=0)
    sin = jnp.concatenate([jnp.zeros((n_ctx, DIFF_HD // 2), F32), sin], axis=0)
    cos_t = jnp.concatenate([cos, cos, cos, cos], axis=1)
    sin_t = jnp.concatenate([-sin, sin, -sin, sin], axis=1)
    return cos_t, sin_t


def _gla_layer(h, mods, i, j, p, n_ctx):
    hk = GLA_HEADS * GLA_DK
    hv = GLA_HEADS * GLA_DV
    w_in = p["gla_w_in"][j]
    n_main = 2 * hk + 2 * hv
    w_lo = jnp.pad(w_in[:, n_main:], ((0, 0), (0, LANE - 2 * GLA_RANK))).astype(BF16)
    qkvg, lo = inproj(h, p["norm1_g"][i], mods, i, 0, [w_in[:, :n_main].astype(BF16), w_lo], n_ctx)
    wa = p["gla_w_alpha"][j]
    wa_pad = jnp.stack([jnp.pad(wa[0], ((0, LANE - GLA_RANK), (0, 0))),
                        jnp.pad(wa[1], ((GLA_RANK, LANE - 2 * GLA_RANK), (0, 0)))])
    ba = p["gla_b_alpha"][j].reshape(2, 1, hk)
    o_f, o_b = gla_scan(qkvg, lo, wa_pad, ba, n_ctx)
    tile = lambda: pl.BlockSpec((1, TM, hv), lambda b, t: (b, t, 0))
    specs = [tile(), tile(), pl.BlockSpec((1, TM, hv), lambda b, t: (b, t, 2)),
             pl.BlockSpec((1, GLA_DV), lambda b, t: (0, 0))]
    args = [o_f, o_b, qkvg, p["gla_norm_g"][j].reshape(1, GLA_DV)]
    return finish("gla", args, specs, p["gla_w_out"][j].astype(BF16), h, p["norm2_g"][i], mods, i, n_ctx)


def _diff_layer(h, mods, i, j, p, n_ctx, rope_tabs):
    wd = DIFF_HEADS * 2 * DIFF_HD
    cos_t, sin_t = rope_tabs
    lt = h.shape[1]
    qkv, = inproj(h, p["norm1_g"][i], mods, i, 0, [p["diff_w_in"][j].astype(BF16)], n_ctx,
                  out_dtype=BF16, rope=(cos_t, sin_t, 2 * wd, wd, DIFF_HD ** -0.5 * math.log2(math.e)))
    lam_init = 0.8 - 0.6 * math.exp(-0.3 * i)
    o = diff_attention(qkv, p["diff_lambda"][j], n_ctx, lam_init)
    specs = [pl.BlockSpec((1, TM, wd), lambda b, t: (b, t, 0)),
             pl.BlockSpec((1, 2 * DIFF_HD), lambda b, t: (0, 0))]
    args = [o, p["diff_norm_g"][j].reshape(1, 2 * DIFF_HD)]
    return finish("diff", args, specs, p["diff_w_out"][j].astype(BF16), h, p["norm2_g"][i], mods, i, n_ctx,
                  prm={"lam_init": lam_init})


def _ssd_layer(h, mods, i, j, p, n_ctx):
    din = SSD_HEADS * SSD_P
    gn = SSD_G * SSD_N
    w_in = p["ssd_w_in"][j]
    g1 = p["norm1_g"][i]
    w_dt = w_in[:, 2 * din + 2 * gn:].reshape(-1, 2, SSD_HEADS)
    w_dt = jnp.pad(w_dt, ((0, 0), (0, 0), (0, LANE - SSD_HEADS))).reshape(-1, 2 * LANE)
    z, xbc, dt = inproj(h, g1, mods, i, 0, [w_in[:, :din].astype(BF16),
                                              w_in[:, din:2 * din + 2 * gn].astype(BF16), w_dt.astype(BF16)], n_ctx)
    pad_h = lambda a: jnp.pad(a, ((0, 0), (0, LANE - SSD_HEADS))).reshape(1, -1)
    dtb, alog = pad_h(p["ssd_dt_bias"][j]), pad_h(p["ssd_a_log"][j])
    xc = ssd_conv(xbc, p["ssd_conv_w"][j], p["ssd_conv_b"][j], n_ctx)
    head_of_col = jnp.arange(SSD_G)[:, None, None] * SSD_R + jnp.arange(SSD_R * SSD_P)[None, None, :] // SSD_P
    expand = (jnp.arange(LANE)[None, :, None] == head_of_col).astype(BF16)
    y_f, y_b = ssd_scan(xc, dt, dtb, alog, expand, n_ctx)
    tile = lambda: pl.BlockSpec((1, TM, din), lambda b, t: (b, t, 0))
    row = lambda: pl.BlockSpec((1, din), lambda b, t: (0, 0))
    specs = [tile(), tile(), tile(), tile(), row(), row()]
    dexp = jnp.repeat(p["ssd_d"][j], SSD_P).reshape(1, din)
    args = [y_f, y_b, xc, z, dexp, p["ssd_norm_g"][j].reshape(1, din)]
    return finish("ssd", args, specs, p["ssd_w_out"][j].astype(BF16), h, p["norm2_g"][i], mods, i, n_ctx)


def kernel(x, c, ctx, c_ctx, norm1_g, norm2_g, w_mod, b_mod, peer_wq, peer_subkeys, peer_u, peer_v, gla_w_in, gla_w_alpha, gla_b_alpha, gla_norm_g, gla_w_out, diff_w_in, diff_lambda, diff_norm_g, diff_w_out, ssd_w_in, ssd_conv_w, ssd_conv_b, ssd_dt_bias, ssd_a_log, ssd_d, ssd_norm_g, ssd_w_out, final_g):
    p = dict(norm1_g=norm1_g, norm2_g=norm2_g, gla_w_in=gla_w_in, gla_w_alpha=gla_w_alpha, gla_b_alpha=gla_b_alpha,
             gla_norm_g=gla_norm_g, gla_w_out=gla_w_out, diff_w_in=diff_w_in, diff_lambda=diff_lambda,
             diff_norm_g=diff_norm_g, diff_w_out=diff_w_out, ssd_w_in=ssd_w_in, ssd_conv_w=ssd_conv_w,
             ssd_conv_b=ssd_conv_b, ssd_dt_bias=ssd_dt_bias, ssd_a_log=ssd_a_log, ssd_d=ssd_d,
             ssd_norm_g=ssd_norm_g, ssd_w_out=ssd_w_out)
    bsz, l, d = x.shape
    n_ctx = ctx.shape[1]
    depth = w_mod.shape[0]
    assert bsz <= 4 and n_ctx % TM == 0 and l % TM == 0 and l % GRID_W == 0
    cond8 = jnp.concatenate([c, jnp.zeros((4 - bsz, d), F32), c_ctx[None], jnp.zeros((3, d), F32)], axis=0)
    mods = mod_table(cond8, w_mod, b_mod)
    rope_tabs = _rope_tables(l, n_ctx)
    u16, v16 = peer_u.astype(BF16), peer_v.astype(BF16)
    h = jnp.concatenate([ctx, x], axis=1)
    for i in range(depth):
        kind, j = i % N_MIXERS, i // N_MIXERS
        if kind == 0:
            h, f = _gla_layer(h, mods, i, j, p, n_ctx)
        elif kind == 1:
            h, f = _diff_layer(h, mods, i, j, p, n_ctx, rope_tabs)
        else:
            h, f = _ssd_layer(h, mods, i, j, p, n_ctx)
        h = peer_ffn(f, h, mods, i, peer_wq[i], peer_subkeys[i], u16, v16, n_ctx)
    return final_norm(h, final_g, n_ctx)
```

```python
import functools
import math

import jax
import jax.numpy as jnp
from jax import lax
from jax.experimental import pallas as pl
from jax.experimental.pallas import tpu as pltpu

F32 = jnp.float32
BF16 = jnp.bfloat16
I32 = jnp.int32

EPS = 1e-6
GRID_W = 64
ROPE_THETA = 10000.0
N_MIXERS = 3

GLA_HEADS, GLA_DK, GLA_DV, GLA_RANK, GLA_TAU = 4, 128, 256, 16, 16.0
GLA_CHUNK = 128
DIFF_HEADS, DIFF_HD = 8, 64
DIFF_HPS = 2
SSD_HEADS, SSD_P, SSD_N, SSD_G, SSD_CONV, SSD_CHUNK = 32, 64, 128, 4, 5, 128
SSD_R = SSD_HEADS // SSD_G
PEER_KEYS, PEER_HEADS, PEER_TOPK = 128, 8, 16

TM = 256
LANE = 128
ROUTE_TM = 1024
PEER_TM = 512
PEER_TE = 2048
G_PITCH = 72
VMEM_LIMIT = 56 * 1024 * 1024


def _cp(sem, vmem=VMEM_LIMIT):
    return pltpu.CompilerParams(dimension_semantics=sem, vmem_limit_bytes=vmem)


def _dot(a, b):
    return jnp.dot(a, b, preferred_element_type=F32)


def _dot_nt(a, b):
    return lax.dot_general(a, b, (((1,), (1,)), ((), ())), preferred_element_type=F32)


def _split2(x):
    hi = x.astype(BF16)
    lo = (x - hi.astype(F32)).astype(BF16)
    return hi, lo


def _dot3(a, b, dot=_dot):
    ah, al = _split2(a)
    bh, bl = _split2(b)
    return dot(ah, bh) + dot(ah, bl) + dot(al, bh)


def _dot_exact_rhs(w01, x):
    h1 = x.astype(BF16)
    r1 = x - h1.astype(F32)
    h2 = r1.astype(BF16)
    h3 = (r1 - h2.astype(F32)).astype(BF16)
    return _dot(w01, h1) + _dot(w01, h2) + _dot(w01, h3)


def _dot_01(x, w01):
    xh, xl = _split2(x)
    return _dot(xh, w01) + _dot(xl, w01)


def _silu(x):
    return x * (1.0 / (1.0 + jnp.exp(-x)))


def _softplus(x):
    return jnp.maximum(x, 0.0) + jnp.log1p(jnp.exp(-jnp.abs(x)))


def _rms(x, g):
    return x * lax.rsqrt(jnp.mean(x * x, axis=-1, keepdims=True) + EPS) * g


def _mod_row(b, j, n_ctx_tiles):
    return jnp.where(j < n_ctx_tiles, 4, b)


def _mod_kernel(cond_ref, w_ref, b_ref, o_ref):
    c = _silu(cond_ref[...])
    o_ref[0] = _dot3(c, w_ref[0]) + b_ref[0]


def mod_table(cond8, w_mod, b_mod):
    depth, d, n = w_mod.shape
    tn = 1024
    return pl.pallas_call(
        _mod_kernel,
        grid=(depth, n // tn),
        in_specs=[pl.BlockSpec((8, d), lambda i, j: (0, 0)),
                  pl.BlockSpec((1, d, tn), lambda i, j: (i, 0, j)),
                  pl.BlockSpec((1, 1, tn), lambda i, j: (i, 0, j))],
        out_specs=pl.BlockSpec((1, 8, tn), lambda i, j: (i, 0, j)),
        out_shape=jax.ShapeDtypeStruct((depth, 8, n), F32),
        compiler_params=_cp(("parallel", "parallel")),
        name="mod_table",
    )(cond8, w_mod, b_mod.reshape(depth, 1, n))


INPROJ_COLS = 1024


def _inproj_kernel(h_ref, g_ref, sh_ref, sc_ref, *rest, n_w, n_ctx_tiles, rope_cols, q_cols, q_scale):
    w_refs, rest = rest[:n_w], rest[n_w:]
    if rope_cols:
        cos_ref, sin_ref = rest[:2]
        rest = rest[2:]
    o_refs = rest
    b, j = pl.program_id(0), pl.program_id(1)
    row = _mod_row(b, j, n_ctx_tiles)
    a = _rms(h_ref[0], g_ref[...]) * (1.0 + sc_ref[0, pl.ds(row, 1), :]) + sh_ref[0, pl.ds(row, 1), :]
    a = a.astype(BF16)
    for k, (w_ref, o_ref) in enumerate(zip(w_refs, o_refs)):
        n = w_ref.shape[1]
        for c0 in range(0, n, INPROJ_COLS):
            c1 = min(n, c0 + INPROJ_COLS)
            y = _dot(a, w_ref[:, c0:c1])
            if k == 0 and c0 < rope_cols:
                tn = c1 - c0
                lane = lax.broadcasted_iota(I32, y.shape, 1)
                first = (lane & (DIFF_HD - 1)) < (DIFF_HD // 2)
                part = jnp.where(first, pltpu.roll(y, tn - DIFF_HD // 2, axis=1),
                                 pltpu.roll(y, DIFF_HD // 2, axis=1))
                cs = jnp.concatenate([cos_ref[0]] * (tn // LANE), axis=1)
                sn = jnp.concatenate([sin_ref[0]] * (tn // LANE), axis=1)
                y = y * cs + part * sn
                if c0 < q_cols:
                    y = y * q_scale
            o_ref[0, :, c0:c1] = y.astype(o_ref.dtype)


def inproj(h, g, mods, layer, k_shift, ws, n_ctx, out_dtype=F32, rope=None):
    bsz, lt, d = h.shape
    nt = lt // TM
    kern = functools.partial(_inproj_kernel, n_w=len(ws), n_ctx_tiles=n_ctx // TM, rope_cols=rope[2] if rope else 0,
                             q_cols=rope[3] if rope else 0, q_scale=rope[4] if rope else 1.0)
    in_specs = [pl.BlockSpec((1, TM, d), lambda b, j: (b, j, 0)),
                pl.BlockSpec((1, d), lambda b, j: (0, 0)),
                pl.BlockSpec((1, 8, d), lambda b, j: (layer, 0, k_shift)),
                pl.BlockSpec((1, 8, d), lambda b, j: (layer, 0, k_shift + 1))]
    in_specs += [pl.BlockSpec(w.shape, lambda b, j: (0, 0)) for w in ws]
    args = [h, g.reshape(1, d), mods, mods, *ws]
    out_specs = [pl.BlockSpec((1, TM, w.shape[1]), lambda b, j: (b, j, 0)) for w in ws]
    out_shape = [jax.ShapeDtypeStruct((bsz, lt, w.shape[1]), out_dtype) for w in ws]
    if rope:
        in_specs += [pl.BlockSpec((1, TM, LANE), lambda b, j: (0, j, 0))] * 2
        args += [rope[0][None], rope[1][None]]
    return pl.pallas_call(
        kern,
        grid=(bsz, nt),
        in_specs=in_specs,
        out_specs=out_specs,
        out_shape=out_shape,
        compiler_params=_cp(("parallel", "parallel")),
        name="inproj",
    )(*args)


def _post_gla(refs, prm):
    o_f, o_b, gate, ng = refs
    o = o_f[0] + o_b[0]
    g = gate[0]
    outs = []
    for hd in range(GLA_HEADS):
        sl = slice(hd * GLA_DV, (hd + 1) * GLA_DV)
        outs.append(_rms(o[:, sl], ng[...]) * _silu(g[:, sl]))
    return jnp.concatenate(outs, axis=1)


def _post_diff(refs, prm):
    o, ng = refs
    x = o[0]
    outs = []
    for hd in range(DIFF_HEADS):
        sl = slice(hd * 2 * DIFF_HD, (hd + 1) * 2 * DIFF_HD)
        outs.append(_rms(x[:, sl], ng[...]) * (1.0 - prm["lam_init"]))
    return jnp.concatenate(outs, axis=1)


def _post_ssd(refs, prm):
    y_f, y_b, xs, z, dexp, ng = refs
    y = (y_f[0] + y_b[0] + dexp[...] * xs[0]) * _silu(z[0])
    gs = y.shape[1] // SSD_G
    outs = []
    for gi in range(SSD_G):
        sl = slice(gi * gs, (gi + 1) * gs)
        outs.append(_rms(y[:, sl], ng[:, sl]))
    return jnp.concatenate(outs, axis=1)


_POST = {"gla": (_post_gla, 4), "diff": (_post_diff, 2), "ssd": (_post_ssd, 6)}


def _finish_kernel(*refs, kind, prm, n_ctx_tiles):
    post, n_in = _POST[kind]
    mix = refs[:n_in]
    w_ref, h_ref, gm_ref, g2_ref, sh_ref, sc_ref, hn_ref, f_ref = refs[n_in:]
    b, j = pl.program_id(0), pl.program_id(1)
    row = _mod_row(b, j, n_ctx_tiles)
    y = post(mix, prm)
    o = _dot(y.astype(BF16), w_ref[...])
    hn = h_ref[0] + gm_ref[0, pl.ds(row, 1), :] * o
    hn_ref[0] = hn
    f_ref[0] = _rms(hn, g2_ref[...]) * (1.0 + sc_ref[0, pl.ds(row, 1), :]) + sh_ref[0, pl.ds(row, 1), :]


def finish(kind, mix_args, mix_specs, w_out, h, g2, mods, layer, n_ctx, prm=None):
    bsz, lt, d = h.shape
    nt = lt // TM
    dm = w_out.shape[0]
    tile = lambda: pl.BlockSpec((1, TM, d), lambda b, j: (b, j, 0))
    modspec = lambda k: pl.BlockSpec((1, 8, d), lambda b, j: (layer, 0, k))
    kern = functools.partial(_finish_kernel, kind=kind, prm=prm or {}, n_ctx_tiles=n_ctx // TM)
    return pl.pallas_call(
        kern,
        grid=(bsz, nt),
        in_specs=list(mix_specs) + [pl.BlockSpec((dm, d), lambda b, j: (0, 0)), tile(), modspec(2),
                                    pl.BlockSpec((1, d), lambda b, j: (0, 0)), modspec(3), modspec(4)],
        out_specs=[tile(), tile()],
        out_shape=[jax.ShapeDtypeStruct((bsz, lt, d), F32)] * 2,
        compiler_params=_cp(("parallel", "parallel")),
        name="finish_" + kind,
    )(*mix_args, w_out, h, mods, g2.reshape(1, d), mods, mods)


def _gla_kernel(qf_ref, kf_ref, vf_ref, lof_ref, qb_ref, kb_ref, vb_ref, lob_ref, wa_ref, ba_ref,
                of_ref, ob_ref, stf_ref, stb_ref):
    @pl.when(pl.program_id(1) == 0)
    def _():
        stf_ref[...] = jnp.zeros_like(stf_ref)
        stb_ref[...] = jnp.zeros_like(stb_ref)

    _gla_chunk(qf_ref, kf_ref, vf_ref, lof_ref, wa_ref[0], ba_ref[0], of_ref, stf_ref, rev=False)
    _gla_chunk(qb_ref, kb_ref, vb_ref, lob_ref, wa_ref[1], ba_ref[1], ob_ref, stb_ref, rev=True)


def _gla_chunk(q_ref, k_ref, v_ref, lo_ref, wa, ba, o_ref, st_ref, *, rev):
    c = GLA_CHUNK
    hk = GLA_HEADS * GLA_DK
    heads = [slice(hd * GLA_DK, (hd + 1) * GLA_DK) for hd in range(GLA_HEADS)]

    q = q_ref[0] * (GLA_DK ** -0.5)
    k = k_ref[0]
    v = v_ref[0]
    z = _dot3(lo_ref[0], wa) + ba
    la = (jnp.minimum(z, 0.0) - jnp.log1p(jnp.exp(-jnp.abs(z)))) * (1.0 / GLA_TAU)

    ri = lax.broadcasted_iota(I32, (c, c), 0)
    ci = lax.broadcasted_iota(I32, (c, c), 1)
    row = lax.broadcasted_iota(I32, (c, hk), 0)
    tri = jnp.where((ci >= ri) if rev else (ci <= ri), 1.0, 0.0).astype(BF16)
    bsum = _dot_exact_rhs(tri, la)
    tot = bsum[0:1, :] if rev else bsum[c - 1:c, :]

    q16, k16 = q.astype(BF16), k.astype(BF16)
    eye = ci == ri
    a = [jnp.where(eye, _dot_nt(q16[:, sl], k16[:, sl]), 0.0) for sl in heads]
    s = c // 2
    while s >= 1:
        pos = row & (2 * s - 1)
        q_half = (pos < s) if rev else (pos >= s)
        if s >= 4:
            blk = bsum.reshape(c // (2 * s), 2 * s, hk)
            rr = s if rev else s - 1
            ref = jnp.broadcast_to(blk[:, rr:rr + 1, :], blk.shape).reshape(c, hk)
            e = jnp.where(q_half, bsum - ref, ref - bsum)
        elif s == 2:
            pos4 = row & 3
            nxt, prv = pltpu.roll(la, c - 1, axis=0), pltpu.roll(la, 1, axis=0)
            if rev:
                e = jnp.where(pos4 == 0, la + nxt, jnp.where(pos4 == 1, la, jnp.where(pos4 == 2, 0.0, prv)))
            else:
                e = jnp.where(pos4 == 3, la + prv, jnp.where(pos4 == 2, la, jnp.where(pos4 == 1, 0.0, nxt)))
        else:
            e = jnp.where(q_half, la, 0.0)
        f = jnp.exp(e)
        qd = jnp.where(q_half, q * f, 0.0).astype(BF16)
        kd = jnp.where(q_half, 0.0, k * f).astype(BF16)
        same_block = (ri & -(2 * s)) == (ci & -(2 * s))
        a = [a[hd] + jnp.where(same_block, _dot_nt(qd[:, sl], kd[:, sl]), 0.0) for hd, sl in enumerate(heads)]
        s //= 2

    qe = (q * jnp.exp(bsum)).astype(BF16)
    kt = (k * jnp.exp(tot - bsum)).astype(BF16)
    et = jnp.exp(tot)
    for hd, sl in enumerate(heads):
        vh = v[:, hd * GLA_DV:(hd + 1) * GLA_DV]
        st = st_ref[hd]
        o_ref[0, :, hd * GLA_DV:(hd + 1) * GLA_DV] = (_dot(a[hd].astype(BF16), vh.astype(BF16))
                                                       + _dot_nt(qe[:, sl], st.astype(BF16)))
        st_ref[hd] = st * et[:, sl] + _dot(vh.T.astype(BF16), kt[:, sl])


def _scan_chunk(c, n_ctx_chunks, n_chunks, rev):
    if not rev:
        return c
    return jnp.where(c < n_ctx_chunks, n_ctx_chunks - 1 - c, n_chunks - 1 - (c - n_ctx_chunks))


def gla_scan(qkvg, lo, wa_pad, ba, n_ctx):
    bsz, lt, _ = qkvg.shape
    c = GLA_CHUNK
    nch, ncc = lt // c, n_ctx // c
    hk, hv = GLA_HEADS * GLA_DK, GLA_HEADS * GLA_DV

    def chunk_specs(rev):
        cm = functools.partial(_scan_chunk, n_ctx_chunks=ncc, n_chunks=nch, rev=rev)
        ins = [pl.BlockSpec((1, c, hk), lambda b, i: (b, cm(i), 0)),
               pl.BlockSpec((1, c, hk), lambda b, i: (b, cm(i), 1)),
               pl.BlockSpec((1, c, hv), lambda b, i: (b, cm(i), 1)),
               pl.BlockSpec((1, c, LANE), lambda b, i: (b, cm(i), 0))]
        return ins, pl.BlockSpec((1, c, hv), lambda b, i: (b, cm(i), 0))

    (in_f, out_f), (in_b, out_b) = chunk_specs(False), chunk_specs(True)
    state = pltpu.VMEM((GLA_HEADS, GLA_DV, GLA_DK), F32)
    return pl.pallas_call(
        _gla_kernel,
        grid=(bsz, nch),
        in_specs=in_f + in_b + [pl.BlockSpec((2, LANE, hk), lambda b, i: (0, 0, 0)),
                                pl.BlockSpec((2, 1, hk), lambda b, i: (0, 0, 0))],
        out_specs=[out_f, out_b],
        out_shape=[jax.ShapeDtypeStruct((bsz, lt, hv), F32)] * 2,
        scratch_shapes=[state, state],
        compiler_params=_cp(("parallel", "arbitrary")),
        name="gla_scan",
    )(qkvg, qkvg, qkvg, lo, qkvg, qkvg, qkvg, lo, wa_pad, ba)


def _diff_kernel(lam_ref, q_ref, k_ref, v_ref, o_ref, *, n_ctx, lam_init):
    j = pl.program_id(2)
    lv = lam_ref[...]
    l1 = jnp.sum(lv[0:1] * lv[1:2], axis=1, keepdims=True)
    l2 = jnp.sum(lv[2:3] * lv[3:4], axis=1, keepdims=True)
    lam = jnp.exp(l1) - jnp.exp(l2) + lam_init

    lane = lax.broadcasted_iota(I32, (TM, LANE), 1)

    def attend(n_keys):
        for hh in range(DIFF_HPS):
            cols = slice(hh * LANE, (hh + 1) * LANE)
            q = q_ref[0, :, cols]
            zero = jnp.zeros_like(q)
            q0 = jnp.where(lane < DIFF_HD, q, zero)
            q1 = jnp.where(lane < DIFF_HD, zero, q)
            k = k_ref[0, :n_keys, cols]
            v = v_ref[0, :n_keys, cols]
            s0 = _dot_nt(q0, k)
            s1 = _dot_nt(q1, k)
            p0 = jnp.exp2(s0 - jnp.max(s0, axis=1, keepdims=True))
            p1 = jnp.exp2(s1 - jnp.max(s1, axis=1, keepdims=True))
            r0 = 1.0 / jnp.sum(p0, axis=1, keepdims=True)
            r1 = lam / jnp.sum(p1, axis=1, keepdims=True)
            o_ref[0, :, cols] = _dot(p0.astype(BF16), v) * r0 - _dot(p1.astype(BF16), v) * r1

    @pl.when(j * TM < n_ctx)
    def _():
        attend(n_ctx)

    @pl.when(j * TM >= n_ctx)
    def _():
        attend(k_ref.shape[1])


def diff_attention(qkv, lam_vecs, n_ctx, lam_init):
    bsz, lt, _ = qkv.shape
    nh = DIFF_HEADS // DIFF_HPS
    w = DIFF_HPS * LANE
    return pl.pallas_call(
        functools.partial(_diff_kernel, n_ctx=n_ctx, lam_init=lam_init),
        grid=(bsz, nh, lt // TM),
        in_specs=[pl.BlockSpec((4, DIFF_HD), lambda b, h, j: (0, 0)),
                  pl.BlockSpec((1, TM, w), lambda b, h, j: (b, j, h)),
                  pl.BlockSpec((1, lt, w), lambda b, h, j: (b, 0, nh + h)),
                  pl.BlockSpec((1, lt, w), lambda b, h, j: (b, 0, 2 * nh + h))],
        out_specs=pl.BlockSpec((1, TM, w), lambda b, h, j: (b, j, h)),
        out_shape=jax.ShapeDtypeStruct((bsz, lt, DIFF_HEADS * LANE), F32),
        compiler_params=_cp(("parallel", "parallel", "arbitrary")),
        name="diff_attention",
    )(lam_vecs, qkv, qkv, qkv)


def _conv_kernel(x_ref, p_ref, n_ref, w_ref, b_ref, o_ref, *, n_ctx_tiles, n_tiles):
    j = pl.program_id(1)
    first = jnp.logical_or(j == 0, j == n_ctx_tiles)
    last = jnp.logical_or(j == n_ctx_tiles - 1, j == n_tiles - 1)
    x = x_ref[0]
    prev = jnp.where(first, 0.0, p_ref[0])
    nxt = jnp.where(last, 0.0, n_ref[0])
    xe = jnp.concatenate([prev, x, nxt], axis=0)
    ne = xe.shape[0]
    half = SSD_CONV // 2
    y = b_ref[...] + w_ref[half:half + 1, :] * x
    for t in range(SSD_CONV):
        if t == half:
            continue
        sh = pltpu.roll(xe, (half - t) % ne, axis=0)[8:8 + TM]
        y = y + w_ref[t:t + 1, :] * sh
    o_ref[0] = _silu(y)


def ssd_conv(xbc, conv_w, conv_b, n_ctx):
    bsz, lt, ch = xbc.shape
    nt = lt // TM
    r8 = TM // 8
    return pl.pallas_call(
        functools.partial(_conv_kernel, n_ctx_tiles=n_ctx // TM, n_tiles=nt),
        grid=(bsz, nt),
        in_specs=[pl.BlockSpec((1, TM, ch), lambda b, j: (b, j, 0)),
                  pl.BlockSpec((1, 8, ch), lambda b, j: (b, jnp.maximum(j * r8 - 1, 0), 0)),
                  pl.BlockSpec((1, 8, ch), lambda b, j: (b, jnp.minimum((j + 1) * r8, lt // 8 - 1), 0)),
                  pl.BlockSpec((SSD_CONV, ch), lambda b, j: (0, 0)),
                  pl.BlockSpec((1, ch), lambda b, j: (0, 0))],
        out_specs=pl.BlockSpec((1, TM, ch), lambda b, j: (b, j, 0)),
        out_shape=jax.ShapeDtypeStruct((bsz, lt, ch), F32),
        compiler_params=_cp(("parallel", "parallel")),
        name="ssd_conv",
    )(xbc, xbc, xbc, conv_w, conv_b.reshape(1, ch))


def _ssd_kernel(xsf_ref, bmf_ref, cmf_ref, dtf_ref, xsb_ref, bmb_ref, cmb_ref, dtb_ref, bias_ref, al_ref, ex_ref,
                yf_ref, yb_ref, stf_ref, stb_ref):
    @pl.when(pl.program_id(1) == 0)
    def _():
        stf_ref[...] = jnp.zeros_like(stf_ref)
        stb_ref[...] = jnp.zeros_like(stb_ref)

    _ssd_chunk(xsf_ref, bmf_ref, cmf_ref, dtf_ref, bias_ref[:, :LANE], al_ref[:, :LANE], ex_ref, yf_ref, stf_ref,
               rev=False)
    _ssd_chunk(xsb_ref, bmb_ref, cmb_ref, dtb_ref, bias_ref[:, LANE:], al_ref[:, LANE:], ex_ref, yb_ref, stb_ref,
               rev=True)


def _ssd_chunk(xs_ref, bm_ref, cm_ref, dt_ref, dt_bias, a_log, ex_ref, y_ref, st_ref, *, rev):
    qn = SSD_CHUNK
    gw = SSD_R * SSD_P

    dt = _softplus(dt_ref[0] + dt_bias)
    da = dt * (-jnp.exp(a_log))
    ri = lax.broadcasted_iota(I32, (qn, qn), 0)
    ci = lax.broadcasted_iota(I32, (qn, qn), 1)
    causal = (ci >= ri) if rev else (ci <= ri)
    tri = jnp.where(causal, 1.0, 0.0).astype(BF16)
    acum = _dot_exact_rhs(tri, da)
    acum_t = acum.T
    dt_t = dt.T
    tot = acum[0:1, :] if rev else acum[qn - 1:qn, :]
    lane = lax.broadcasted_iota(I32, (qn, 2 * SSD_P), 1)
    e_acum = jnp.exp(acum)
    w_state = jnp.exp(tot - acum) * dt
    e_tot8 = jnp.broadcast_to(jnp.exp(tot), (8, LANE))

    for g in range(SSD_G):
        xs = xs_ref[0, :, g * gw:(g + 1) * gw]
        bm = bm_ref[0, :, g * SSD_N:(g + 1) * SSD_N]
        cmat = cm_ref[0, :, g * SSD_N:(g + 1) * SSD_N]
        ex = ex_ref[g]

        cb = _dot_nt(cmat.astype(BF16), bm.astype(BF16))
        xs16 = xs.astype(BF16)
        pieces = []
        for rp in range(SSD_R // 2):
            acc = None
            for sub in range(2):
                r = g * SSD_R + 2 * rp + sub
                seg = acum[:, r:r + 1] - acum_t[r:r + 1, :]
                w = cb * jnp.exp(jnp.minimum(seg, 0.0)) * dt_t[r:r + 1, :]
                w = jnp.where(causal, w, 0.0).astype(BF16)
                xpair = xs16[:, rp * 2 * SSD_P:(rp + 1) * 2 * SSD_P]
                keep = (lane < SSD_P) if sub == 0 else (lane >= SSD_P)
                part = _dot(w, jnp.where(keep, xpair, jnp.zeros_like(xpair)))
                acc = part if acc is None else acc + part
            pieces.append(acc)
        y = jnp.concatenate(pieces, axis=1)

        st = st_ref[g]
        e_i = _dot_01(e_acum, ex)
        y_ref[0, :, g * gw:(g + 1) * gw] = y + _dot(cmat.astype(BF16), st.astype(BF16)) * e_i
        wt = _dot_01(w_state, ex)
        e_tot = _dot_01(e_tot8, ex)[0:1, :]
        st_ref[g] = st * e_tot + _dot(bm.T.astype(BF16), (xs * wt).astype(BF16))


def ssd_scan(xc, dt, dtb, alog, expand, n_ctx):
    bsz, lt, _ = xc.shape
    qn = SSD_CHUNK
    nch, ncc = lt // qn, n_ctx // qn
    gw = SSD_R * SSD_P
    din, gn = SSD_HEADS * SSD_P, SSD_G * SSD_N

    def chunk_specs(rev):
        cm = functools.partial(_scan_chunk, n_ctx_chunks=ncc, n_chunks=nch, rev=rev)
        d = 1 if rev else 0
        ins = [pl.BlockSpec((1, qn, din), lambda b, i: (b, cm(i), 0)),
               pl.BlockSpec((1, qn, gn), lambda b, i: (b, cm(i), din // gn)),
               pl.BlockSpec((1, qn, gn), lambda b, i: (b, cm(i), din // gn + 1)),
               pl.BlockSpec((1, qn, LANE), lambda b, i: (b, cm(i), d))]
        return ins, pl.BlockSpec((1, qn, din), lambda b, i: (b, cm(i), 0))

    (in_f, out_f), (in_b, out_b) = chunk_specs(False), chunk_specs(True)
    state = pltpu.VMEM((SSD_G, SSD_N, gw), F32)
    return pl.pallas_call(
        _ssd_kernel,
        grid=(bsz, nch),
        in_specs=in_f + in_b + [pl.BlockSpec((1, 2 * LANE), lambda b, i: (0, 0)),
                                pl.BlockSpec((1, 2 * LANE), lambda b, i: (0, 0)),
                                pl.BlockSpec((SSD_G, LANE, gw), lambda b, i: (0, 0, 0))],
        out_specs=[out_f, out_b],
        out_shape=[jax.ShapeDtypeStruct((bsz, lt, din), F32)] * 2,
        scratch_shapes=[state, state],
        compiler_params=_cp(("parallel", "arbitrary")),
        name="ssd_scan",
    )(xc, xc, xc, dt, xc, xc, xc, dt, dtb, alog, expand)


SUBL = 8


def _vrow(ref, k):
    return ref[SUBL * k:SUBL * (k + 1), :]


def _top16(tasks):
    def body(r, carry):
        out = pl.ds(pl.multiple_of(r * SUBL, SUBL), SUBL)
        for s_ref, ids, vals_ref, idx_ref in tasks:
            rows = [_vrow(s_ref, k) for k in range(len(ids))]
            level = list(zip(rows, ids))
            while len(level) > 1:
                nxt = []
                for j in range(0, len(level) - 1, 2):
                    (va, ia), (vb, ib) = level[j], level[j + 1]
                    gt = vb > va
                    nxt.append((jnp.where(gt, vb, va), jnp.where(gt, ib, ia)))
                if len(level) % 2:
                    nxt.append(level[-1])
                level = nxt
            m, sel = level[0]
            for k, v in enumerate(rows):
                s_ref[SUBL * k:SUBL * (k + 1), :] = jnp.where(sel == ids[k], -jnp.inf, v)
            vals_ref[out, :] = m
            idx_ref[out, :] = sel
        return carry
    lax.fori_loop(0, PEER_TOPK, body, 0, unroll=2)


ROUTE_SUB = ROUTE_TM // LANE
assert ROUTE_SUB == SUBL
N_PAIRS = sum(1 for p in range(PEER_TOPK) for q in range(PEER_TOPK) if (p + 1) * (q + 1) <= PEER_TOPK)


def _peer_route_kernel(f_ref, wh_ref, wl_ref, kc_ref, a_ref, b_ref, g_ref,
                       q_ref, s1_ref, s2_ref, c_ref, v1_ref, i1_ref, v2_ref, i2_ref, vb_ref, ib_ref):
    h = pl.program_id(1)
    nk = PEER_KEYS
    rows_q = 256

    @pl.when(h == 0)
    def _():
        for rc in range(ROUTE_TM // rows_q):
            rs = slice(rc * rows_q, (rc + 1) * rows_q)
            fh, fl = _split2(f_ref[rs, :])
            q_ref[rs, :] = _dot(fh, wh_ref[...]) + _dot(fh, wl_ref[...]) + _dot(fl, wh_ref[...])

    for z, s_ref in enumerate((s1_ref, s2_ref)):
        col = pl.multiple_of((h * 2 + z) * nk, nk)
        kcat = kc_ref[z]
        for c in range(ROUTE_SUB):
            qh, ql = _split2(q_ref[c * LANE:(c + 1) * LANE, pl.ds(col, nk)])
            st = _dot_nt(kcat, jnp.concatenate([qh, ql, qh], axis=1))
            s_ref[pl.ds(c, nk, stride=ROUTE_SUB), :] = st
    keys = list(range(nk))
    _top16([(s1_ref, keys, v1_ref, i1_ref), (s2_ref, keys, v2_ref, i2_ref)])

    pairs = [(p, q) for p in range(PEER_TOPK) for q in range(PEER_TOPK) if (p + 1) * (q + 1) <= PEER_TOPK]
    for j, (p, q) in enumerate(pairs):
        c_ref[SUBL * j:SUBL * (j + 1), :] = _vrow(v1_ref, p) + _vrow(v2_ref, q)
    _top16([(c_ref, [p * PEER_TOPK + q for p, q in pairs], vb_ref, ib_ref)])

    fold3 = (PEER_TOPK, ROUTE_SUB, LANE)
    sel = ib_ref[...].reshape(fold3)
    p, qq = sel >> 4, sel & (PEER_TOPK - 1)
    i1, i2 = i1_ref[...].reshape(fold3), i2_ref[...].reshape(fold3)
    a = jnp.zeros_like(sel)
    b = jnp.zeros_like(sel)
    for t in range(PEER_TOPK):
        a = jnp.where(p == t, i1[t:t + 1], a)
        b = jnp.where(qq == t, i2[t:t + 1], b)
    vb = vb_ref[...].reshape(fold3)
    e = jnp.exp(vb - vb[0:1])
    a_ref[0] = a
    b_ref[0] = b
    g_ref[0] = e / jnp.sum(e, axis=0, keepdims=True)


def peer_route(f2, wq_hi, wq_lo, sk_cat):
    t, d = f2.shape
    nq = wq_hi.shape[1]
    k = PEER_TOPK
    tm = ROUTE_TM
    fold = (ROUTE_SUB, LANE)
    outspec = pl.BlockSpec((1, k) + fold, lambda i, h: (h, 0, i, 0))
    oshape = (PEER_HEADS, k, t // LANE, LANE)
    a, b, g = pl.pallas_call(
        _peer_route_kernel,
        grid=(t // tm, PEER_HEADS),
        in_specs=[pl.BlockSpec((tm, d), lambda i, h: (i, 0)),
                  pl.BlockSpec((d, nq), lambda i, h: (0, 0)),
                  pl.BlockSpec((d, nq), lambda i, h: (0, 0)),
                  pl.BlockSpec(sk_cat.shape, lambda i, h: (0, 0, 0))],
        out_specs=[outspec, outspec, outspec],
        out_shape=[jax.ShapeDtypeStruct(oshape, I32)] * 2 + [jax.ShapeDtypeStruct(oshape, F32)],
        scratch_shapes=[pltpu.VMEM((tm, nq), F32), pltpu.VMEM((PEER_KEYS * SUBL, LANE), F32),
                        pltpu.VMEM((PEER_KEYS * SUBL, LANE), F32), pltpu.VMEM((N_PAIRS * SUBL, LANE), F32),
                        pltpu.VMEM((k * SUBL, LANE), F32), pltpu.VMEM((k * SUBL, LANE), I32),
                        pltpu.VMEM((k * SUBL, LANE), F32), pltpu.VMEM((k * SUBL, LANE), I32),
                        pltpu.VMEM((k * SUBL, LANE), F32), pltpu.VMEM((k * SUBL, LANE), I32)],
        compiler_params=_cp(("parallel", "arbitrary")),
        name="peer_route",
    )(f2, wq_hi, wq_lo, sk_cat)
    return a.reshape(PEER_HEADS, k, t), b.reshape(PEER_HEADS, k, t), g.reshape(PEER_HEADS, k, t)


def _peer_dense_kernel(f_ref, h_ref, gm_ref, a_ref, b_ref, g_ref, u_ref, v_ref, o_ref,
                       f16_ref, ar_ref, br_ref, gr_ref, gs_ref, acc_ref, *, tiles_per_batch, n_ctx_tiles):
    i, e = pl.program_id(0), pl.program_id(1)
    nk = PEER_KEYS
    half = nk // 2
    hi_mask = jnp.uint32(0xFFFF0000)

    @pl.when(e == 0)
    def _():
        f16_ref[...] = f_ref[...].astype(BF16)
        acc_ref[...] = jnp.zeros_like(acc_ref)
        ar_ref[...] = a_ref[...].T
        br_ref[...] = b_ref[...].T
        gr_ref[...] = g_ref[...].T
        r = lax.broadcasted_iota(I32, (nk, nk), 0)
        key1 = jnp.where(r < half, 2 * r, 2 * (r - half) + 1)
        key2 = r

        def per_token(t, carry):
            arow = ar_ref[pl.ds(t, 1), :]
            brow = br_ref[pl.ds(t, 1), :]
            grow = 0.5 * gr_ref[pl.ds(t, 1), :]
            ga = jnp.where(key1 == arow, grow, 0.0).astype(BF16)
            ob = jnp.where(key2 == brow, 1.0, 0.0).astype(BF16)
            gm = _dot_nt(ga, ob).astype(BF16).astype(F32)
            bits = lax.bitcast_convert_type(gm, jnp.uint32)
            off = pl.multiple_of(t * G_PITCH, 8)
            gs_ref[pl.ds(off, half), :] = bits[:half] | (bits[half:] >> 16)
            return carry
        lax.fori_loop(0, PEER_TM, per_token, 0, unroll=64)

    f16 = f16_ref[...]
    sub = 2 * nk
    ws = []
    for c in range(PEER_TE // sub):
        s = _dot_nt(f16, u_ref[0, c * sub:(c + 1) * sub, :])
        word = gs_ref[pl.ds(e * (PEER_TE // sub) + c, PEER_TM, stride=G_PITCH), :]
        gt = jnp.concatenate([lax.bitcast_convert_type(word & hi_mask, F32),
                              lax.bitcast_convert_type(word << 16, F32)], axis=1)
        act = s * (1.0 + lax.erf(s * (2.0 ** -0.5)))
        ws.append((act * gt).astype(BF16))
    acc_ref[...] += _dot(jnp.concatenate(ws, axis=1), v_ref[0])

    @pl.when(e == pl.num_programs(1) - 1)
    def _():
        for part in range(PEER_TM // TM):
            j = i * (PEER_TM // TM) + part
            row = _mod_row(j // tiles_per_batch, j % tiles_per_batch, n_ctx_tiles)
            sl = slice(part * TM, (part + 1) * TM)
            o_ref[sl, :] = h_ref[sl, :] + gm_ref[0, pl.ds(row, 1), :] * acc_ref[sl, :]


def peer_dense(f2, h2, mods, layer, a_t, b_t, g_t, u16, v16, tiles_per_batch, n_ctx):
    t, d = f2.shape
    ne = u16.shape[1]
    hk = PEER_HEADS * PEER_TOPK
    kern = functools.partial(_peer_dense_kernel, tiles_per_batch=tiles_per_batch, n_ctx_tiles=n_ctx // TM)
    tok = lambda: pl.BlockSpec((PEER_TM, d), lambda i, e: (i, 0))
    rt = lambda: pl.BlockSpec((hk, PEER_TM), lambda i, e: (0, i))
    return pl.pallas_call(
        kern,
        grid=(t // PEER_TM, ne // PEER_TE),
        in_specs=[tok(), tok(), pl.BlockSpec((1, 8, d), lambda i, e: (layer, 0, 5)), rt(), rt(), rt(),
                  pl.BlockSpec((1, PEER_TE, d), lambda i, e: (layer, e, 0)),
                  pl.BlockSpec((1, PEER_TE, d), lambda i, e: (layer, e, 0))],
        out_specs=tok(),
        out_shape=jax.ShapeDtypeStruct((t, d), F32),
        scratch_shapes=[pltpu.VMEM((PEER_TM, d), BF16), pltpu.VMEM((PEER_TM, hk), I32),
                        pltpu.VMEM((PEER_TM, hk), I32), pltpu.VMEM((PEER_TM, hk), F32),
                        pltpu.VMEM((PEER_TM * G_PITCH, PEER_KEYS), jnp.uint32), pltpu.VMEM((PEER_TM, d), F32)],
        compiler_params=_cp(("parallel", "arbitrary")),
        name="peer_dense",
    )(f2, h2, mods, a_t, b_t, g_t, u16, v16)


def peer_ffn(f, h, mods, layer, wq, subkeys, u16, v16, n_ctx):
    bsz, lt, d = h.shape
    t = bsz * lt
    f2, h2 = f.reshape(t, d), h.reshape(t, d)
    wq_hi, wq_lo = _split2(wq)
    sk_hi, sk_lo = _split2(subkeys)
    a, b, g = peer_route(f2, wq_hi, wq_lo, jnp.concatenate([sk_hi, sk_hi, sk_lo], axis=-1))
    hk = PEER_HEADS * PEER_TOPK
    out = peer_dense(f2, h2, mods, layer, a.reshape(hk, t), b.reshape(hk, t), g.reshape(hk, t),
                     u16, v16, lt // TM, n_ctx)
    return out.reshape(bsz, lt, d)


def _final_kernel(h_ref, g_ref, o_ref):
    o_ref[0] = _rms(h_ref[0], g_ref[...])


def final_norm(h, g, n_ctx):
    bsz, lt, d = h.shape
    l = lt - n_ctx
    off = n_ctx // TM
    return pl.pallas_call(
        _final_kernel,
        grid=(bsz, l // TM),
        in_specs=[pl.BlockSpec((1, TM, d), lambda b, j: (b, j + off, 0)),
                  pl.BlockSpec((1, d), lambda b, j: (0, 0))],
        out_specs=pl.BlockSpec((1, TM, d), lambda b, j: (b, j, 0)),
        out_shape=jax.ShapeDtypeStruct((bsz, l, d), F32),
        compiler_params=_cp(("parallel", "parallel")),
        name="final_norm",
    )(h, g.reshape(1, d))


def _rope_tables(l, n_ctx):
    rows = l // GRID_W
    row = jnp.repeat(jnp.arange(rows), GRID_W).astype(F32)
    col = jnp.tile(jnp.arange(GRID_W), rows).astype(F32)
    n_freq = DIFF_HD // 4
    freqs = ROPE_THETA ** (-jnp.arange(n_freq, dtype=F32) / n_freq)
    ang = jnp.concatenate([row[:, None] * freqs, col[:, None] * freqs], axis=-1)
    cos, sin = jnp.cos(ang), jnp.sin(ang)
    cos = jnp.concatenate([jnp.ones((n_ctx, DIFF_HD // 2), F32), cos], axis=0)
    sin = jnp.concatenate([jnp.zeros((n_ctx, DIFF_HD // 2), F32), sin], axis=0)
    cos_t = jnp.concatenate([cos, cos, cos, cos], axis=1)
    sin_t = jnp.concatenate([-sin, sin, -sin, sin], axis=1)
    return cos_t, sin_t


def _gla_layer(h, mods, i, j, p, n_ctx):
    hk = GLA_HEADS * GLA_DK
    hv = GLA_HEADS * GLA_DV
    w_in = p["gla_w_in"][j]
    n_main = 2 * hk + 2 * hv
    w_lo = jnp.pad(w_in[:, n_main:], ((0, 0), (0, LANE - 2 * GLA_RANK))).astype(BF16)
    qkvg, lo = inproj(h, p["norm1_g"][i], mods, i, 0, [w_in[:, :n_main].astype(BF16), w_lo], n_ctx)
    wa = p["gla_w_alpha"][j]
    wa_pad = jnp.stack([jnp.pad(wa[0], ((0, LANE - GLA_RANK), (0, 0))),
                        jnp.pad(wa[1], ((GLA_RANK, LANE - 2 * GLA_RANK), (0, 0)))])
    ba = p["gla_b_alpha"][j].reshape(2, 1, hk)
    o_f, o_b = gla_scan(qkvg, lo, wa_pad, ba, n_ctx)
    tile = lambda: pl.BlockSpec((1, TM, hv), lambda b, t: (b, t, 0))
    specs = [tile(), tile(), pl.BlockSpec((1, TM, hv), lambda b, t: (b, t, 2)),
             pl.BlockSpec((1, GLA_DV), lambda b, t: (0, 0))]
    args = [o_f, o_b, qkvg, p["gla_norm_g"][j].reshape(1, GLA_DV)]
    return finish("gla", args, specs, p["gla_w_out"][j].astype(BF16), h, p["norm2_g"][i], mods, i, n_ctx)


def _diff_layer(h, mods, i, j, p, n_ctx, rope_tabs):
    wd = DIFF_HEADS * 2 * DIFF_HD
    cos_t, sin_t = rope_tabs
    lt = h.shape[1]
    qkv, = inproj(h, p["norm1_g"][i], mods, i, 0, [p["diff_w_in"][j].astype(BF16)], n_ctx,
                  out_dtype=BF16, rope=(cos_t, sin_t, 2 * wd, wd, DIFF_HD ** -0.5 * math.log2(math.e)))
    lam_init = 0.8 - 0.6 * math.exp(-0.3 * i)
    o = diff_attention(qkv, p["diff_lambda"][j], n_ctx, lam_init)
    specs = [pl.BlockSpec((1, TM, wd), lambda b, t: (b, t, 0)),
             pl.BlockSpec((1, 2 * DIFF_HD), lambda b, t: (0, 0))]
    args = [o, p["diff_norm_g"][j].reshape(1, 2 * DIFF_HD)]
    return finish("diff", args, specs, p["diff_w_out"][j].astype(BF16), h, p["norm2_g"][i], mods, i, n_ctx,
                  prm={"lam_init": lam_init})


def _ssd_layer(h, mods, i, j, p, n_ctx):
    din = SSD_HEADS * SSD_P
    gn = SSD_G * SSD_N
    w_in = p["ssd_w_in"][j]
    g1 = p["norm1_g"][i]
    w_dt = w_in[:, 2 * din + 2 * gn:].reshape(-1, 2, SSD_HEADS)
    w_dt = jnp.pad(w_dt, ((0, 0), (0, 0), (0, LANE - SSD_HEADS))).reshape(-1, 2 * LANE)
    z, xbc, dt = inproj(h, g1, mods, i, 0, [w_in[:, :din].astype(BF16),
                                              w_in[:, din:2 * din + 2 * gn].astype(BF16), w_dt.astype(BF16)], n_ctx)
    pad_h = lambda a: jnp.pad(a, ((0, 0), (0, LANE - SSD_HEADS))).reshape(1, -1)
    dtb, alog = pad_h(p["ssd_dt_bias"][j]), pad_h(p["ssd_a_log"][j])
    xc = ssd_conv(xbc, p["ssd_conv_w"][j], p["ssd_conv_b"][j], n_ctx)
    head_of_col = jnp.arange(SSD_G)[:, None, None] * SSD_R + jnp.arange(SSD_R * SSD_P)[None, None, :] // SSD_P
    expand = (jnp.arange(LANE)[None, :, None] == head_of_col).astype(BF16)
    y_f, y_b = ssd_scan(xc, dt, dtb, alog, expand, n_ctx)
    tile = lambda: pl.BlockSpec((1, TM, din), lambda b, t: (b, t, 0))
    row = lambda: pl.BlockSpec((1, din), lambda b, t: (0, 0))
    specs = [tile(), tile(), tile(), tile(), row(), row()]
    dexp = jnp.repeat(p["ssd_d"][j], SSD_P).reshape(1, din)
    args = [y_f, y_b, xc, z, dexp, p["ssd_norm_g"][j].reshape(1, din)]
    return finish("ssd", args, specs, p["ssd_w_out"][j].astype(BF16), h, p["norm2_g"][i], mods, i, n_ctx)


def kernel(x, c, ctx, c_ctx, norm1_g, norm2_g, w_mod, b_mod, peer_wq, peer_subkeys, peer_u, peer_v, gla_w_in, gla_w_alpha, gla_b_alpha, gla_norm_g, gla_w_out, diff_w_in, diff_lambda, diff_norm_g, diff_w_out, ssd_w_in, ssd_conv_w, ssd_conv_b, ssd_dt_bias, ssd_a_log, ssd_d, ssd_norm_g, ssd_w_out, final_g):
    p = dict(norm1_g=norm1_g, norm2_g=norm2_g, gla_w_in=gla_w_in, gla_w_alpha=gla_w_alpha, gla_b_alpha=gla_b_alpha,
             gla_norm_g=gla_norm_g, gla_w_out=gla_w_out, diff_w_in=diff_w_in, diff_lambda=diff_lambda,
             diff_norm_g=diff_norm_g, diff_w_out=diff_w_out, ssd_w_in=ssd_w_in, ssd_conv_w=ssd_conv_w,
             ssd_conv_b=ssd_conv_b, ssd_dt_bias=ssd_dt_bias, ssd_a_log=ssd_a_log, ssd_d=ssd_d,
             ssd_norm_g=ssd_norm_g, ssd_w_out=ssd_w_out)
    bsz, l, d = x.shape
    n_ctx = ctx.shape[1]
    depth = w_mod.shape[0]
    assert bsz <= 4 and n_ctx % TM == 0 and l % TM == 0 and l % GRID_W == 0
    cond8 = jnp.concatenate([c, jnp.zeros((4 - bsz, d), F32), c_ctx[None], jnp.zeros((3, d), F32)], axis=0)
    mods = mod_table(cond8, w_mod, b_mod)
    rope_tabs = _rope_tables(l, n_ctx)
    u16, v16 = peer_u.astype(BF16), peer_v.astype(BF16)
    h = jnp.concatenate([ctx, x], axis=1)
    for i in range(depth):
        kind, j = i % N_MIXERS, i // N_MIXERS
        if kind == 0:
            h, f = _gla_layer(h, mods, i, j, p, n_ctx)
        elif kind == 1:
            h, f = _diff_layer(h, mods, i, j, p, n_ctx, rope_tabs)
        else:
            h, f = _ssd_layer(h, mods, i, j, p, n_ctx)
        h = peer_ffn(f, h, mods, i, peer_wq[i], peer_subkeys[i], u16, v16, n_ctx)
    return final_norm(h, final_g, n_ctx)
```

```python
import functools
import math

import jax
import jax.numpy as jnp
from jax import lax
from jax.experimental import pallas as pl
from jax.experimental.pallas import tpu as pltpu

F32 = jnp.float32
BF16 = jnp.bfloat16
I32 = jnp.int32

EPS = 1e-6
GRID_W = 64
ROPE_THETA = 10000.0
N_MIXERS = 3

GLA_HEADS, GLA_DK, GLA_DV, GLA_RANK, GLA_TAU = 4, 128, 256, 16, 16.0
GLA_CHUNK = 128
DIFF_HEADS, DIFF_HD = 8, 64
DIFF_HPS = 2
SSD_HEADS, SSD_P, SSD_N, SSD_G, SSD_CONV, SSD_CHUNK = 32, 64, 128, 4, 5, 128
SSD_R = SSD_HEADS // SSD_G
PEER_KEYS, PEER_HEADS, PEER_TOPK = 128, 8, 16

TM = 256
LANE = 128
ROUTE_TM = 1024
PEER_TM = 512
PEER_TE = 2048
G_PITCH = 72
VMEM_LIMIT = 56 * 1024 * 1024


def _cp(sem, vmem=VMEM_LIMIT):
    return pltpu.CompilerParams(dimension_semantics=sem, vmem_limit_bytes=vmem)


def _dot(a, b):
    return jnp.dot(a, b, preferred_element_type=F32)


def _dot_nt(a, b):
    return lax.dot_general(a, b, (((1,), (1,)), ((), ())), preferred_element_type=F32)


def _split2(x):
    hi = x.astype(BF16)
    lo = (x - hi.astype(F32)).astype(BF16)
    return hi, lo


def _dot3(a, b):
    ah, al = _split2(a)
    bh, bl = _split2(b)
    return _dot(jnp.concatenate([ah, ah, al], axis=1), jnp.concatenate([bh, bl, bh], axis=0))


def _dot_exact_rhs(w01, x):
    h1 = x.astype(BF16)
    r1 = x - h1.astype(F32)
    h2 = r1.astype(BF16)
    h3 = (r1 - h2.astype(F32)).astype(BF16)
    return _dot(jnp.concatenate([w01, w01, w01], axis=1), jnp.concatenate([h1, h2, h3], axis=0))


def _dot_01(x, w01):
    xh, xl = _split2(x)
    return _dot(jnp.concatenate([xh, xl], axis=1), jnp.concatenate([w01, w01], axis=0))


def _silu(x):
    return x * (1.0 / (1.0 + jnp.exp(-x)))


def _softplus(x):
    return jnp.maximum(x, 0.0) + jnp.log1p(jnp.exp(-jnp.abs(x)))


def _rms(x, g):
    return x * lax.rsqrt(jnp.mean(x * x, axis=-1, keepdims=True) + EPS) * g


def _mod_row(b, j, n_ctx_tiles):
    return jnp.where(j < n_ctx_tiles, 4, b)


def _mod_kernel(cond_ref, w_ref, b_ref, o_ref):
    c = _silu(cond_ref[...])
    o_ref[0] = _dot3(c, w_ref[0]) + b_ref[0]


def mod_table(cond8, w_mod, b_mod):
    depth, d, n = w_mod.shape
    tn = 1024
    return pl.pallas_call(
        _mod_kernel,
        grid=(depth, n // tn),
        in_specs=[pl.BlockSpec((8, d), lambda i, j: (0, 0)),
                  pl.BlockSpec((1, d, tn), lambda i, j: (i, 0, j)),
                  pl.BlockSpec((1, 1, tn), lambda i, j: (i, 0, j))],
        out_specs=pl.BlockSpec((1, 8, tn), lambda i, j: (i, 0, j)),
        out_shape=jax.ShapeDtypeStruct((depth, 8, n), F32),
        compiler_params=_cp(("parallel", "parallel")),
        name="mod_table",
    )(cond8, w_mod, b_mod.reshape(depth, 1, n))


INPROJ_COLS = 1024


def _inproj_kernel(h_ref, g_ref, sh_ref, sc_ref, *rest, n_w, n_ctx_tiles, rope_cols, q_cols, q_scale):
    w_refs, rest = rest[:n_w], rest[n_w:]
    if rope_cols:
        cos_ref, sin_ref = rest[:2]
        rest = rest[2:]
    o_refs = rest
    b, j = pl.program_id(0), pl.program_id(1)
    row = _mod_row(b, j, n_ctx_tiles)
    a = _rms(h_ref[0], g_ref[...]) * (1.0 + sc_ref[0, pl.ds(row, 1), :]) + sh_ref[0, pl.ds(row, 1), :]
    a = a.astype(BF16)
    for k, (w_ref, o_ref) in enumerate(zip(w_refs, o_refs)):
        n = w_ref.shape[1]
        for c0 in range(0, n, INPROJ_COLS):
            c1 = min(n, c0 + INPROJ_COLS)
            y = _dot(a, w_ref[:, c0:c1])
            if k == 0 and c0 < rope_cols:
                tn = c1 - c0
                lane = lax.broadcasted_iota(I32, y.shape, 1)
                first = (lane & (DIFF_HD - 1)) < (DIFF_HD // 2)
                part = jnp.where(first, pltpu.roll(y, tn - DIFF_HD // 2, axis=1),
                                 pltpu.roll(y, DIFF_HD // 2, axis=1))
                cs = jnp.concatenate([cos_ref[0]] * (tn // LANE), axis=1)
                sn = jnp.concatenate([sin_ref[0]] * (tn // LANE), axis=1)
                y = y * cs + part * sn
                if c0 < q_cols:
                    y = y * q_scale
            o_ref[0, :, c0:c1] = y.astype(o_ref.dtype)


def inproj(h, g, mods, layer, k_shift, ws, n_ctx, out_dtype=F32, rope=None):
    bsz, lt, d = h.shape
    nt = lt // TM
    kern = functools.partial(_inproj_kernel, n_w=len(ws), n_ctx_tiles=n_ctx // TM, rope_cols=rope[2] if rope else 0,
                             q_cols=rope[3] if rope else 0, q_scale=rope[4] if rope else 1.0)
    in_specs = [pl.BlockSpec((1, TM, d), lambda b, j: (b, j, 0)),
                pl.BlockSpec((1, d), lambda b, j: (0, 0)),
                pl.BlockSpec((1, 8, d), lambda b, j: (layer, 0, k_shift)),
                pl.BlockSpec((1, 8, d), lambda b, j: (layer, 0, k_shift + 1))]
    in_specs += [pl.BlockSpec(w.shape, lambda b, j: (0, 0)) for w in ws]
    args = [h, g.reshape(1, d), mods, mods, *ws]
    out_specs = [pl.BlockSpec((1, TM, w.shape[1]), lambda b, j: (b, j, 0)) for w in ws]
    out_shape = [jax.ShapeDtypeStruct((bsz, lt, w.shape[1]), out_dtype) for w in ws]
    if rope:
        in_specs += [pl.BlockSpec((1, TM, LANE), lambda b, j: (0, j, 0))] * 2
        args += [rope[0][None], rope[1][None]]
    return pl.pallas_call(
        kern,
        grid=(bsz, nt),
        in_specs=in_specs,
        out_specs=out_specs,
        out_shape=out_shape,
        compiler_params=_cp(("parallel", "parallel")),
        name="inproj",
    )(*args)


def _post_gla(refs, prm):
    o_f, o_b, gate, ng = refs
    o = o_f[0] + o_b[0]
    g = gate[0]
    outs = []
    for hd in range(GLA_HEADS):
        sl = slice(hd * GLA_DV, (hd + 1) * GLA_DV)
        outs.append(_rms(o[:, sl], ng[...]) * _silu(g[:, sl]))
    return jnp.concatenate(outs, axis=1)


def _post_diff(refs, prm):
    o, ng = refs
    x = o[0]
    outs = []
    for hd in range(DIFF_HEADS):
        sl = slice(hd * 2 * DIFF_HD, (hd + 1) * 2 * DIFF_HD)
        outs.append(_rms(x[:, sl], ng[...]) * (1.0 - prm["lam_init"]))
    return jnp.concatenate(outs, axis=1)


def _post_ssd(refs, prm):
    y_f, y_b, xs, z, dexp, ng = refs
    y = (y_f[0] + y_b[0] + dexp[...] * xs[0]) * _silu(z[0])
    gs = y.shape[1] // SSD_G
    outs = []
    for gi in range(SSD_G):
        sl = slice(gi * gs, (gi + 1) * gs)
        outs.append(_rms(y[:, sl], ng[:, sl]))
    return jnp.concatenate(outs, axis=1)


_POST = {"gla": (_post_gla, 4), "diff": (_post_diff, 2), "ssd": (_post_ssd, 6)}


def _finish_kernel(*refs, kind, prm, n_ctx_tiles):
    post, n_in = _POST[kind]
    mix = refs[:n_in]
    w_ref, h_ref, gm_ref, g2_ref, sh_ref, sc_ref, hn_ref, f_ref = refs[n_in:]
    b, j = pl.program_id(0), pl.program_id(1)
    row = _mod_row(b, j, n_ctx_tiles)
    y = post(mix, prm)
    o = _dot(y.astype(BF16), w_ref[...])
    hn = h_ref[0] + gm_ref[0, pl.ds(row, 1), :] * o
    hn_ref[0] = hn
    f_ref[0] = _rms(hn, g2_ref[...]) * (1.0 + sc_ref[0, pl.ds(row, 1), :]) + sh_ref[0, pl.ds(row, 1), :]


def finish(kind, mix_args, mix_specs, w_out, h, g2, mods, layer, n_ctx, prm=None):
    bsz, lt, d = h.shape
    nt = lt // TM
    dm = w_out.shape[0]
    tile = lambda: pl.BlockSpec((1, TM, d), lambda b, j: (b, j, 0))
    modspec = lambda k: pl.BlockSpec((1, 8, d), lambda b, j: (layer, 0, k))
    kern = functools.partial(_finish_kernel, kind=kind, prm=prm or {}, n_ctx_tiles=n_ctx // TM)
    return pl.pallas_call(
        kern,
        grid=(bsz, nt),
        in_specs=list(mix_specs) + [pl.BlockSpec((dm, d), lambda b, j: (0, 0)), tile(), modspec(2),
                                    pl.BlockSpec((1, d), lambda b, j: (0, 0)), modspec(3), modspec(4)],
        out_specs=[tile(), tile()],
        out_shape=[jax.ShapeDtypeStruct((bsz, lt, d), F32)] * 2,
        compiler_params=_cp(("parallel", "parallel")),
        name="finish_" + kind,
    )(*mix_args, w_out, h, mods, g2.reshape(1, d), mods, mods)


def _gla_kernel(qf_ref, kf_ref, vf_ref, lof_ref, qb_ref, kb_ref, vb_ref, lob_ref, wa_ref, ba_ref,
                of_ref, ob_ref, stf_ref, stb_ref):
    @pl.when(pl.program_id(1) == 0)
    def _():
        stf_ref[...] = jnp.zeros_like(stf_ref)
        stb_ref[...] = jnp.zeros_like(stb_ref)

    _gla_chunk(qf_ref, kf_ref, vf_ref, lof_ref, wa_ref[0], ba_ref[0], of_ref, stf_ref, rev=False)
    _gla_chunk(qb_ref, kb_ref, vb_ref, lob_ref, wa_ref[1], ba_ref[1], ob_ref, stb_ref, rev=True)


def _gla_chunk(q_ref, k_ref, v_ref, lo_ref, wa, ba, o_ref, st_ref, *, rev):
    c = GLA_CHUNK
    hk = GLA_HEADS * GLA_DK
    heads = [slice(hd * GLA_DK, (hd + 1) * GLA_DK) for hd in range(GLA_HEADS)]

    q = q_ref[0] * (GLA_DK ** -0.5)
    k = k_ref[0]
    v = v_ref[0]
    z = _dot3(lo_ref[0], wa) + ba
    la = (jnp.minimum(z, 0.0) - jnp.log1p(jnp.exp(-jnp.abs(z)))) * (1.0 / GLA_TAU)

    ri = lax.broadcasted_iota(I32, (c, c), 0)
    ci = lax.broadcasted_iota(I32, (c, c), 1)
    row = lax.broadcasted_iota(I32, (c, hk), 0)
    tri = jnp.where((ci >= ri) if rev else (ci <= ri), 1.0, 0.0).astype(BF16)
    bsum = _dot_exact_rhs(tri, la)
    tot = bsum[0:1, :] if rev else bsum[c - 1:c, :]

    q16, k16 = q.astype(BF16), k.astype(BF16)
    eye = ci == ri
    a = [jnp.where(eye, _dot_nt(q16[:, sl], k16[:, sl]), 0.0) for sl in heads]
    s = c // 2
    while s >= 1:
        pos = row & (2 * s - 1)
        q_half = (pos < s) if rev else (pos >= s)
        if s >= 4:
            blk = bsum.reshape(c // (2 * s), 2 * s, hk)
            rr = s if rev else s - 1
            ref = jnp.broadcast_to(blk[:, rr:rr + 1, :], blk.shape).reshape(c, hk)
            e = jnp.where(q_half, bsum - ref, ref - bsum)
        elif s == 2:
            pos4 = row & 3
            nxt, prv = pltpu.roll(la, c - 1, axis=0), pltpu.roll(la, 1, axis=0)
            if rev:
                e = jnp.where(pos4 == 0, la + nxt, jnp.where(pos4 == 1, la, jnp.where(pos4 == 2, 0.0, prv)))
            else:
                e = jnp.where(pos4 == 3, la + prv, jnp.where(pos4 == 2, la, jnp.where(pos4 == 1, 0.0, nxt)))
        else:
            e = jnp.where(q_half, la, 0.0)
        f = jnp.exp(e)
        qd = jnp.where(q_half, q * f, 0.0).astype(BF16)
        kd = jnp.where(q_half, 0.0, k * f).astype(BF16)
        same_block = (ri & -(2 * s)) == (ci & -(2 * s))
        a = [a[hd] + jnp.where(same_block, _dot_nt(qd[:, sl], kd[:, sl]), 0.0) for hd, sl in enumerate(heads)]
        s //= 2

    qe = (q * jnp.exp(bsum)).astype(BF16)
    kt = (k * jnp.exp(tot - bsum)).astype(BF16)
    et = jnp.exp(tot)
    for hd, sl in enumerate(heads):
        vh = v[:, hd * GLA_DV:(hd + 1) * GLA_DV]
        st = st_ref[hd]
        o_ref[0, :, hd * GLA_DV:(hd + 1) * GLA_DV] = (_dot(a[hd].astype(BF16), vh.astype(BF16))
                                                       + _dot_nt(qe[:, sl], st.astype(BF16)))
        st_ref[hd] = st * et[:, sl] + _dot(vh.T.astype(BF16), kt[:, sl])


def _scan_chunk(c, n_ctx_chunks, n_chunks, rev):
    if not rev:
        return c
    return jnp.where(c < n_ctx_chunks, n_ctx_chunks - 1 - c, n_chunks - 1 - (c - n_ctx_chunks))


def gla_scan(qkvg, lo, wa_pad, ba, n_ctx):
    bsz, lt, _ = qkvg.shape
    c = GLA_CHUNK
    nch, ncc = lt // c, n_ctx // c
    hk, hv = GLA_HEADS * GLA_DK, GLA_HEADS * GLA_DV

    def chunk_specs(rev):
        cm = functools.partial(_scan_chunk, n_ctx_chunks=ncc, n_chunks=nch, rev=rev)
        ins = [pl.BlockSpec((1, c, hk), lambda b, i: (b, cm(i), 0)),
               pl.BlockSpec((1, c, hk), lambda b, i: (b, cm(i), 1)),
               pl.BlockSpec((1, c, hv), lambda b, i: (b, cm(i), 1)),
               pl.BlockSpec((1, c, LANE), lambda b, i: (b, cm(i), 0))]
        return ins, pl.BlockSpec((1, c, hv), lambda b, i: (b, cm(i), 0))

    (in_f, out_f), (in_b, out_b) = chunk_specs(False), chunk_specs(True)
    state = pltpu.VMEM((GLA_HEADS, GLA_DV, GLA_DK), F32)
    return pl.pallas_call(
        _gla_kernel,
        grid=(bsz, nch),
        in_specs=in_f + in_b + [pl.BlockSpec((2, LANE, hk), lambda b, i: (0, 0, 0)),
                                pl.BlockSpec((2, 1, hk), lambda b, i: (0, 0, 0))],
        out_specs=[out_f, out_b],
        out_shape=[jax.ShapeDtypeStruct((bsz, lt, hv), F32)] * 2,
        scratch_shapes=[state, state],
        compiler_params=_cp(("parallel", "arbitrary")),
        name="gla_scan",
    )(qkvg, qkvg, qkvg, lo, qkvg, qkvg, qkvg, lo, wa_pad, ba)


def _diff_kernel(lam_ref, q_ref, k_ref, v_ref, o_ref, *, n_ctx, lam_init):
    j = pl.program_id(2)
    lv = lam_ref[...]
    l1 = jnp.sum(lv[0:1] * lv[1:2], axis=1, keepdims=True)
    l2 = jnp.sum(lv[2:3] * lv[3:4], axis=1, keepdims=True)
    lam = jnp.exp(l1) - jnp.exp(l2) + lam_init

    lane = lax.broadcasted_iota(I32, (TM, LANE), 1)

    def attend(n_keys):
        for hh in range(DIFF_HPS):
            cols = slice(hh * LANE, (hh + 1) * LANE)
            q = q_ref[0, :, cols]
            zero = jnp.zeros_like(q)
            q0 = jnp.where(lane < DIFF_HD, q, zero)
            q1 = jnp.where(lane < DIFF_HD, zero, q)
            k = k_ref[0, :n_keys, cols]
            v = v_ref[0, :n_keys, cols]
            s0 = _dot_nt(q0, k)
            s1 = _dot_nt(q1, k)
            p0 = jnp.exp2(s0 - jnp.max(s0, axis=1, keepdims=True))
            p1 = jnp.exp2(s1 - jnp.max(s1, axis=1, keepdims=True))
            r0 = 1.0 / jnp.sum(p0, axis=1, keepdims=True)
            r1 = lam / jnp.sum(p1, axis=1, keepdims=True)
            o_ref[0, :, cols] = _dot(p0.astype(BF16), v) * r0 - _dot(p1.astype(BF16), v) * r1

    @pl.when(j * TM < n_ctx)
    def _():
        attend(n_ctx)

    @pl.when(j * TM >= n_ctx)
    def _():
        attend(k_ref.shape[1])


def diff_attention(qkv, lam_vecs, n_ctx, lam_init):
    bsz, lt, _ = qkv.shape
    nh = DIFF_HEADS // DIFF_HPS
    w = DIFF_HPS * LANE
    return pl.pallas_call(
        functools.partial(_diff_kernel, n_ctx=n_ctx, lam_init=lam_init),
        grid=(bsz, nh, lt // TM),
        in_specs=[pl.BlockSpec((4, DIFF_HD), lambda b, h, j: (0, 0)),
                  pl.BlockSpec((1, TM, w), lambda b, h, j: (b, j, h)),
                  pl.BlockSpec((1, lt, w), lambda b, h, j: (b, 0, nh + h)),
                  pl.BlockSpec((1, lt, w), lambda b, h, j: (b, 0, 2 * nh + h))],
        out_specs=pl.BlockSpec((1, TM, w), lambda b, h, j: (b, j, h)),
        out_shape=jax.ShapeDtypeStruct((bsz, lt, DIFF_HEADS * LANE), F32),
        compiler_params=_cp(("parallel", "parallel", "arbitrary")),
        name="diff_attention",
    )(lam_vecs, qkv, qkv, qkv)


def _conv_kernel(x_ref, p_ref, n_ref, w_ref, b_ref, o_ref, *, n_ctx_tiles, n_tiles):
    j = pl.program_id(1)
    first = jnp.logical_or(j == 0, j == n_ctx_tiles)
    last = jnp.logical_or(j == n_ctx_tiles - 1, j == n_tiles - 1)
    x = x_ref[0]
    prev = jnp.where(first, 0.0, p_ref[0])
    nxt = jnp.where(last, 0.0, n_ref[0])
    xe = jnp.concatenate([prev, x, nxt], axis=0)
    ne = xe.shape[0]
    half = SSD_CONV // 2
    y = b_ref[...] + w_ref[half:half + 1, :] * x
    for t in range(SSD_CONV):
        if t == half:
            continue
        sh = pltpu.roll(xe, (half - t) % ne, axis=0)[8:8 + TM]
        y = y + w_ref[t:t + 1, :] * sh
    o_ref[0] = _silu(y)


def ssd_conv(xbc, conv_w, conv_b, n_ctx):
    bsz, lt, ch = xbc.shape
    nt = lt // TM
    r8 = TM // 8
    return pl.pallas_call(
        functools.partial(_conv_kernel, n_ctx_tiles=n_ctx // TM, n_tiles=nt),
        grid=(bsz, nt),
        in_specs=[pl.BlockSpec((1, TM, ch), lambda b, j: (b, j, 0)),
                  pl.BlockSpec((1, 8, ch), lambda b, j: (b, jnp.maximum(j * r8 - 1, 0), 0)),
                  pl.BlockSpec((1, 8, ch), lambda b, j: (b, jnp.minimum((j + 1) * r8, lt // 8 - 1), 0)),
                  pl.BlockSpec((SSD_CONV, ch), lambda b, j: (0, 0)),
                  pl.BlockSpec((1, ch), lambda b, j: (0, 0))],
        out_specs=pl.BlockSpec((1, TM, ch), lambda b, j: (b, j, 0)),
        out_shape=jax.ShapeDtypeStruct((bsz, lt, ch), F32),
        compiler_params=_cp(("parallel", "parallel")),
        name="ssd_conv",
    )(xbc, xbc, xbc, conv_w, conv_b.reshape(1, ch))


def _ssd_kernel(xsf_ref, bmf_ref, cmf_ref, dtf_ref, xsb_ref, bmb_ref, cmb_ref, dtb_ref, bias_ref, al_ref, ex_ref,
                yf_ref, yb_ref, stf_ref, stb_ref):
    @pl.when(pl.program_id(1) == 0)
    def _():
        stf_ref[...] = jnp.zeros_like(stf_ref)
        stb_ref[...] = jnp.zeros_like(stb_ref)

    _ssd_chunk(xsf_ref, bmf_ref, cmf_ref, dtf_ref, bias_ref[:, :LANE], al_ref[:, :LANE], ex_ref, yf_ref, stf_ref,
               rev=False)
    _ssd_chunk(xsb_ref, bmb_ref, cmb_ref, dtb_ref, bias_ref[:, LANE:], al_ref[:, LANE:], ex_ref, yb_ref, stb_ref,
               rev=True)


def _ssd_chunk(xs_ref, bm_ref, cm_ref, dt_ref, dt_bias, a_log, ex_ref, y_ref, st_ref, *, rev):
    qn = SSD_CHUNK
    gw = SSD_R * SSD_P

    dt = _softplus(dt_ref[0] + dt_bias)
    da = dt * (-jnp.exp(a_log))
    ri = lax.broadcasted_iota(I32, (qn, qn), 0)
    ci = lax.broadcasted_iota(I32, (qn, qn), 1)
    causal = (ci >= ri) if rev else (ci <= ri)
    tri = jnp.where(causal, 1.0, 0.0).astype(BF16)
    acum = _dot_exact_rhs(tri, da)
    acum_t = acum.T
    dt_t = dt.T
    tot = acum[0:1, :] if rev else acum[qn - 1:qn, :]
    lane = lax.broadcasted_iota(I32, (qn, 2 * SSD_P), 1)
    e_acum = jnp.exp(acum)
    w_state = jnp.exp(tot - acum) * dt
    e_tot8 = jnp.broadcast_to(jnp.exp(tot), (8, LANE))

    for g in range(SSD_G):
        xs = xs_ref[0, :, g * gw:(g + 1) * gw]
        bm = bm_ref[0, :, g * SSD_N:(g + 1) * SSD_N]
        cmat = cm_ref[0, :, g * SSD_N:(g + 1) * SSD_N]
        ex = ex_ref[g]

        cb = _dot_nt(cmat.astype(BF16), bm.astype(BF16))
        xs16 = xs.astype(BF16)
        pieces = []
        for rp in range(SSD_R // 2):
            acc = None
            for sub in range(2):
                r = g * SSD_R + 2 * rp + sub
                seg = acum[:, r:r + 1] - acum_t[r:r + 1, :]
                w = cb * jnp.exp(jnp.minimum(seg, 0.0)) * dt_t[r:r + 1, :]
                w = jnp.where(causal, w, 0.0).astype(BF16)
                xpair = xs16[:, rp * 2 * SSD_P:(rp + 1) * 2 * SSD_P]
                keep = (lane < SSD_P) if sub == 0 else (lane >= SSD_P)
                part = _dot(w, jnp.where(keep, xpair, jnp.zeros_like(xpair)))
                acc = part if acc is None else acc + part
            pieces.append(acc)
        y = jnp.concatenate(pieces, axis=1)

        st = st_ref[g]
        e_i = _dot_01(e_acum, ex)
        y_ref[0, :, g * gw:(g + 1) * gw] = y + _dot(cmat.astype(BF16), st.astype(BF16)) * e_i
        wt = _dot_01(w_state, ex)
        e_tot = _dot_01(e_tot8, ex)[0:1, :]
        st_ref[g] = st * e_tot + _dot(bm.T.astype(BF16), (xs * wt).astype(BF16))


def ssd_scan(xc, dt, dtb, alog, expand, n_ctx):
    bsz, lt, _ = xc.shape
    qn = SSD_CHUNK
    nch, ncc = lt // qn, n_ctx // qn
    gw = SSD_R * SSD_P
    din, gn = SSD_HEADS * SSD_P, SSD_G * SSD_N

    def chunk_specs(rev):
        cm = functools.partial(_scan_chunk, n_ctx_chunks=ncc, n_chunks=nch, rev=rev)
        d = 1 if rev else 0
        ins = [pl.BlockSpec((1, qn, din), lambda b, i: (b, cm(i), 0)),
               pl.BlockSpec((1, qn, gn), lambda b, i: (b, cm(i), din // gn)),
               pl.BlockSpec((1, qn, gn), lambda b, i: (b, cm(i), din // gn + 1)),
               pl.BlockSpec((1, qn, LANE), lambda b, i: (b, cm(i), d))]
        return ins, pl.BlockSpec((1, qn, din), lambda b, i: (b, cm(i), 0))

    (in_f, out_f), (in_b, out_b) = chunk_specs(False), chunk_specs(True)
    state = pltpu.VMEM((SSD_G, SSD_N, gw), F32)
    return pl.pallas_call(
        _ssd_kernel,
        grid=(bsz, nch),
        in_specs=in_f + in_b + [pl.BlockSpec((1, 2 * LANE), lambda b, i: (0, 0)),
                                pl.BlockSpec((1, 2 * LANE), lambda b, i: (0, 0)),
                                pl.BlockSpec((SSD_G, LANE, gw), lambda b, i: (0, 0, 0))],
        out_specs=[out_f, out_b],
        out_shape=[jax.ShapeDtypeStruct((bsz, lt, din), F32)] * 2,
        scratch_shapes=[state, state],
        compiler_params=_cp(("parallel", "arbitrary")),
        name="ssd_scan",
    )(xc, xc, xc, dt, xc, xc, xc, dt, dtb, alog, expand)


SUBL = 8


def _vrow(ref, k):
    return ref[SUBL * k:SUBL * (k + 1), :]


def _top16(tasks):
    def body(r, carry):
        out = pl.ds(pl.multiple_of(r * SUBL, SUBL), SUBL)
        for s_ref, ids, vals_ref, idx_ref in tasks:
            rows = [_vrow(s_ref, k) for k in range(len(ids))]
            level = list(zip(rows, ids))
            while len(level) > 1:
                nxt = []
                for j in range(0, len(level) - 1, 2):
                    (va, ia), (vb, ib) = level[j], level[j + 1]
                    gt = vb > va
                    nxt.append((jnp.where(gt, vb, va), jnp.where(gt, ib, ia)))
                if len(level) % 2:
                    nxt.append(level[-1])
                level = nxt
            m, sel = level[0]
            for k, v in enumerate(rows):
                s_ref[SUBL * k:SUBL * (k + 1), :] = jnp.where(sel == ids[k], -jnp.inf, v)
            vals_ref[out, :] = m
            idx_ref[out, :] = sel
        return carry
    lax.fori_loop(0, PEER_TOPK, body, 0, unroll=2)


ROUTE_SUB = ROUTE_TM // LANE
assert ROUTE_SUB == SUBL
N_PAIRS = sum(1 for p in range(PEER_TOPK) for q in range(PEER_TOPK) if (p + 1) * (q + 1) <= PEER_TOPK)


def _peer_route_kernel(f_ref, wh_ref, wl_ref, kc_ref, a_ref, b_ref, g_ref,
                       q_ref, s1_ref, s2_ref, c_ref, v1_ref, i1_ref, v2_ref, i2_ref, vb_ref, ib_ref):
    h = pl.program_id(1)
    nk = PEER_KEYS
    rows_q = 256

    @pl.when(h == 0)
    def _():
        for rc in range(ROUTE_TM // rows_q):
            rs = slice(rc * rows_q, (rc + 1) * rows_q)
            fh, fl = _split2(f_ref[rs, :])
            q_ref[rs, :] = _dot(fh, wh_ref[...]) + _dot(fh, wl_ref[...]) + _dot(fl, wh_ref[...])

    for z, s_ref in enumerate((s1_ref, s2_ref)):
        col = pl.multiple_of((h * 2 + z) * nk, nk)
        kcat = kc_ref[z]
        for c in range(ROUTE_SUB):
            qh, ql = _split2(q_ref[c * LANE:(c + 1) * LANE, pl.ds(col, nk)])
            st = _dot_nt(kcat, jnp.concatenate([qh, ql, qh], axis=1))
            s_ref[pl.ds(c, nk, stride=ROUTE_SUB), :] = st
    keys = list(range(nk))
    _top16([(s1_ref, keys, v1_ref, i1_ref), (s2_ref, keys, v2_ref, i2_ref)])

    pairs = [(p, q) for p in range(PEER_TOPK) for q in range(PEER_TOPK) if (p + 1) * (q + 1) <= PEER_TOPK]
    for j, (p, q) in enumerate(pairs):
        c_ref[SUBL * j:SUBL * (j + 1), :] = _vrow(v1_ref, p) + _vrow(v2_ref, q)
    _top16([(c_ref, [p * PEER_TOPK + q for p, q in pairs], vb_ref, ib_ref)])

    fold3 = (PEER_TOPK, ROUTE_SUB, LANE)
    sel = ib_ref[...].reshape(fold3)
    p, qq = sel >> 4, sel & (PEER_TOPK - 1)
    i1, i2 = i1_ref[...].reshape(fold3), i2_ref[...].reshape(fold3)
    a = jnp.zeros_like(sel)
    b = jnp.zeros_like(sel)
    for t in range(PEER_TOPK):
        a = jnp.where(p == t, i1[t:t + 1], a)
        b = jnp.where(qq == t, i2[t:t + 1], b)
    vb = vb_ref[...].reshape(fold3)
    e = jnp.exp(vb - vb[0:1])
    a_ref[0] = a
    b_ref[0] = b
    g_ref[0] = e / jnp.sum(e, axis=0, keepdims=True)


def peer_route(f2, wq_hi, wq_lo, sk_cat):
    t, d = f2.shape
    nq = wq_hi.shape[1]
    k = PEER_TOPK
    tm = ROUTE_TM
    fold = (ROUTE_SUB, LANE)
    outspec = pl.BlockSpec((1, k) + fold, lambda i, h: (h, 0, i, 0))
    oshape = (PEER_HEADS, k, t // LANE, LANE)
    a, b, g = pl.pallas_call(
        _peer_route_kernel,
        grid=(t // tm, PEER_HEADS),
        in_specs=[pl.BlockSpec((tm, d), lambda i, h: (i, 0)),
                  pl.BlockSpec((d, nq), lambda i, h: (0, 0)),
                  pl.BlockSpec((d, nq), lambda i, h: (0, 0)),
                  pl.BlockSpec(sk_cat.shape, lambda i, h: (0, 0, 0))],
        out_specs=[outspec, outspec, outspec],
        out_shape=[jax.ShapeDtypeStruct(oshape, I32)] * 2 + [jax.ShapeDtypeStruct(oshape, F32)],
        scratch_shapes=[pltpu.VMEM((tm, nq), F32), pltpu.VMEM((PEER_KEYS * SUBL, LANE), F32),
                        pltpu.VMEM((PEER_KEYS * SUBL, LANE), F32), pltpu.VMEM((N_PAIRS * SUBL, LANE), F32),
                        pltpu.VMEM((k * SUBL, LANE), F32), pltpu.VMEM((k * SUBL, LANE), I32),
                        pltpu.VMEM((k * SUBL, LANE), F32), pltpu.VMEM((k * SUBL, LANE), I32),
                        pltpu.VMEM((k * SUBL, LANE), F32), pltpu.VMEM((k * SUBL, LANE), I32)],
        compiler_params=_cp(("parallel", "arbitrary")),
        name="peer_route",
    )(f2, wq_hi, wq_lo, sk_cat)
    return a.reshape(PEER_HEADS, k, t), b.reshape(PEER_HEADS, k, t), g.reshape(PEER_HEADS, k, t)


def _peer_dense_kernel(f_ref, h_ref, gm_ref, a_ref, b_ref, g_ref, u_ref, v_ref, o_ref,
                       f16_ref, ar_ref, br_ref, gr_ref, gs_ref, acc_ref, *, tiles_per_batch, n_ctx_tiles):
    i, e = pl.program_id(0), pl.program_id(1)
    nk = PEER_KEYS
    half = nk // 2
    hi_mask = jnp.uint32(0xFFFF0000)

    @pl.when(e == 0)
    def _():
        f16_ref[...] = f_ref[...].astype(BF16)
        acc_ref[...] = jnp.zeros_like(acc_ref)
        ar_ref[...] = a_ref[...].T
        br_ref[...] = b_ref[...].T
        gr_ref[...] = g_ref[...].T
        r = lax.broadcasted_iota(I32, (nk, nk), 0)
        key1 = jnp.where(r < half, 2 * r, 2 * (r - half) + 1)
        key2 = r

        def per_token(t, carry):
            arow = ar_ref[pl.ds(t, 1), :]
            brow = br_ref[pl.ds(t, 1), :]
            grow = 0.5 * gr_ref[pl.ds(t, 1), :]
            ga = jnp.where(key1 == arow, grow, 0.0).astype(BF16)
            ob = jnp.where(key2 == brow, 1.0, 0.0).astype(BF16)
            gm = _dot_nt(ga, ob).astype(BF16).astype(F32)
            bits = lax.bitcast_convert_type(gm, jnp.uint32)
            off = pl.multiple_of(t * G_PITCH, 8)
            gs_ref[pl.ds(off, half), :] = bits[:half] | (bits[half:] >> 16)
            return carry
        lax.fori_loop(0, PEER_TM, per_token, 0, unroll=64)

    f16 = f16_ref[...]
    sub = 2 * nk
    ws = []
    for c in range(PEER_TE // sub):
        s = _dot_nt(f16, u_ref[0, c * sub:(c + 1) * sub, :])
        word = gs_ref[pl.ds(e * (PEER_TE // sub) + c, PEER_TM, stride=G_PITCH), :]
        gt = jnp.concatenate([lax.bitcast_convert_type(word & hi_mask, F32),
                              lax.bitcast_convert_type(word << 16, F32)], axis=1)
        act = s * (1.0 + lax.erf(s * (2.0 ** -0.5)))
        ws.append((act * gt).astype(BF16))
    acc_ref[...] += _dot(jnp.concatenate(ws, axis=1), v_ref[0])

    @pl.when(e == pl.num_programs(1) - 1)
    def _():
        for part in range(PEER_TM // TM):
            j = i * (PEER_TM // TM) + part
            row = _mod_row(j // tiles_per_batch, j % tiles_per_batch, n_ctx_tiles)
            sl = slice(part * TM, (part + 1) * TM)
            o_ref[sl, :] = h_ref[sl, :] + gm_ref[0, pl.ds(row, 1), :] * acc_ref[sl, :]


def peer_dense(f2, h2, mods, layer, a_t, b_t, g_t, u16, v16, tiles_per_batch, n_ctx):
    t, d = f2.shape
    ne = u16.shape[1]
    hk = PEER_HEADS * PEER_TOPK
    kern = functools.partial(_peer_dense_kernel, tiles_per_batch=tiles_per_batch, n_ctx_tiles=n_ctx // TM)
    tok = lambda: pl.BlockSpec((PEER_TM, d), lambda i, e: (i, 0))
    rt = lambda: pl.BlockSpec((hk, PEER_TM), lambda i, e: (0, i))
    return pl.pallas_call(
        kern,
        grid=(t // PEER_TM, ne // PEER_TE),
        in_specs=[tok(), tok(), pl.BlockSpec((1, 8, d), lambda i, e: (layer, 0, 5)), rt(), rt(), rt(),
                  pl.BlockSpec((1, PEER_TE, d), lambda i, e: (layer, e, 0)),
                  pl.BlockSpec((1, PEER_TE, d), lambda i, e: (layer, e, 0))],
        out_specs=tok(),
        out_shape=jax.ShapeDtypeStruct((t, d), F32),
        scratch_shapes=[pltpu.VMEM((PEER_TM, d), BF16), pltpu.VMEM((PEER_TM, hk), I32),
                        pltpu.VMEM((PEER_TM, hk), I32), pltpu.VMEM((PEER_TM, hk), F32),
                        pltpu.VMEM((PEER_TM * G_PITCH, PEER_KEYS), jnp.uint32), pltpu.VMEM((PEER_TM, d), F32)],
        compiler_params=_cp(("parallel", "arbitrary")),
        name="peer_dense",
    )(f2, h2, mods, a_t, b_t, g_t, u16, v16)


def peer_ffn(f, h, mods, layer, wq, subkeys, u16, v16, n_ctx):
    bsz, lt, d = h.shape
    t = bsz * lt
    f2, h2 = f.reshape(t, d), h.reshape(t, d)
    wq_hi, wq_lo = _split2(wq)
    sk_hi, sk_lo = _split2(subkeys)
    a, b, g = peer_route(f2, wq_hi, wq_lo, jnp.concatenate([sk_hi, sk_hi, sk_lo], axis=-1))
    hk = PEER_HEADS * PEER_TOPK
    out = peer_dense(f2, h2, mods, layer, a.reshape(hk, t), b.reshape(hk, t), g.reshape(hk, t),
                     u16, v16, lt // TM, n_ctx)
    return out.reshape(bsz, lt, d)


def _final_kernel(h_ref, g_ref, o_ref):
    o_ref[0] = _rms(h_ref[0], g_ref[...])


def final_norm(h, g, n_ctx):
    bsz, lt, d = h.shape
    l = lt - n_ctx
    off = n_ctx // TM
    return pl.pallas_call(
        _final_kernel,
        grid=(bsz, l // TM),
        in_specs=[pl.BlockSpec((1, TM, d), lambda b, j: (b, j + off, 0)),
                  pl.BlockSpec((1, d), lambda b, j: (0, 0))],
        out_specs=pl.BlockSpec((1, TM, d), lambda b, j: (b, j, 0)),
        out_shape=jax.ShapeDtypeStruct((bsz, l, d), F32),
        compiler_params=_cp(("parallel", "parallel")),
        name="final_norm",
    )(h, g.reshape(1, d))


def _rope_tables(l, n_ctx):
    rows = l // GRID_W
    row = jnp.repeat(jnp.arange(rows), GRID_W).astype(F32)
    col = jnp.tile(jnp.arange(GRID_W), rows).astype(F32)
    n_freq = DIFF_HD // 4
    freqs = ROPE_THETA ** (-jnp.arange(n_freq, dtype=F32) / n_freq)
    ang = jnp.concatenate([row[:, None] * freqs, col[:, None] * freqs], axis=-1)
    cos, sin = jnp.cos(ang), jnp.sin(ang)
    cos = jnp.concatenate([jnp.ones((n_ctx, DIFF_HD // 2), F32), cos], axis=0)
    sin = jnp.concatenate([jnp.zeros((n_ctx, DIFF_HD // 2), F32), sin], axis=0)
    cos_t = jnp.concatenate([cos, cos, cos, cos], axis=1)
    sin_t = jnp.concatenate([-sin, sin, -sin, sin], axis=1)
    return cos_t, sin_t


def _gla_layer(h, mods, i, j, p, n_ctx):
    hk = GLA_HEADS * GLA_DK
    hv = GLA_HEADS * GLA_DV
    w_in = p["gla_w_in"][j]
    n_main = 2 * hk + 2 * hv
    w_lo = jnp.pad(w_in[:, n_main:], ((0, 0), (0, LANE - 2 * GLA_RANK))).astype(BF16)
    qkvg, lo = inproj(h, p["norm1_g"][i], mods, i, 0, [w_in[:, :n_main].astype(BF16), w_lo], n_ctx)
    wa = p["gla_w_alpha"][j]
    wa_pad = jnp.stack([jnp.pad(wa[0], ((0, LANE - GLA_RANK), (0, 0))),
                        jnp.pad(wa[1], ((GLA_RANK, LANE - 2 * GLA_RANK), (0, 0)))])
    ba = p["gla_b_alpha"][j].reshape(2, 1, hk)
    o_f, o_b = gla_scan(qkvg, lo, wa_pad, ba, n_ctx)
    tile = lambda: pl.BlockSpec((1, TM, hv), lambda b, t: (b, t, 0))
    specs = [tile(), tile(), pl.BlockSpec((1, TM, hv), lambda b, t: (b, t, 2)),
             pl.BlockSpec((1, GLA_DV), lambda b, t: (0, 0))]
    args = [o_f, o_b, qkvg, p["gla_norm_g"][j].reshape(1, GLA_DV)]
    return finish("gla", args, specs, p["gla_w_out"][j].astype(BF16), h, p["norm2_g"][i], mods, i, n_ctx)


def _diff_layer(h, mods, i, j, p, n_ctx, rope_tabs):
    wd = DIFF_HEADS * 2 * DIFF_HD
    cos_t, sin_t = rope_tabs
    lt = h.shape[1]
    qkv, = inproj(h, p["norm1_g"][i], mods, i, 0, [p["diff_w_in"][j].astype(BF16)], n_ctx,
                  out_dtype=BF16, rope=(cos_t, sin_t, 2 * wd, wd, DIFF_HD ** -0.5 * math.log2(math.e)))
    lam_init = 0.8 - 0.6 * math.exp(-0.3 * i)
    o = diff_attention(qkv, p["diff_lambda"][j], n_ctx, lam_init)
    specs = [pl.BlockSpec((1, TM, wd), lambda b, t: (b, t, 0)),
             pl.BlockSpec((1, 2 * DIFF_HD), lambda b, t: (0, 0))]
    args = [o, p["diff_norm_g"][j].reshape(1, 2 * DIFF_HD)]
    return finish("diff", args, specs, p["diff_w_out"][j].astype(BF16), h, p["norm2_g"][i], mods, i, n_ctx,
                  prm={"lam_init": lam_init})


def _ssd_layer(h, mods, i, j, p, n_ctx):
    din = SSD_HEADS * SSD_P
    gn = SSD_G * SSD_N
    w_in = p["ssd_w_in"][j]
    g1 = p["norm1_g"][i]
    w_dt = w_in[:, 2 * din + 2 * gn:].reshape(-1, 2, SSD_HEADS)
    w_dt = jnp.pad(w_dt, ((0, 0), (0, 0), (0, LANE - SSD_HEADS))).reshape(-1, 2 * LANE)
    z, xbc, dt = inproj(h, g1, mods, i, 0, [w_in[:, :din].astype(BF16),
                                              w_in[:, din:2 * din + 2 * gn].astype(BF16), w_dt.astype(BF16)], n_ctx)
    pad_h = lambda a: jnp.pad(a, ((0, 0), (0, LANE - SSD_HEADS))).reshape(1, -1)
    dtb, alog = pad_h(p["ssd_dt_bias"][j]), pad_h(p["ssd_a_log"][j])
    xc = ssd_conv(xbc, p["ssd_conv_w"][j], p["ssd_conv_b"][j], n_ctx)
    head_of_col = jnp.arange(SSD_G)[:, None, None] * SSD_R + jnp.arange(SSD_R * SSD_P)[None, None, :] // SSD_P
    expand = (jnp.arange(LANE)[None, :, None] == head_of_col).astype(BF16)
    y_f, y_b = ssd_scan(xc, dt, dtb, alog, expand, n_ctx)
    tile = lambda: pl.BlockSpec((1, TM, din), lambda b, t: (b, t, 0))
    row = lambda: pl.BlockSpec((1, din), lambda b, t: (0, 0))
    specs = [tile(), tile(), tile(), tile(), row(), row()]
    dexp = jnp.repeat(p["ssd_d"][j], SSD_P).reshape(1, din)
    args = [y_f, y_b, xc, z, dexp, p["ssd_norm_g"][j].reshape(1, din)]
    return finish("ssd", args, specs, p["ssd_w_out"][j].astype(BF16), h, p["norm2_g"][i], mods, i, n_ctx)


def kernel(x, c, ctx, c_ctx, norm1_g, norm2_g, w_mod, b_mod, peer_wq, peer_subkeys, peer_u, peer_v, gla_w_in, gla_w_alpha, gla_b_alpha, gla_norm_g, gla_w_out, diff_w_in, diff_lambda, diff_norm_g, diff_w_out, ssd_w_in, ssd_conv_w, ssd_conv_b, ssd_dt_bias, ssd_a_log, ssd_d, ssd_norm_g, ssd_w_out, final_g):
    p = dict(norm1_g=norm1_g, norm2_g=norm2_g, gla_w_in=gla_w_in, gla_w_alpha=gla_w_alpha, gla_b_alpha=gla_b_alpha,
             gla_norm_g=gla_norm_g, gla_w_out=gla_w_out, diff_w_in=diff_w_in, diff_lambda=diff_lambda,
             diff_norm_g=diff_norm_g, diff_w_out=diff_w_out, ssd_w_in=ssd_w_in, ssd_conv_w=ssd_conv_w,
             ssd_conv_b=ssd_conv_b, ssd_dt_bias=ssd_dt_bias, ssd_a_log=ssd_a_log, ssd_d=ssd_d,
             ssd_norm_g=ssd_norm_g, ssd_w_out=ssd_w_out)
    bsz, l, d = x.shape
    n_ctx = ctx.shape[1]
    depth = w_mod.shape[0]
    assert bsz <= 4 and n_ctx % TM == 0 and l % TM == 0 and l % GRID_W == 0
    cond8 = jnp.concatenate([c, jnp.zeros((4 - bsz, d), F32), c_ctx[None], jnp.zeros((3, d), F32)], axis=0)
    mods = mod_table(cond8, w_mod, b_mod)
    rope_tabs = _rope_tables(l, n_ctx)
    u16, v16 = peer_u.astype(BF16), peer_v.astype(BF16)
    h = jnp.concatenate([ctx, x], axis=1)
    for i in range(depth):
        kind, j = i % N_MIXERS, i // N_MIXERS
        if kind == 0:
            h, f = _gla_layer(h, mods, i, j, p, n_ctx)
        elif kind == 1:
            h, f = _diff_layer(h, mods, i, j, p, n_ctx, rope_tabs)
        else:
            h, f = _ssd_layer(h, mods, i, j, p, n_ctx)
        h = peer_ffn(f, h, mods, i, peer_wq[i], peer_subkeys[i], u16, v16, n_ctx)
    return final_norm(h, final_g, n_ctx)
```

```python
import functools
import math

import jax
import jax.numpy as jnp
from jax import lax
from jax.experimental import pallas as pl
from jax.experimental.pallas import tpu as pltpu

F32 = jnp.float32
BF16 = jnp.bfloat16
I32 = jnp.int32

EPS = 1e-6
GRID_W = 64
ROPE_THETA = 10000.0
N_MIXERS = 3

GLA_HEADS, GLA_DK, GLA_DV, GLA_RANK, GLA_TAU = 4, 128, 256, 16, 16.0
GLA_CHUNK = 128
DIFF_HEADS, DIFF_HD = 8, 64
DIFF_HPS = 2
SSD_HEADS, SSD_P, SSD_N, SSD_G, SSD_CONV, SSD_CHUNK = 32, 64, 128, 4, 5, 128
SSD_R = SSD_HEADS // SSD_G
PEER_KEYS, PEER_HEADS, PEER_TOPK = 128, 8, 16

TM = 256
LANE = 128
ROUTE_TM = 1024
PEER_TM = 512
PEER_TE = 2048
G_PITCH = 72
VMEM_LIMIT = 56 * 1024 * 1024


def _cp(sem, vmem=VMEM_LIMIT):
    return pltpu.CompilerParams(dimension_semantics=sem, vmem_limit_bytes=vmem)


def _dot(a, b):
    return jnp.dot(a, b, preferred_element_type=F32)


def _dot_nt(a, b):
    return lax.dot_general(a, b, (((1,), (1,)), ((), ())), preferred_element_type=F32)


def _split2(x):
    hi = x.astype(BF16)
    lo = (x - hi.astype(F32)).astype(BF16)
    return hi, lo


def _dot3(a, b):
    ah, al = _split2(a)
    bh, bl = _split2(b)
    return _dot(jnp.concatenate([ah, ah, al], axis=1), jnp.concatenate([bh, bl, bh], axis=0))


def _dot_exact_rhs(w01, x):
    h1 = x.astype(BF16)
    r1 = x - h1.astype(F32)
    h2 = r1.astype(BF16)
    h3 = (r1 - h2.astype(F32)).astype(BF16)
    return _dot(jnp.concatenate([w01, w01, w01], axis=1), jnp.concatenate([h1, h2, h3], axis=0))


def _dot_01(x, w01):
    xh, xl = _split2(x)
    return _dot(jnp.concatenate([xh, xl], axis=1), jnp.concatenate([w01, w01], axis=0))


def _silu(x):
    return x * (1.0 / (1.0 + jnp.exp(-x)))


def _softplus(x):
    return jnp.maximum(x, 0.0) + jnp.log1p(jnp.exp(-jnp.abs(x)))


def _rms(x, g):
    return x * lax.rsqrt(jnp.mean(x * x, axis=-1, keepdims=True) + EPS) * g


def _mod_row(b, j, n_ctx_tiles):
    return jnp.where(j < n_ctx_tiles, 4, b)


def _mod_kernel(cond_ref, w_ref, b_ref, o_ref):
    c = _silu(cond_ref[...])
    o_ref[0] = _dot3(c, w_ref[0]) + b_ref[0]


def mod_table(cond8, w_mod, b_mod):
    depth, d, n = w_mod.shape
    tn = 1024
    return pl.pallas_call(
        _mod_kernel,
        grid=(depth, n // tn),
        in_specs=[pl.BlockSpec((8, d), lambda i, j: (0, 0)),
                  pl.BlockSpec((1, d, tn), lambda i, j: (i, 0, j)),
                  pl.BlockSpec((1, 1, tn), lambda i, j: (i, 0, j))],
        out_specs=pl.BlockSpec((1, 8, tn), lambda i, j: (i, 0, j)),
        out_shape=jax.ShapeDtypeStruct((depth, 8, n), F32),
        compiler_params=_cp(("parallel", "parallel")),
        name="mod_table",
    )(cond8, w_mod, b_mod.reshape(depth, 1, n))


INPROJ_COLS = 1024


def _inproj_kernel(h_ref, g_ref, sh_ref, sc_ref, *rest, n_w, n_ctx_tiles, rope_cols, q_cols, q_scale):
    w_refs, rest = rest[:n_w], rest[n_w:]
    if rope_cols:
        cos_ref, sin_ref = rest[:2]
        rest = rest[2:]
    o_refs = rest
    b, j = pl.program_id(0), pl.program_id(1)
    row = _mod_row(b, j, n_ctx_tiles)
    a = _rms(h_ref[0], g_ref[...]) * (1.0 + sc_ref[0, pl.ds(row, 1), :]) + sh_ref[0, pl.ds(row, 1), :]
    a = a.astype(BF16)
    for k, (w_ref, o_ref) in enumerate(zip(w_refs, o_refs)):
        n = w_ref.shape[1]
        for c0 in range(0, n, INPROJ_COLS):
            c1 = min(n, c0 + INPROJ_COLS)
            y = _dot(a, w_ref[:, c0:c1])
            if k == 0 and c0 < rope_cols:
                tn = c1 - c0
                lane = lax.broadcasted_iota(I32, y.shape, 1)
                first = (lane & (DIFF_HD - 1)) < (DIFF_HD // 2)
                part = jnp.where(first, pltpu.roll(y, tn - DIFF_HD // 2, axis=1),
                                 pltpu.roll(y, DIFF_HD // 2, axis=1))
                cs = jnp.concatenate([cos_ref[0]] * (tn // LANE), axis=1)
                sn = jnp.concatenate([sin_ref[0]] * (tn // LANE), axis=1)
                y = y * cs + part * sn
                if c0 < q_cols:
                    y = y * q_scale
            o_ref[0, :, c0:c1] = y.astype(o_ref.dtype)


def inproj(h, g, mods, layer, k_shift, ws, n_ctx, out_dtype=F32, rope=None):
    bsz, lt, d = h.shape
    nt = lt // TM
    kern = functools.partial(_inproj_kernel, n_w=len(ws), n_ctx_tiles=n_ctx // TM, rope_cols=rope[2] if rope else 0,
                             q_cols=rope[3] if rope else 0, q_scale=rope[4] if rope else 1.0)
    in_specs = [pl.BlockSpec((1, TM, d), lambda b, j: (b, j, 0)),
                pl.BlockSpec((1, d), lambda b, j: (0, 0)),
                pl.BlockSpec((1, 8, d), lambda b, j: (layer, 0, k_shift)),
                pl.BlockSpec((1, 8, d), lambda b, j: (layer, 0, k_shift + 1))]
    in_specs += [pl.BlockSpec(w.shape, lambda b, j: (0, 0)) for w in ws]
    args = [h, g.reshape(1, d), mods, mods, *ws]
    out_specs = [pl.BlockSpec((1, TM, w.shape[1]), lambda b, j: (b, j, 0)) for w in ws]
    out_shape = [jax.ShapeDtypeStruct((bsz, lt, w.shape[1]), out_dtype) for w in ws]
    if rope:
        in_specs += [pl.BlockSpec((1, TM, LANE), lambda b, j: (0, j, 0))] * 2
        args += [rope[0][None], rope[1][None]]
    return pl.pallas_call(
        kern,
        grid=(bsz, nt),
        in_specs=in_specs,
        out_specs=out_specs,
        out_shape=out_shape,
        compiler_params=_cp(("parallel", "parallel")),
        name="inproj",
    )(*args)


def _post_gla(refs, prm):
    o_f, o_b, gate, ng = refs
    o = o_f[0] + o_b[0]
    g = gate[0]
    outs = []
    for hd in range(GLA_HEADS):
        sl = slice(hd * GLA_DV, (hd + 1) * GLA_DV)
        outs.append(_rms(o[:, sl], ng[...]) * _silu(g[:, sl]))
    return jnp.concatenate(outs, axis=1)


def _post_diff(refs, prm):
    o, ng = refs
    x = o[0]
    outs = []
    for hd in range(DIFF_HEADS):
        sl = slice(hd * 2 * DIFF_HD, (hd + 1) * 2 * DIFF_HD)
        outs.append(_rms(x[:, sl], ng[...]) * (1.0 - prm["lam_init"]))
    return jnp.concatenate(outs, axis=1)


def _post_ssd(refs, prm):
    y_f, y_b, xs, z, dexp, ng = refs
    y = (y_f[0] + y_b[0] + dexp[...] * xs[0]) * _silu(z[0])
    gs = y.shape[1] // SSD_G
    outs = []
    for gi in range(SSD_G):
        sl = slice(gi * gs, (gi + 1) * gs)
        outs.append(_rms(y[:, sl], ng[:, sl]))
    return jnp.concatenate(outs, axis=1)


_POST = {"gla": (_post_gla, 4), "diff": (_post_diff, 2), "ssd": (_post_ssd, 6)}


def _finish_kernel(*refs, kind, prm, n_ctx_tiles):
    post, n_in = _POST[kind]
    mix = refs[:n_in]
    w_ref, h_ref, gm_ref, g2_ref, sh_ref, sc_ref, hn_ref, f_ref = refs[n_in:]
    b, j = pl.program_id(0), pl.program_id(1)
    row = _mod_row(b, j, n_ctx_tiles)
    y = post(mix, prm)
    o = _dot(y.astype(BF16), w_ref[...])
    hn = h_ref[0] + gm_ref[0, pl.ds(row, 1), :] * o
    hn_ref[0] = hn
    f_ref[0] = _rms(hn, g2_ref[...]) * (1.0 + sc_ref[0, pl.ds(row, 1), :]) + sh_ref[0, pl.ds(row, 1), :]


def finish(kind, mix_args, mix_specs, w_out, h, g2, mods, layer, n_ctx, prm=None):
    bsz, lt, d = h.shape
    nt = lt // TM
    dm = w_out.shape[0]
    tile = lambda: pl.BlockSpec((1, TM, d), lambda b, j: (b, j, 0))
    modspec = lambda k: pl.BlockSpec((1, 8, d), lambda b, j: (layer, 0, k))
    kern = functools.partial(_finish_kernel, kind=kind, prm=prm or {}, n_ctx_tiles=n_ctx // TM)
    return pl.pallas_call(
        kern,
        grid=(bsz, nt),
        in_specs=list(mix_specs) + [pl.BlockSpec((dm, d), lambda b, j: (0, 0)), tile(), modspec(2),
                                    pl.BlockSpec((1, d), lambda b, j: (0, 0)), modspec(3), modspec(4)],
        out_specs=[tile(), tile()],
        out_shape=[jax.ShapeDtypeStruct((bsz, lt, d), F32)] * 2,
        compiler_params=_cp(("parallel", "parallel")),
        name="finish_" + kind,
    )(*mix_args, w_out, h, mods, g2.reshape(1, d), mods, mods)


def _gla_kernel(qf_ref, kf_ref, vf_ref, lof_ref, qb_ref, kb_ref, vb_ref, lob_ref, wa_ref, ba_ref,
                of_ref, ob_ref, stf_ref, stb_ref):
    @pl.when(pl.program_id(1) == 0)
    def _():
        stf_ref[...] = jnp.zeros_like(stf_ref)
        stb_ref[...] = jnp.zeros_like(stb_ref)

    _gla_chunk(qf_ref, kf_ref, vf_ref, lof_ref, wa_ref[0], ba_ref[0], of_ref, stf_ref, rev=False)
    _gla_chunk(qb_ref, kb_ref, vb_ref, lob_ref, wa_ref[1], ba_ref[1], ob_ref, stb_ref, rev=True)


def _gla_chunk(q_ref, k_ref, v_ref, lo_ref, wa, ba, o_ref, st_ref, *, rev):
    c = GLA_CHUNK
    hk = GLA_HEADS * GLA_DK
    heads = [slice(hd * GLA_DK, (hd + 1) * GLA_DK) for hd in range(GLA_HEADS)]

    q = q_ref[0] * (GLA_DK ** -0.5)
    k = k_ref[0]
    v = v_ref[0]
    z = _dot3(lo_ref[0], wa) + ba
    la = (jnp.minimum(z, 0.0) - jnp.log1p(jnp.exp(-jnp.abs(z)))) * (1.0 / GLA_TAU)

    ri = lax.broadcasted_iota(I32, (c, c), 0)
    ci = lax.broadcasted_iota(I32, (c, c), 1)
    row = lax.broadcasted_iota(I32, (c, hk), 0)
    tri = jnp.where((ci >= ri) if rev else (ci <= ri), 1.0, 0.0).astype(BF16)
    bsum = _dot_exact_rhs(tri, la)
    tot = bsum[0:1, :] if rev else bsum[c - 1:c, :]

    q16, k16 = q.astype(BF16), k.astype(BF16)
    eye = ci == ri
    a = [jnp.where(eye, _dot_nt(q16[:, sl], k16[:, sl]), 0.0) for sl in heads]
    s = c // 2
    while s >= 1:
        pos = row & (2 * s - 1)
        q_half = (pos < s) if rev else (pos >= s)
        if s >= 4:
            blk = bsum.reshape(c // (2 * s), 2 * s, hk)
            rr = s if rev else s - 1
            ref = jnp.broadcast_to(blk[:, rr:rr + 1, :], blk.shape).reshape(c, hk)
            e = jnp.where(q_half, bsum - ref, ref - bsum)
        elif s == 2:
            pos4 = row & 3
            nxt, prv = pltpu.roll(la, c - 1, axis=0), pltpu.roll(la, 1, axis=0)
            if rev:
                e = jnp.where(pos4 == 0, la + nxt, jnp.where(pos4 == 1, la, jnp.where(pos4 == 2, 0.0, prv)))
            else:
                e = jnp.where(pos4 == 3, la + prv, jnp.where(pos4 == 2, la, jnp.where(pos4 == 1, 0.0, nxt)))
        else:
            e = jnp.where(q_half, la, 0.0)
        f = jnp.exp(e)
        qd = jnp.where(q_half, q * f, 0.0).astype(BF16)
        kd = jnp.where(q_half, 0.0, k * f).astype(BF16)
        same_block = (ri & -(2 * s)) == (ci & -(2 * s))
        a = [a[hd] + jnp.where(same_block, _dot_nt(qd[:, sl], kd[:, sl]), 0.0) for hd, sl in enumerate(heads)]
        s //= 2

    qe = (q * jnp.exp(bsum)).astype(BF16)
    kt = (k * jnp.exp(tot - bsum)).astype(BF16)
    et = jnp.exp(tot)
    for hd, sl in enumerate(heads):
        vh = v[:, hd * GLA_DV:(hd + 1) * GLA_DV]
        st = st_ref[hd]
        o_ref[0, :, hd * GLA_DV:(hd + 1) * GLA_DV] = (_dot(a[hd].astype(BF16), vh.astype(BF16))
                                                       + _dot_nt(qe[:, sl], st.astype(BF16)))
        st_ref[hd] = st * et[:, sl] + _dot(vh.T.astype(BF16), kt[:, sl])


def _scan_chunk(c, n_ctx_chunks, n_chunks, rev):
    if not rev:
        return c
    return jnp.where(c < n_ctx_chunks, n_ctx_chunks - 1 - c, n_chunks - 1 - (c - n_ctx_chunks))


def gla_scan(qkvg, lo, wa_pad, ba, n_ctx):
    bsz, lt, _ = qkvg.shape
    c = GLA_CHUNK
    nch, ncc = lt // c, n_ctx // c
    hk, hv = GLA_HEADS * GLA_DK, GLA_HEADS * GLA_DV

    def chunk_specs(rev):
        cm = functools.partial(_scan_chunk, n_ctx_chunks=ncc, n_chunks=nch, rev=rev)
        ins = [pl.BlockSpec((1, c, hk), lambda b, i: (b, cm(i), 0)),
               pl.BlockSpec((1, c, hk), lambda b, i: (b, cm(i), 1)),
               pl.BlockSpec((1, c, hv), lambda b, i: (b, cm(i), 1)),
               pl.BlockSpec((1, c, LANE), lambda b, i: (b, cm(i), 0))]
        return ins, pl.BlockSpec((1, c, hv), lambda b, i: (b, cm(i), 0))

    (in_f, out_f), (in_b, out_b) = chunk_specs(False), chunk_specs(True)
    state = pltpu.VMEM((GLA_HEADS, GLA_DV, GLA_DK), F32)
    return pl.pallas_call(
        _gla_kernel,
        grid=(bsz, nch),
        in_specs=in_f + in_b + [pl.BlockSpec((2, LANE, hk), lambda b, i: (0, 0, 0)),
                                pl.BlockSpec((2, 1, hk), lambda b, i: (0, 0, 0))],
        out_specs=[out_f, out_b],
        out_shape=[jax.ShapeDtypeStruct((bsz, lt, hv), F32)] * 2,
        scratch_shapes=[state, state],
        compiler_params=_cp(("parallel", "arbitrary")),
        name="gla_scan",
    )(qkvg, qkvg, qkvg, lo, qkvg, qkvg, qkvg, lo, wa_pad, ba)


def _diff_kernel(lam_ref, q_ref, k_ref, v_ref, o_ref, *, n_ctx, lam_init):
    j = pl.program_id(2)
    lv = lam_ref[...]
    l1 = jnp.sum(lv[0:1] * lv[1:2], axis=1, keepdims=True)
    l2 = jnp.sum(lv[2:3] * lv[3:4], axis=1, keepdims=True)
    lam = jnp.exp(l1) - jnp.exp(l2) + lam_init

    lane = lax.broadcasted_iota(I32, (TM, LANE), 1)

    def attend(n_keys):
        for hh in range(DIFF_HPS):
            cols = slice(hh * LANE, (hh + 1) * LANE)
            q = q_ref[0, :, cols]
            zero = jnp.zeros_like(q)
            q0 = jnp.where(lane < DIFF_HD, q, zero)
            q1 = jnp.where(lane < DIFF_HD, zero, q)
            k = k_ref[0, :n_keys, cols]
            v = v_ref[0, :n_keys, cols]
            s0 = _dot_nt(q0, k)
            s1 = _dot_nt(q1, k)
            p0 = jnp.exp2(s0 - jnp.max(s0, axis=1, keepdims=True))
            p1 = jnp.exp2(s1 - jnp.max(s1, axis=1, keepdims=True))
            r0 = 1.0 / jnp.sum(p0, axis=1, keepdims=True)
            r1 = lam / jnp.sum(p1, axis=1, keepdims=True)
            o_ref[0, :, cols] = _dot(p0.astype(BF16), v) * r0 - _dot(p1.astype(BF16), v) * r1

    @pl.when(j * TM < n_ctx)
    def _():
        attend(n_ctx)

    @pl.when(j * TM >= n_ctx)
    def _():
        attend(k_ref.shape[1])


def diff_attention(qkv, lam_vecs, n_ctx, lam_init):
    bsz, lt, _ = qkv.shape
    nh = DIFF_HEADS // DIFF_HPS
    w = DIFF_HPS * LANE
    return pl.pallas_call(
        functools.partial(_diff_kernel, n_ctx=n_ctx, lam_init=lam_init),
        grid=(bsz, nh, lt // TM),
        in_specs=[pl.BlockSpec((4, DIFF_HD), lambda b, h, j: (0, 0)),
                  pl.BlockSpec((1, TM, w), lambda b, h, j: (b, j, h)),
                  pl.BlockSpec((1, lt, w), lambda b, h, j: (b, 0, nh + h)),
                  pl.BlockSpec((1, lt, w), lambda b, h, j: (b, 0, 2 * nh + h))],
        out_specs=pl.BlockSpec((1, TM, w), lambda b, h, j: (b, j, h)),
        out_shape=jax.ShapeDtypeStruct((bsz, lt, DIFF_HEADS * LANE), F32),
        compiler_params=_cp(("parallel", "parallel", "arbitrary")),
        name="diff_attention",
    )(lam_vecs, qkv, qkv, qkv)


def _conv_kernel(x_ref, p_ref, n_ref, w_ref, b_ref, o_ref, *, n_ctx_tiles, n_tiles):
    j = pl.program_id(1)
    first = jnp.logical_or(j == 0, j == n_ctx_tiles)
    last = jnp.logical_or(j == n_ctx_tiles - 1, j == n_tiles - 1)
    x = x_ref[0]
    prev = jnp.where(first, 0.0, p_ref[0])
    nxt = jnp.where(last, 0.0, n_ref[0])
    xe = jnp.concatenate([prev, x, nxt], axis=0)
    ne = xe.shape[0]
    half = SSD_CONV // 2
    y = b_ref[...] + w_ref[half:half + 1, :] * x
    for t in range(SSD_CONV):
        if t == half:
            continue
        sh = pltpu.roll(xe, (half - t) % ne, axis=0)[8:8 + TM]
        y = y + w_ref[t:t + 1, :] * sh
    o_ref[0] = _silu(y)


def ssd_conv(xbc, conv_w, conv_b, n_ctx):
    bsz, lt, ch = xbc.shape
    nt = lt // TM
    r8 = TM // 8
    return pl.pallas_call(
        functools.partial(_conv_kernel, n_ctx_tiles=n_ctx // TM, n_tiles=nt),
        grid=(bsz, nt),
        in_specs=[pl.BlockSpec((1, TM, ch), lambda b, j: (b, j, 0)),
                  pl.BlockSpec((1, 8, ch), lambda b, j: (b, jnp.maximum(j * r8 - 1, 0), 0)),
                  pl.BlockSpec((1, 8, ch), lambda b, j: (b, jnp.minimum((j + 1) * r8, lt // 8 - 1), 0)),
                  pl.BlockSpec((SSD_CONV, ch), lambda b, j: (0, 0)),
                  pl.BlockSpec((1, ch), lambda b, j: (0, 0))],
        out_specs=pl.BlockSpec((1, TM, ch), lambda b, j: (b, j, 0)),
        out_shape=jax.ShapeDtypeStruct((bsz, lt, ch), F32),
        compiler_params=_cp(("parallel", "parallel")),
        name="ssd_conv",
    )(xbc, xbc, xbc, conv_w, conv_b.reshape(1, ch))


def _ssd_kernel(xsf_ref, bmf_ref, cmf_ref, dtf_ref, xsb_ref, bmb_ref, cmb_ref, dtb_ref, bias_ref, al_ref, ex_ref,
                yf_ref, yb_ref, stf_ref, stb_ref):
    @pl.when(pl.program_id(1) == 0)
    def _():
        stf_ref[...] = jnp.zeros_like(stf_ref)
        stb_ref[...] = jnp.zeros_like(stb_ref)

    _ssd_chunk(xsf_ref, bmf_ref, cmf_ref, dtf_ref, bias_ref[:, :LANE], al_ref[:, :LANE], ex_ref, yf_ref, stf_ref,
               rev=False)
    _ssd_chunk(xsb_ref, bmb_ref, cmb_ref, dtb_ref, bias_ref[:, LANE:], al_ref[:, LANE:], ex_ref, yb_ref, stb_ref,
               rev=True)


def _ssd_chunk(xs_ref, bm_ref, cm_ref, dt_ref, dt_bias, a_log, ex_ref, y_ref, st_ref, *, rev):
    qn = SSD_CHUNK
    gw = SSD_R * SSD_P

    dt = _softplus(dt_ref[0] + dt_bias)
    da = dt * (-jnp.exp(a_log))
    ri = lax.broadcasted_iota(I32, (qn, qn), 0)
    ci = lax.broadcasted_iota(I32, (qn, qn), 1)
    causal = (ci >= ri) if rev else (ci <= ri)
    tri = jnp.where(causal, 1.0, 0.0).astype(BF16)
    acum = _dot_exact_rhs(tri, da)
    acum_t = acum.T
    dt_t = dt.T
    tot = acum[0:1, :] if rev else acum[qn - 1:qn, :]
    lane = lax.broadcasted_iota(I32, (qn, 2 * SSD_P), 1)
    e_acum = jnp.exp(acum)
    w_state = jnp.exp(tot - acum) * dt
    e_tot8 = jnp.broadcast_to(jnp.exp(tot), (8, LANE))

    for g in range(SSD_G):
        xs = xs_ref[0, :, g * gw:(g + 1) * gw]
        bm = bm_ref[0, :, g * SSD_N:(g + 1) * SSD_N]
        cmat = cm_ref[0, :, g * SSD_N:(g + 1) * SSD_N]
        ex = ex_ref[g]

        cb = _dot_nt(cmat.astype(BF16), bm.astype(BF16))
        xs16 = xs.astype(BF16)
        pieces = []
        for rp in range(SSD_R // 2):
            acc = None
            for sub in range(2):
                r = g * SSD_R + 2 * rp + sub
                seg = acum[:, r:r + 1] - acum_t[r:r + 1, :]
                w = cb * jnp.exp(jnp.minimum(seg, 0.0)) * dt_t[r:r + 1, :]
                w = jnp.where(causal, w, 0.0).astype(BF16)
                xpair = xs16[:, rp * 2 * SSD_P:(rp + 1) * 2 * SSD_P]
                keep = (lane < SSD_P) if sub == 0 else (lane >= SSD_P)
                part = _dot(w, jnp.where(keep, xpair, jnp.zeros_like(xpair)))
                acc = part if acc is None else acc + part
            pieces.append(acc)
        y = jnp.concatenate(pieces, axis=1)

        st = st_ref[g]
        e_i = _dot_01(e_acum, ex)
        y_ref[0, :, g * gw:(g + 1) * gw] = y + _dot(cmat.astype(BF16), st.astype(BF16)) * e_i
        wt = _dot_01(w_state, ex)
        e_tot = _dot_01(e_tot8, ex)[0:1, :]
        st_ref[g] = st * e_tot + _dot(bm.T.astype(BF16), (xs * wt).astype(BF16))


def ssd_scan(xc, dt, dtb, alog, expand, n_ctx):
    bsz, lt, _ = xc.shape
    qn = SSD_CHUNK
    nch, ncc = lt // qn, n_ctx // qn
    gw = SSD_R * SSD_P
    din, gn = SSD_HEADS * SSD_P, SSD_G * SSD_N

    def chunk_specs(rev):
        cm = functools.partial(_scan_chunk, n_ctx_chunks=ncc, n_chunks=nch, rev=rev)
        d = 1 if rev else 0
        ins = [pl.BlockSpec((1, qn, din), lambda b, i: (b, cm(i), 0)),
               pl.BlockSpec((1, qn, gn), lambda b, i: (b, cm(i), din // gn)),
               pl.BlockSpec((1, qn, gn), lambda b, i: (b, cm(i), din // gn + 1)),
               pl.BlockSpec((1, qn, LANE), lambda b, i: (b, cm(i), d))]
        return ins, pl.BlockSpec((1, qn, din), lambda b, i: (b, cm(i), 0))

    (in_f, out_f), (in_b, out_b) = chunk_specs(False), chunk_specs(True)
    state = pltpu.VMEM((SSD_G, SSD_N, gw), F32)
    return pl.pallas_call(
        _ssd_kernel,
        grid=(bsz, nch),
        in_specs=in_f + in_b + [pl.BlockSpec((1, 2 * LANE), lambda b, i: (0, 0)),
                                pl.BlockSpec((1, 2 * LANE), lambda b, i: (0, 0)),
                                pl.BlockSpec((SSD_G, LANE, gw), lambda b, i: (0, 0, 0))],
        out_specs=[out_f, out_b],
        out_shape=[jax.ShapeDtypeStruct((bsz, lt, din), F32)] * 2,
        scratch_shapes=[state, state],
        compiler_params=_cp(("parallel", "arbitrary")),
        name="ssd_scan",
    )(xc, xc, xc, dt, xc, xc, xc, dt, dtb, alog, expand)


SUBL = 8


def _vrow(ref, k):
    return ref[SUBL * k:SUBL * (k + 1), :]


def _top16(tasks):
    def body(r, carry):
        out = pl.ds(pl.multiple_of(r * SUBL, SUBL), SUBL)
        for s_ref, ids, vals_ref, idx_ref in tasks:
            rows = [_vrow(s_ref, k) for k in range(len(ids))]
            level = list(zip(rows, ids))
            while len(level) > 1:
                nxt = []
                for j in range(0, len(level) - 1, 2):
                    (va, ia), (vb, ib) = level[j], level[j + 1]
                    gt = vb > va
                    nxt.append((jnp.where(gt, vb, va), jnp.where(gt, ib, ia)))
                if len(level) % 2:
                    nxt.append(level[-1])
                level = nxt
            m, sel = level[0]
            for k, v in enumerate(rows):
                s_ref[SUBL * k:SUBL * (k + 1), :] = jnp.where(sel == ids[k], -jnp.inf, v)
            vals_ref[out, :] = m
            idx_ref[out, :] = sel
        return carry
    lax.fori_loop(0, PEER_TOPK, body, 0, unroll=2)


ROUTE_SUB = ROUTE_TM // LANE
assert ROUTE_SUB == SUBL
N_PAIRS = sum(1 for p in range(PEER_TOPK) for q in range(PEER_TOPK) if (p + 1) * (q + 1) <= PEER_TOPK)


def _peer_route_kernel(f_ref, wh_ref, wl_ref, kc_ref, a_ref, b_ref, g_ref,
                       q_ref, s1_ref, s2_ref, c_ref, v1_ref, i1_ref, v2_ref, i2_ref, vb_ref, ib_ref):
    h = pl.program_id(1)
    nk = PEER_KEYS
    rows_q = 256

    @pl.when(h == 0)
    def _():
        for rc in range(ROUTE_TM // rows_q):
            rs = slice(rc * rows_q, (rc + 1) * rows_q)
            fh, fl = _split2(f_ref[rs, :])
            q_ref[rs, :] = _dot(fh, wh_ref[...]) + _dot(fh, wl_ref[...]) + _dot(fl, wh_ref[...])

    for z, s_ref in enumerate((s1_ref, s2_ref)):
        col = pl.multiple_of((h * 2 + z) * nk, nk)
        kcat = kc_ref[z]
        for c in range(ROUTE_SUB):
            qh, ql = _split2(q_ref[c * LANE:(c + 1) * LANE, pl.ds(col, nk)])
            st = _dot_nt(kcat, jnp.concatenate([qh, ql, qh], axis=1))
            s_ref[pl.ds(c, nk, stride=ROUTE_SUB), :] = st
    keys = list(range(nk))
    _top16([(s1_ref, keys, v1_ref, i1_ref), (s2_ref, keys, v2_ref, i2_ref)])

    pairs = [(p, q) for p in range(PEER_TOPK) for q in range(PEER_TOPK) if (p + 1) * (q + 1) <= PEER_TOPK]
    for j, (p, q) in enumerate(pairs):
        c_ref[SUBL * j:SUBL * (j + 1), :] = _vrow(v1_ref, p) + _vrow(v2_ref, q)
    _top16([(c_ref, [p * PEER_TOPK + q for p, q in pairs], vb_ref, ib_ref)])

    fold3 = (PEER_TOPK, ROUTE_SUB, LANE)
    sel = ib_ref[...].reshape(fold3)
    p, qq = sel >> 4, sel & (PEER_TOPK - 1)
    i1, i2 = i1_ref[...].reshape(fold3), i2_ref[...].reshape(fold3)
    a = jnp.zeros_like(sel)
    b = jnp.zeros_like(sel)
    for t in range(PEER_TOPK):
        a = jnp.where(p == t, i1[t:t + 1], a)
        b = jnp.where(qq == t, i2[t:t + 1], b)
    vb = vb_ref[...].reshape(fold3)
    e = jnp.exp(vb - vb[0:1])
    a_ref[0] = a
    b_ref[0] = b
    g_ref[0] = e / jnp.sum(e, axis=0, keepdims=True)


def peer_route(f2, wq_hi, wq_lo, sk_cat):
    t, d = f2.shape
    nq = wq_hi.shape[1]
    k = PEER_TOPK
    tm = ROUTE_TM
    fold = (ROUTE_SUB, LANE)
    outspec = pl.BlockSpec((1, k) + fold, lambda i, h: (h, 0, i, 0))
    oshape = (PEER_HEADS, k, t // LANE, LANE)
    a, b, g = pl.pallas_call(
        _peer_route_kernel,
        grid=(t // tm, PEER_HEADS),
        in_specs=[pl.BlockSpec((tm, d), lambda i, h: (i, 0)),
                  pl.BlockSpec((d, nq), lambda i, h: (0, 0)),
                  pl.BlockSpec((d, nq), lambda i, h: (0, 0)),
                  pl.BlockSpec(sk_cat.shape, lambda i, h: (0, 0, 0))],
        out_specs=[outspec, outspec, outspec],
        out_shape=[jax.ShapeDtypeStruct(oshape, I32)] * 2 + [jax.ShapeDtypeStruct(oshape, F32)],
        scratch_shapes=[pltpu.VMEM((tm, nq), F32), pltpu.VMEM((PEER_KEYS * SUBL, LANE), F32),
                        pltpu.VMEM((PEER_KEYS * SUBL, LANE), F32), pltpu.VMEM((N_PAIRS * SUBL, LANE), F32),
                        pltpu.VMEM((k * SUBL, LANE), F32), pltpu.VMEM((k * SUBL, LANE), I32),
                        pltpu.VMEM((k * SUBL, LANE), F32), pltpu.VMEM((k * SUBL, LANE), I32),
                        pltpu.VMEM((k * SUBL, LANE), F32), pltpu.VMEM((k * SUBL, LANE), I32)],
        compiler_params=_cp(("parallel", "arbitrary")),
        name="peer_route",
    )(f2, wq_hi, wq_lo, sk_cat)
    return a.reshape(PEER_HEADS, k, t), b.reshape(PEER_HEADS, k, t), g.reshape(PEER_HEADS, k, t)


def _peer_dense_kernel(f_ref, h_ref, gm_ref, a_ref, b_ref, g_ref, u_ref, v_ref, o_ref,
                       f16_ref, ar_ref, br_ref, gr_ref, gs_ref, acc_ref, *, tiles_per_batch, n_ctx_tiles):
    i, e = pl.program_id(0), pl.program_id(1)
    nk = PEER_KEYS
    half = nk // 2
    hi_mask = jnp.uint32(0xFFFF0000)

    @pl.when(e == 0)
    def _():
        f16_ref[...] = f_ref[...].astype(BF16)
        acc_ref[...] = jnp.zeros_like(acc_ref)
        ar_ref[...] = a_ref[...].T
        br_ref[...] = b_ref[...].T
        gr_ref[...] = g_ref[...].T
        r = lax.broadcasted_iota(I32, (nk, nk), 0)
        key1 = jnp.where(r < half, 2 * r, 2 * (r - half) + 1)
        key2 = r

        def per_token(t, carry):
            arow = ar_ref[pl.ds(t, 1), :]
            brow = br_ref[pl.ds(t, 1), :]
            grow = 0.5 * gr_ref[pl.ds(t, 1), :]
            ga = jnp.where(key1 == arow, grow, 0.0).astype(BF16)
            ob = jnp.where(key2 == brow, 1.0, 0.0).astype(BF16)
            gm = _dot_nt(ga, ob).astype(BF16).astype(F32)
            bits = lax.bitcast_convert_type(gm, jnp.uint32)
            off = pl.multiple_of(t * G_PITCH, 8)
            gs_ref[pl.ds(off, half), :] = bits[:half] | (bits[half:] >> 16)
            return carry
        lax.fori_loop(0, PEER_TM, per_token, 0, unroll=128)

    f16 = f16_ref[...]
    sub = 2 * nk
    ws = []
    for c in range(PEER_TE // sub):
        s = _dot_nt(f16, u_ref[0, c * sub:(c + 1) * sub, :])
        word = gs_ref[pl.ds(e * (PEER_TE // sub) + c, PEER_TM, stride=G_PITCH), :]
        gt = jnp.concatenate([lax.bitcast_convert_type(word & hi_mask, F32),
                              lax.bitcast_convert_type(word << 16, F32)], axis=1)
        act = s * (1.0 + lax.erf(s * (2.0 ** -0.5)))
        ws.append((act * gt).astype(BF16))
    acc_ref[...] += _dot(jnp.concatenate(ws, axis=1), v_ref[0])

    @pl.when(e == pl.num_programs(1) - 1)
    def _():
        for part in range(PEER_TM // TM):
            j = i * (PEER_TM // TM) + part
            row = _mod_row(j // tiles_per_batch, j % tiles_per_batch, n_ctx_tiles)
            sl = slice(part * TM, (part + 1) * TM)
            o_ref[sl, :] = h_ref[sl, :] + gm_ref[0, pl.ds(row, 1), :] * acc_ref[sl, :]


def peer_dense(f2, h2, mods, layer, a_t, b_t, g_t, u16, v16, tiles_per_batch, n_ctx):
    t, d = f2.shape
    ne = u16.shape[1]
    hk = PEER_HEADS * PEER_TOPK
    kern = functools.partial(_peer_dense_kernel, tiles_per_batch=tiles_per_batch, n_ctx_tiles=n_ctx // TM)
    tok = lambda: pl.BlockSpec((PEER_TM, d), lambda i, e: (i, 0))
    rt = lambda: pl.BlockSpec((hk, PEER_TM), lambda i, e: (0, i))
    return pl.pallas_call(
        kern,
        grid=(t // PEER_TM, ne // PEER_TE),
        in_specs=[tok(), tok(), pl.BlockSpec((1, 8, d), lambda i, e: (layer, 0, 5)), rt(), rt(), rt(),
                  pl.BlockSpec((1, PEER_TE, d), lambda i, e: (layer, e, 0)),
                  pl.BlockSpec((1, PEER_TE, d), lambda i, e: (layer, e, 0))],
        out_specs=tok(),
        out_shape=jax.ShapeDtypeStruct((t, d), F32),
        scratch_shapes=[pltpu.VMEM((PEER_TM, d), BF16), pltpu.VMEM((PEER_TM, hk), I32),
                        pltpu.VMEM((PEER_TM, hk), I32), pltpu.VMEM((PEER_TM, hk), F32),
                        pltpu.VMEM((PEER_TM * G_PITCH, PEER_KEYS), jnp.uint32), pltpu.VMEM((PEER_TM, d), F32)],
        compiler_params=_cp(("parallel", "arbitrary")),
        name="peer_dense",
    )(f2, h2, mods, a_t, b_t, g_t, u16, v16)


def peer_ffn(f, h, mods, layer, wq, subkeys, u16, v16, n_ctx):
    bsz, lt, d = h.shape
    t = bsz * lt
    f2, h2 = f.reshape(t, d), h.reshape(t, d)
    wq_hi, wq_lo = _split2(wq)
    sk_hi, sk_lo = _split2(subkeys)
    a, b, g = peer_route(f2, wq_hi, wq_lo, jnp.concatenate([sk_hi, sk_hi, sk_lo], axis=-1))
    hk = PEER_HEADS * PEER_TOPK
    out = peer_dense(f2, h2, mods, layer, a.reshape(hk, t), b.reshape(hk, t), g.reshape(hk, t),
                     u16, v16, lt // TM, n_ctx)
    return out.reshape(bsz, lt, d)


def _final_kernel(h_ref, g_ref, o_ref):
    o_ref[0] = _rms(h_ref[0], g_ref[...])


def final_norm(h, g, n_ctx):
    bsz, lt, d = h.shape
    l = lt - n_ctx
    off = n_ctx // TM
    return pl.pallas_call(
        _final_kernel,
        grid=(bsz, l // TM),
        in_specs=[pl.BlockSpec((1, TM, d), lambda b, j: (b, j + off, 0)),
                  pl.BlockSpec((1, d), lambda b, j: (0, 0))],
        out_specs=pl.BlockSpec((1, TM, d), lambda b, j: (b, j, 0)),
        out_shape=jax.ShapeDtypeStruct((bsz, l, d), F32),
        compiler_params=_cp(("parallel", "parallel")),
        name="final_norm",
    )(h, g.reshape(1, d))


def _rope_tables(l, n_ctx):
    rows = l // GRID_W
    row = jnp.repeat(jnp.arange(rows), GRID_W).astype(F32)
    col = jnp.tile(jnp.arange(GRID_W), rows).astype(F32)
    n_freq = DIFF_HD // 4
    freqs = ROPE_THETA ** (-jnp.arange(n_freq, dtype=F32) / n_freq)
    ang = jnp.concatenate([row[:, None] * freqs, col[:, None] * freqs], axis=-1)
    cos, sin = jnp.cos(ang), jnp.sin(ang)
    cos = jnp.concatenate([jnp.ones((n_ctx, DIFF_HD // 2), F32), cos], axis=0)
    sin = jnp.concatenate([jnp.zeros((n_ctx, DIFF_HD // 2), F32), sin], axis=0)
    cos_t = jnp.concatenate([cos, cos, cos, cos], axis=1)
    sin_t = jnp.concatenate([-sin, sin, -sin, sin], axis=1)
    return cos_t, sin_t


def _gla_layer(h, mods, i, j, p, n_ctx):
    hk = GLA_HEADS * GLA_DK
    hv = GLA_HEADS * GLA_DV
    w_in = p["gla_w_in"][j]
    n_main = 2 * hk + 2 * hv
    w_lo = jnp.pad(w_in[:, n_main:], ((0, 0), (0, LANE - 2 * GLA_RANK))).astype(BF16)
    qkvg, lo = inproj(h, p["norm1_g"][i], mods, i, 0, [w_in[:, :n_main].astype(BF16), w_lo], n_ctx)
    wa = p["gla_w_alpha"][j]
    wa_pad = jnp.stack([jnp.pad(wa[0], ((0, LANE - GLA_RANK), (0, 0))),
                        jnp.pad(wa[1], ((GLA_RANK, LANE - 2 * GLA_RANK), (0, 0)))])
    ba = p["gla_b_alpha"][j].reshape(2, 1, hk)
    o_f, o_b = gla_scan(qkvg, lo, wa_pad, ba, n_ctx)
    tile = lambda: pl.BlockSpec((1, TM, hv), lambda b, t: (b, t, 0))
    specs = [tile(), tile(), pl.BlockSpec((1, TM, hv), lambda b, t: (b, t, 2)),
             pl.BlockSpec((1, GLA_DV), lambda b, t: (0, 0))]
    args = [o_f, o_b, qkvg, p["gla_norm_g"][j].reshape(1, GLA_DV)]
    return finish("gla", args, specs, p["gla_w_out"][j].astype(BF16), h, p["norm2_g"][i], mods, i, n_ctx)


def _diff_layer(h, mods, i, j, p, n_ctx, rope_tabs):
    wd = DIFF_HEADS * 2 * DIFF_HD
    cos_t, sin_t = rope_tabs
    lt = h.shape[1]
    qkv, = inproj(h, p["norm1_g"][i], mods, i, 0, [p["diff_w_in"][j].astype(BF16)], n_ctx,
                  out_dtype=BF16, rope=(cos_t, sin_t, 2 * wd, wd, DIFF_HD ** -0.5 * math.log2(math.e)))
    lam_init = 0.8 - 0.6 * math.exp(-0.3 * i)
    o = diff_attention(qkv, p["diff_lambda"][j], n_ctx, lam_init)
    specs = [pl.BlockSpec((1, TM, wd), lambda b, t: (b, t, 0)),
             pl.BlockSpec((1, 2 * DIFF_HD), lambda b, t: (0, 0))]
    args = [o, p["diff_norm_g"][j].reshape(1, 2 * DIFF_HD)]
    return finish("diff", args, specs, p["diff_w_out"][j].astype(BF16), h, p["norm2_g"][i], mods, i, n_ctx,
                  prm={"lam_init": lam_init})


def _ssd_layer(h, mods, i, j, p, n_ctx):
    din = SSD_HEADS * SSD_P
    gn = SSD_G * SSD_N
    w_in = p["ssd_w_in"][j]
    g1 = p["norm1_g"][i]
    w_dt = w_in[:, 2 * din + 2 * gn:].reshape(-1, 2, SSD_HEADS)
    w_dt = jnp.pad(w_dt, ((0, 0), (0, 0), (0, LANE - SSD_HEADS))).reshape(-1, 2 * LANE)
    z, xbc, dt = inproj(h, g1, mods, i, 0, [w_in[:, :din].astype(BF16),
                                              w_in[:, din:2 * din + 2 * gn].astype(BF16), w_dt.astype(BF16)], n_ctx)
    pad_h = lambda a: jnp.pad(a, ((0, 0), (0, LANE - SSD_HEADS))).reshape(1, -1)
    dtb, alog = pad_h(p["ssd_dt_bias"][j]), pad_h(p["ssd_a_log"][j])
    xc = ssd_conv(xbc, p["ssd_conv_w"][j], p["ssd_conv_b"][j], n_ctx)
    head_of_col = jnp.arange(SSD_G)[:, None, None] * SSD_R + jnp.arange(SSD_R * SSD_P)[None, None, :] // SSD_P
    expand = (jnp.arange(LANE)[None, :, None] == head_of_col).astype(BF16)
    y_f, y_b = ssd_scan(xc, dt, dtb, alog, expand, n_ctx)
    tile = lambda: pl.BlockSpec((1, TM, din), lambda b, t: (b, t, 0))
    row = lambda: pl.BlockSpec((1, din), lambda b, t: (0, 0))
    specs = [tile(), tile(), tile(), tile(), row(), row()]
    dexp = jnp.repeat(p["ssd_d"][j], SSD_P).reshape(1, din)
    args = [y_f, y_b, xc, z, dexp, p["ssd_norm_g"][j].reshape(1, din)]
    return finish("ssd", args, specs, p["ssd_w_out"][j].astype(BF16), h, p["norm2_g"][i], mods, i, n_ctx)


def kernel(x, c, ctx, c_ctx, norm1_g, norm2_g, w_mod, b_mod, peer_wq, peer_subkeys, peer_u, peer_v, gla_w_in, gla_w_alpha, gla_b_alpha, gla_norm_g, gla_w_out, diff_w_in, diff_lambda, diff_norm_g, diff_w_out, ssd_w_in, ssd_conv_w, ssd_conv_b, ssd_dt_bias, ssd_a_log, ssd_d, ssd_norm_g, ssd_w_out, final_g):
    p = dict(norm1_g=norm1_g, norm2_g=norm2_g, gla_w_in=gla_w_in, gla_w_alpha=gla_w_alpha, gla_b_alpha=gla_b_alpha,
             gla_norm_g=gla_norm_g, gla_w_out=gla_w_out, diff_w_in=diff_w_in, diff_lambda=diff_lambda,
             diff_norm_g=diff_norm_g, diff_w_out=diff_w_out, ssd_w_in=ssd_w_in, ssd_conv_w=ssd_conv_w,
             ssd_conv_b=ssd_conv_b, ssd_dt_bias=ssd_dt_bias, ssd_a_log=ssd_a_log, ssd_d=ssd_d,
             ssd_norm_g=ssd_norm_g, ssd_w_out=ssd_w_out)
    bsz, l, d = x.shape
    n_ctx = ctx.shape[1]
    depth = w_mod.shape[0]
    assert bsz <= 4 and n_ctx % TM == 0 and l % TM == 0 and l % GRID_W == 0
    cond8 = jnp.concatenate([c, jnp.zeros((4 - bsz, d), F32), c_ctx[None], jnp.zeros((3, d), F32)], axis=0)
    mods = mod_table(cond8, w_mod, b_mod)
    rope_tabs = _rope_tables(l, n_ctx)
    u16, v16 = peer_u.astype(BF16), peer_v.astype(BF16)
    h = jnp.concatenate([ctx, x], axis=1)
    for i in range(depth):
        kind, j = i % N_MIXERS, i // N_MIXERS
        if kind == 0:
            h, f = _gla_layer(h, mods, i, j, p, n_ctx)
        elif kind == 1:
            h, f = _diff_layer(h, mods, i, j, p, n_ctx, rope_tabs)
        else:
            h, f = _ssd_layer(h, mods, i, j, p, n_ctx)
        h = peer_ffn(f, h, mods, i, peer_wq[i], peer_subkeys[i], u16, v16, n_ctx)
    return final_norm(h, final_g, n_ctx)
```
